```python
import jax, jax.numpy as jnp
from jax import lax
import numpy as np

D_MODEL = 2048
BATCH = 8
SEQ = 8192
DEPTH = 4

N_MIXERS = 2
FOX_HEADS = 16
FOX_HEAD_DIM = D_MODEL // FOX_HEADS
FOX_WIDTH = FOX_HEADS * FOX_HEAD_DIM
Q_BLOCK = 128
CONV_CHANNELS = D_MODEL
CONV_KERNEL = 31
RMS_EPS = 1e-6
LN_EPS = 1e-5
N_FOX = (DEPTH + 1) // 2
N_CONV = DEPTH // 2

kernel_name = 'hybrid_fox_conformer_conv_trunk'


def rmsnorm(x, g):
    xf = x.astype(jnp.float32)
    y = xf * lax.rsqrt(jnp.mean(xf * xf, axis=-1, keepdims=True) + RMS_EPS) * g.astype(jnp.float32)
    return y.astype(x.dtype)


def layernorm(x, g, b):
    xf = x.astype(jnp.float32)
    mu = jnp.mean(xf, axis=-1, keepdims=True)
    var = jnp.mean(jnp.square(xf - mu), axis=-1, keepdims=True)
    y = (xf - mu) * lax.rsqrt(var + LN_EPS) * g.astype(jnp.float32) + b.astype(jnp.float32)
    return y.astype(x.dtype)


def fox_mixer(h, w_in, b_f, w_out):
    B, S, _ = h.shape
    W, H, Dh = FOX_WIDTH, FOX_HEADS, FOX_HEAD_DIM
    proj = h @ w_in
    q, k, v, gate, f_logit = jnp.split(proj, [W, 2 * W, 3 * W, 4 * W], axis=-1)
    q = q.reshape(B, S, H, Dh).transpose(0, 2, 1, 3)
    kf = k.reshape(B, S, H, Dh).transpose(0, 2, 1, 3).astype(jnp.float32)
    v = v.reshape(B, S, H, Dh).transpose(0, 2, 1, 3)
    log_f = jax.nn.log_sigmoid((f_logit + b_f).astype(jnp.float32))
    c = jnp.cumsum(log_f, axis=1).transpose(0, 2, 1)
    nb = S // Q_BLOCK
    q_blocks = q.reshape(B, H, nb, Q_BLOCK, Dh).transpose(2, 0, 1, 3, 4)
    c_blocks = c.reshape(B, H, nb, Q_BLOCK).transpose(2, 0, 1, 3)
    starts = jnp.arange(nb, dtype=jnp.int32) * Q_BLOCK
    k_pos = jnp.arange(S, dtype=jnp.int32)
    scale = FOX_HEAD_DIM ** -0.5

    def attend(args):
        q_blk, c_blk, start = args
        s = jnp.einsum('bhqd,bhkd->bhqk', q_blk.astype(jnp.float32), kf) * scale
        s = s + c_blk[..., :, None] - c[:, :, None, :]
        q_pos = start + jnp.arange(Q_BLOCK, dtype=jnp.int32)
        s = jnp.where(q_pos[:, None] >= k_pos[None, :], s, -jnp.inf)
        p = jax.nn.softmax(s, axis=-1)
        return jnp.einsum('bhqk,bhkd->bhqd', p.astype(v.dtype), v)

    o = lax.map(attend, (q_blocks, c_blocks, starts))
    o = o.transpose(1, 0, 3, 2, 4).reshape(B, S, W)
    y = o * jax.nn.silu(gate)
    return y @ w_out


def conv_mixer(h, w_in, b_in, dw, dw_b, ln_g, ln_b, w_out):
    C = CONV_CHANNELS
    proj = h @ w_in + b_in
    a, b, gate = jnp.split(proj, [C, 2 * C], axis=-1)
    u = a * jax.nn.sigmoid(b)
    u = lax.conv_general_dilated(
        u, dw[:, None, :].astype(u.dtype), window_strides=(1,),
        padding=[(CONV_KERNEL - 1, 0)],
        dimension_numbers=('NWC', 'WIO', 'NWC'),
        feature_group_count=C) + dw_b
    u = jax.nn.silu(layernorm(u, ln_g, ln_b))
    y = u * jax.nn.silu(gate)
    return y @ w_out


def _fwd_setup_inputs(seed: int = 0) -> dict:
    key = jax.random.key(seed)
    ks = jax.random.split(key, 16)
    D, W, H, C, K = D_MODEL, FOX_WIDTH, FOX_HEADS, CONV_CHANNELS, CONV_KERNEL
    nrm = jax.random.normal
    return {
        'x': nrm(ks[0], (BATCH, SEQ, D), jnp.float32),
        'norm_g': 1.0 + 0.02 * nrm(ks[1], (DEPTH, D), jnp.float32),
        'fox_w_in': nrm(ks[2], (N_FOX, D, 4 * W + H), jnp.float32) * D ** -0.5,
        'fox_b_f': 2.0 + 0.5 * nrm(ks[3], (N_FOX, H), jnp.float32),
        'fox_w_out': nrm(ks[4], (N_FOX, W, D), jnp.float32) * W ** -0.5,
        'conv_w_in': nrm(ks[5], (N_CONV, D, 3 * C), jnp.float32) * D ** -0.5,
        'conv_b_in': 0.02 * nrm(ks[6], (N_CONV, 3 * C), jnp.float32),
        'conv_dw': nrm(ks[7], (N_CONV, K, C), jnp.float32) * K ** -0.5,
        'conv_dw_b': 0.02 * nrm(ks[8], (N_CONV, C), jnp.float32),
        'conv_ln_g': 1.0 + 0.02 * nrm(ks[9], (N_CONV, C), jnp.float32),
        'conv_ln_b': 0.02 * nrm(ks[10], (N_CONV, C), jnp.float32),
        'conv_w_out': nrm(ks[11], (N_CONV, C, D), jnp.float32) * C ** -0.5,
        'final_norm_g': 1.0 + 0.02 * nrm(ks[12], (D,), jnp.float32),
    }


def _fwd_reference(x, norm_g, fox_w_in, fox_b_f, fox_w_out, conv_w_in, conv_b_in, conv_dw,
              conv_dw_b, conv_ln_g, conv_ln_b, conv_w_out, final_norm_g):
    h = x
    for i in range(DEPTH):
        hn = rmsnorm(h, norm_g[i])
        j = i // N_MIXERS
        if i % N_MIXERS == 0:
            h = h + fox_mixer(hn, fox_w_in[j], fox_b_f[j], fox_w_out[j])
        else:
            h = h + conv_mixer(hn, conv_w_in[j], conv_b_in[j], conv_dw[j], conv_dw_b[j],
                               conv_ln_g[j], conv_ln_b[j], conv_w_out[j])
    return rmsnorm(h, final_norm_g)


import jax as _jax
import jax.numpy as _jnp

TWIN_FORMAT = 'train_step'
FWD_PARAMS = ['x', 'norm_g', 'fox_w_in', 'fox_b_f', 'fox_w_out', 'conv_w_in', 'conv_b_in', 'conv_dw', 'conv_dw_b', 'conv_ln_g', 'conv_ln_b', 'conv_w_out', 'final_norm_g']
TWIN_WEIGHTS = ['norm_g', 'fox_w_in', 'fox_b_f', 'fox_w_out', 'conv_w_in', 'conv_b_in', 'conv_dw', 'conv_dw_b', 'conv_ln_g', 'conv_ln_b', 'conv_w_out', 'final_norm_g']
TWIN_DIFF_INPUT = 'x'
TWIN_INPUTS = ['x', 'norm_g', 'fox_w_in', 'fox_b_f', 'fox_w_out', 'conv_w_in', 'conv_b_in', 'conv_dw', 'conv_dw_b', 'conv_ln_g', 'conv_ln_b', 'conv_w_out', 'final_norm_g', 'loss_target', 'm_norm_g', 'm_fox_w_in', 'm_fox_b_f', 'm_fox_w_out', 'm_conv_w_in', 'm_conv_b_in', 'm_conv_dw', 'm_conv_dw_b', 'm_conv_ln_g', 'm_conv_ln_b', 'm_conv_w_out', 'm_final_norm_g', 'v_norm_g', 'v_fox_w_in', 'v_fox_b_f', 'v_fox_w_out', 'v_conv_w_in', 'v_conv_b_in', 'v_conv_dw', 'v_conv_dw_b', 'v_conv_ln_g', 'v_conv_ln_b', 'v_conv_w_out', 'v_final_norm_g']
TWIN_OUTPUTS = ['loss', 'grad_x', 'grad_norm_g', 'grad_fox_w_in', 'grad_fox_b_f', 'grad_fox_w_out', 'grad_conv_w_in', 'grad_conv_b_in', 'grad_conv_dw', 'grad_conv_dw_b', 'grad_conv_ln_g', 'grad_conv_ln_b', 'grad_conv_w_out', 'grad_final_norm_g', 'delta_norm_g', 'delta_fox_w_in', 'delta_fox_b_f', 'delta_fox_w_out', 'delta_conv_w_in', 'delta_conv_b_in', 'delta_conv_dw', 'delta_conv_dw_b', 'delta_conv_ln_g', 'delta_conv_ln_b', 'delta_conv_w_out', 'delta_final_norm_g', 'new_m_norm_g', 'new_m_fox_w_in', 'new_m_fox_b_f', 'new_m_fox_w_out', 'new_m_conv_w_in', 'new_m_conv_b_in', 'new_m_conv_dw', 'new_m_conv_dw_b', 'new_m_conv_ln_g', 'new_m_conv_ln_b', 'new_m_conv_w_out', 'new_m_final_norm_g', 'new_v_norm_g', 'new_v_fox_w_in', 'new_v_fox_b_f', 'new_v_fox_w_out', 'new_v_conv_w_in', 'new_v_conv_b_in', 'new_v_conv_dw', 'new_v_conv_dw_b', 'new_v_conv_ln_g', 'new_v_conv_ln_b', 'new_v_conv_w_out', 'new_v_final_norm_g']
TWIN_LEAF_KINDS = {'loss': 'loss', 'grad_x': 'grad_x', 'grad_norm_g': 'grad_w', 'grad_fox_w_in': 'grad_w', 'grad_fox_b_f': 'grad_w', 'grad_fox_w_out': 'grad_w', 'grad_conv_w_in': 'grad_w', 'grad_conv_b_in': 'grad_w', 'grad_conv_dw': 'grad_w', 'grad_conv_dw_b': 'grad_w', 'grad_conv_ln_g': 'grad_w', 'grad_conv_ln_b': 'grad_w', 'grad_conv_w_out': 'grad_w', 'grad_final_norm_g': 'grad_w', 'delta_norm_g': 'delta_w', 'delta_fox_w_in': 'delta_w', 'delta_fox_b_f': 'delta_w', 'delta_fox_w_out': 'delta_w', 'delta_conv_w_in': 'delta_w', 'delta_conv_b_in': 'delta_w', 'delta_conv_dw': 'delta_w', 'delta_conv_dw_b': 'delta_w', 'delta_conv_ln_g': 'delta_w', 'delta_conv_ln_b': 'delta_w', 'delta_conv_w_out': 'delta_w', 'delta_final_norm_g': 'delta_w', 'new_m_norm_g': 'new_m', 'new_m_fox_w_in': 'new_m', 'new_m_fox_b_f': 'new_m', 'new_m_fox_w_out': 'new_m', 'new_m_conv_w_in': 'new_m', 'new_m_conv_b_in': 'new_m', 'new_m_conv_dw': 'new_m', 'new_m_conv_dw_b': 'new_m', 'new_m_conv_ln_g': 'new_m', 'new_m_conv_ln_b': 'new_m', 'new_m_conv_w_out': 'new_m', 'new_m_final_norm_g': 'new_m', 'new_v_norm_g': 'new_v', 'new_v_fox_w_in': 'new_v', 'new_v_fox_b_f': 'new_v', 'new_v_fox_w_out': 'new_v', 'new_v_conv_w_in': 'new_v', 'new_v_conv_b_in': 'new_v', 'new_v_conv_dw': 'new_v', 'new_v_conv_dw_b': 'new_v', 'new_v_conv_ln_g': 'new_v', 'new_v_conv_ln_b': 'new_v', 'new_v_conv_w_out': 'new_v', 'new_v_final_norm_g': 'new_v'}


def _forward(args):
    return _fwd_reference(*[args[k] for k in FWD_PARAMS])


def _output_shape():
    def fwd():
        inp = _fwd_setup_inputs(0)
        return _fwd_reference(*[inp[k] for k in FWD_PARAMS])
    out = _jax.eval_shape(fwd)
    return out.shape, out.dtype

N_MICROBATCH = 1
ADAM_LR = 0.001
ADAM_B1 = 0.9
ADAM_B2 = 0.999
ADAM_EPS = 1e-08
ADAM_WD = 0.01
ADAM_STEP = 10
PER_EXAMPLE_BATCH_AXIS = {'x': 0, 'loss_target': 0}
SHARED_INPUTS = []
_WEIGHT_DTYPES = {'norm_g': _jnp.float32, 'fox_w_in': _jnp.float32, 'fox_b_f': _jnp.float32, 'fox_w_out': _jnp.float32, 'conv_w_in': _jnp.float32, 'conv_b_in': _jnp.float32, 'conv_dw': _jnp.float32, 'conv_dw_b': _jnp.float32, 'conv_ln_g': _jnp.float32, 'conv_ln_b': _jnp.float32, 'conv_w_out': _jnp.float32, 'final_norm_g': _jnp.float32}
MOMENT_SCALE = {'norm_g': 6.325650e-02, 'fox_w_in': 3.043892e-02, 'fox_b_f': 2.857801e-01, 'fox_w_out': 3.433144e-02, 'conv_w_in': 3.696499e-02, 'conv_b_in': 3.931084e-02, 'conv_dw': 4.372073e-02, 'conv_dw_b': 9.225928e-02, 'conv_ln_g': 5.344503e-02, 'conv_ln_b': 4.548417e-02, 'conv_w_out': 4.244255e-02, 'final_norm_g': 3.198048e+01}


def _to_microbatches(a, axis):
    t = _jnp.moveaxis(a, axis, 0)
    t = t.reshape((N_MICROBATCH, t.shape[0] // N_MICROBATCH) + t.shape[1:])
    return _jnp.moveaxis(t, 1, axis + 1)


def setup_inputs(seed: int = 0) -> dict:
    inp = _fwd_setup_inputs(seed)
    key = _jax.random.fold_in(_jax.random.key(seed), 7919)
    shape, _ = _output_shape()
    out = dict(inp)
    out["loss_target"] = _jax.random.normal(_jax.random.fold_in(key, 0), shape, _jnp.float32)
    for i, name in enumerate(TWIN_WEIGHTS):
        w = inp[name].astype(_jnp.float32)
        if MOMENT_SCALE is None:
            s = _jnp.sqrt(_jnp.mean(_jnp.square(w)) + 1e-30)
        else:
            s = MOMENT_SCALE[name]
        km, kv = _jax.random.split(_jax.random.fold_in(key, i + 1))
        out[name] = w
        out["m_" + name] = s * _jax.random.normal(km, w.shape, _jnp.float32)
        out["v_" + name] = (s * s) * _jax.random.uniform(kv, w.shape, _jnp.float32, 0.5, 1.5)
    if N_MICROBATCH > 1:
        for name, axis in PER_EXAMPLE_BATCH_AXIS.items():
            out[name] = _to_microbatches(out[name], axis)
    return {'x': out['x'], 'norm_g': out['norm_g'], 'fox_w_in': out['fox_w_in'], 'fox_b_f': out['fox_b_f'], 'fox_w_out': out['fox_w_out'], 'conv_w_in': out['conv_w_in'], 'conv_b_in': out['conv_b_in'], 'conv_dw': out['conv_dw'], 'conv_dw_b': out['conv_dw_b'], 'conv_ln_g': out['conv_ln_g'], 'conv_ln_b': out['conv_ln_b'], 'conv_w_out': out['conv_w_out'], 'final_norm_g': out['final_norm_g'], 'loss_target': out['loss_target'], 'm_norm_g': out['m_norm_g'], 'm_fox_w_in': out['m_fox_w_in'], 'm_fox_b_f': out['m_fox_b_f'], 'm_fox_w_out': out['m_fox_w_out'], 'm_conv_w_in': out['m_conv_w_in'], 'm_conv_b_in': out['m_conv_b_in'], 'm_conv_dw': out['m_conv_dw'], 'm_conv_dw_b': out['m_conv_dw_b'], 'm_conv_ln_g': out['m_conv_ln_g'], 'm_conv_ln_b': out['m_conv_ln_b'], 'm_conv_w_out': out['m_conv_w_out'], 'm_final_norm_g': out['m_final_norm_g'], 'v_norm_g': out['v_norm_g'], 'v_fox_w_in': out['v_fox_w_in'], 'v_fox_b_f': out['v_fox_b_f'], 'v_fox_w_out': out['v_fox_w_out'], 'v_conv_w_in': out['v_conv_w_in'], 'v_conv_b_in': out['v_conv_b_in'], 'v_conv_dw': out['v_conv_dw'], 'v_conv_dw_b': out['v_conv_dw_b'], 'v_conv_ln_g': out['v_conv_ln_g'], 'v_conv_ln_b': out['v_conv_ln_b'], 'v_conv_w_out': out['v_conv_w_out'], 'v_final_norm_g': out['v_final_norm_g']}


def _loss(weights, diff, rest, loss_target):
    with _jax.named_scope("forward"):
        args = {**rest, TWIN_DIFF_INPUT: diff, **{k: w.astype(_WEIGHT_DTYPES[k]) for k, w in weights.items()}}
        y = _forward(args)
    with _jax.named_scope("loss_head"):
        err = _jnp.square(y.astype(_jnp.float32) - loss_target)
        return 0.5 * _jnp.sum(_jnp.mean(err, axis=-1)) if err.ndim else 0.5 * err


def _adamw(w, g, m, v):
    m = ADAM_B1 * m + (1.0 - ADAM_B1) * g
    v = ADAM_B2 * v + (1.0 - ADAM_B2) * _jnp.square(g)
    m_hat = m / (1.0 - ADAM_B1 ** ADAM_STEP)
    v_hat = v / (1.0 - ADAM_B2 ** ADAM_STEP)
    delta = -ADAM_LR * (m_hat / (_jnp.sqrt(v_hat) + ADAM_EPS) + ADAM_WD * w)
    return delta, m, v


def reference(x, norm_g, fox_w_in, fox_b_f, fox_w_out, conv_w_in, conv_b_in, conv_dw, conv_dw_b, conv_ln_g, conv_ln_b, conv_w_out, final_norm_g, loss_target, m_norm_g, m_fox_w_in, m_fox_b_f, m_fox_w_out, m_conv_w_in, m_conv_b_in, m_conv_dw, m_conv_dw_b, m_conv_ln_g, m_conv_ln_b, m_conv_w_out, m_final_norm_g, v_norm_g, v_fox_w_in, v_fox_b_f, v_fox_w_out, v_conv_w_in, v_conv_b_in, v_conv_dw, v_conv_dw_b, v_conv_ln_g, v_conv_ln_b, v_conv_w_out, v_final_norm_g):
    given = dict(x=x, norm_g=norm_g, fox_w_in=fox_w_in, fox_b_f=fox_b_f, fox_w_out=fox_w_out, conv_w_in=conv_w_in, conv_b_in=conv_b_in, conv_dw=conv_dw, conv_dw_b=conv_dw_b, conv_ln_g=conv_ln_g, conv_ln_b=conv_ln_b, conv_w_out=conv_w_out, final_norm_g=final_norm_g, loss_target=loss_target, m_norm_g=m_norm_g, m_fox_w_in=m_fox_w_in, m_fox_b_f=m_fox_b_f, m_fox_w_out=m_fox_w_out, m_conv_w_in=m_conv_w_in, m_conv_b_in=m_conv_b_in, m_conv_dw=m_conv_dw, m_conv_dw_b=m_conv_dw_b, m_conv_ln_g=m_conv_ln_g, m_conv_ln_b=m_conv_ln_b, m_conv_w_out=m_conv_w_out, m_final_norm_g=m_final_norm_g, v_norm_g=v_norm_g, v_fox_w_in=v_fox_w_in, v_fox_b_f=v_fox_b_f, v_fox_w_out=v_fox_w_out, v_conv_w_in=v_conv_w_in, v_conv_b_in=v_conv_b_in, v_conv_dw=v_conv_dw, v_conv_dw_b=v_conv_dw_b, v_conv_ln_g=v_conv_ln_g, v_conv_ln_b=v_conv_ln_b, v_conv_w_out=v_conv_w_out, v_final_norm_g=v_final_norm_g)
    weights = {n: given[n] for n in TWIN_WEIGHTS}
    shared = {n: given[n] for n in SHARED_INPUTS}
    per_example = {n: given[n] for n in ['x']}
    grad_fn = _jax.value_and_grad(_loss, argnums=(0, 1))

    def one_microbatch(ex, loss_target):
        ex = dict(ex)
        diff = ex.pop(TWIN_DIFF_INPUT)
        return grad_fn(weights, diff, {**shared, **ex}, loss_target)

    if N_MICROBATCH == 1:
        loss, (grad_w, grad_x) = one_microbatch(per_example, given["loss_target"])
    else:
        def body(carry, xs):
            loss_sum, grad_sum = carry
            l_k, (gw_k, gx_k) = one_microbatch(xs[0], xs[1])
            with _jax.named_scope("update"):
                return (loss_sum + l_k, _jax.tree.map(_jnp.add, grad_sum, gw_k)), gx_k

        init = (_jnp.zeros((), _jnp.float32), _jax.tree.map(_jnp.zeros_like, weights))
        (loss, grad_w), grad_x = _jax.lax.scan(body, init, (per_example, given["loss_target"]))
    with _jax.named_scope("update"):
        delta_w, new_m, new_v = {}, {}, {}
        for n in TWIN_WEIGHTS:
            delta_w[n], new_m[n], new_v[n] = _adamw(weights[n], grad_w[n], given["m_" + n], given["v_" + n])
    return (loss, grad_x, *[grad_w[n] for n in TWIN_WEIGHTS], *[delta_w[n] for n in TWIN_WEIGHTS],
            *[new_m[n] for n in TWIN_WEIGHTS], *[new_v[n] for n in TWIN_WEIGHTS])
```

```python
import functools

import numpy as np
import jax
import jax.numpy as jnp
from jax import lax
from jax.experimental import pallas as pl
from jax.experimental.pallas import tpu as pltpu

F32 = jnp.float32
BF16 = jnp.bfloat16
MESH_ID = pl.DeviceIdType.MESH

N_DEV = 8
RMS_EPS = 1e-6
LN_EPS = 1e-5
ADAM_LR = 0.001
ADAM_B1 = 0.9
ADAM_B2 = 0.999
ADAM_EPS = 1e-08
ADAM_WD = 0.01
ADAM_STEP = 10

LANES = 128
SUBLANES = 8
VMEM_LIMIT = 56 * 1024 * 1024
NEG_BIG = -1e30
CONV_HALO = 32
FLASH_TILE = 512
CONV_ROWS = 128
FGATE_ROWS = 256
ROW_TILE = 256


def _params(*sem):
    return pltpu.CompilerParams(dimension_semantics=sem if sem else None, vmem_limit_bytes=VMEM_LIMIT)


def _tile(n, pref):
    if n <= pref:
        return n
    t = pref
    while n % t:
        t //= 2
    return t


def _sigmoid(x):
    return 1.0 / (1.0 + jnp.exp(-x))


def _dsilu(x, s):
    return s * (1.0 + x * (1.0 - s))


def _rows(n, chunk, fn):
    def step(i, carry):
        fn(pl.multiple_of(i * chunk, chunk))
        return carry
    lax.fori_loop(0, n // chunk, step, 0)


def _peer(k):
    x, y, c = lax.axis_index("x"), lax.axis_index("y"), lax.axis_index("c")
    px = 1 - x if (k >> 2) & 1 else x
    py = 1 - y if (k >> 1) & 1 else y
    pc = 1 - c if k & 1 else c
    return (px, py, pc), 4 * px + 2 * py + pc


def _exchange(x, name, gather):
    out_shape = (N_DEV,) + x.shape if gather else x.shape

    def body(x_ref, o_ref, send_sems, recv_sems, local_sem):
        _, me = _peer(0)
        mine = x_ref if gather else x_ref.at[me]
        local = pltpu.make_async_copy(mine, o_ref.at[me], local_sem)
        local.start()
        sends = []
        for k in range(1, N_DEV):
            peer, pidx = _peer(k)
            cp = pltpu.make_async_remote_copy(
                src_ref=x_ref if gather else x_ref.at[pidx], dst_ref=o_ref.at[me],
                send_sem=send_sems.at[k - 1], recv_sem=recv_sems.at[k - 1],
                device_id=peer, device_id_type=MESH_ID)
            cp.start()
            sends.append(cp)
        for k in range(1, N_DEV):
            peer, pidx = _peer(k)
            pltpu.make_async_remote_copy(
                src_ref=mine, dst_ref=o_ref.at[pidx],
                send_sem=send_sems.at[k - 1], recv_sem=recv_sems.at[k - 1],
                device_id=peer, device_id_type=MESH_ID).wait_recv()
        for cp in sends:
            cp.wait_send()
        local.wait()

    return pl.pallas_call(
        body, name=name,
        out_shape=jax.ShapeDtypeStruct(out_shape, x.dtype),
        in_specs=[pl.BlockSpec(memory_space=pl.ANY)],
        out_specs=pl.BlockSpec(memory_space=pl.ANY),
        scratch_shapes=[pltpu.SemaphoreType.DMA((N_DEV - 1,)), pltpu.SemaphoreType.DMA((N_DEV - 1,)),
                        pltpu.SemaphoreType.DMA],
    )(x)


def _matmul(a, b, mode, out_dtype, name, bias=None, add=None, b_col_off=0, n_out=None):
    if mode == "tn":
        K, M = a.shape
    else:
        M, K = a.shape
    N = n_out if n_out is not None else (b.shape[0] if mode == "nt" else b.shape[1])
    tm = _tile(M, 512)
    tn = _tile(N, 1024)
    tk = _tile(K, 2048 if mode != "tn" else 1024)
    nk = K // tk
    assert b_col_off % tn == 0
    joff = b_col_off // tn
    dims = {"nn": (((1,), (0,)), ((), ())), "nt": (((1,), (1,)), ((), ())), "tn": (((0,), (0,)), ((), ()))}[mode]

    def body(*refs):
        a_ref, b_ref = refs[0], refs[1]
        pos = 2
        bias_ref = add_ref = None
        if bias is not None:
            bias_ref = refs[pos]
            pos += 1
        if add is not None:
            add_ref = refs[pos]
            pos += 1
        o_ref = refs[pos]
        acc_ref = refs[pos + 1] if nk > 1 else None

        part = lax.dot_general(a_ref[...].astype(BF16), b_ref[...].astype(BF16), dims,
                               preferred_element_type=F32)

        def finish(r):
            if bias_ref is not None:
                r = r + bias_ref[...]
            if add_ref is not None:
                r = r + add_ref[...]
            o_ref[...] = r.astype(o_ref.dtype)

        if nk == 1:
            finish(part)
        else:
            kk = pl.program_id(2)

            @pl.when(kk == 0)
            def _():
                acc_ref[...] = part

            @pl.when(kk > 0)
            def _():
                acc_ref[...] += part

            @pl.when(kk == nk - 1)
            def _():
                finish(acc_ref[...])

    if mode == "tn":
        a_spec = pl.BlockSpec((tk, tm), lambda i, j, k: (k, i))
    else:
        a_spec = pl.BlockSpec((tm, tk), lambda i, j, k: (i, k))
    if mode == "nt":
        b_spec = pl.BlockSpec((tn, tk), lambda i, j, k: (j, k))
    else:
        b_spec = pl.BlockSpec((tk, tn), lambda i, j, k: (k, j + joff))
    in_specs = [a_spec, b_spec]
    args = [a, b]
    if bias is not None:
        in_specs.append(pl.BlockSpec((1, tn), lambda i, j, k: (0, j)))
        args.append(bias)
    if add is not None:
        in_specs.append(pl.BlockSpec((tm, tn), lambda i, j, k: (i, j)))
        args.append(add)
    return pl.pallas_call(
        body, name=name,
        grid=(M // tm, N // tn, nk),
        in_specs=in_specs,
        out_specs=pl.BlockSpec((tm, tn), lambda i, j, k: (i, j)),
        out_shape=jax.ShapeDtypeStruct((M, N), out_dtype),
        scratch_shapes=[pltpu.VMEM((tm, tn), F32)] if nk > 1 else [],
        compiler_params=_params("parallel", "parallel", "arbitrary"),
    )(*args)


def _rms_fwd(h, g, name):
    S, D = h.shape
    ts = _tile(S, 512)

    def body(h_ref, g_ref, o_ref):
        x = h_ref[...]
        r = lax.rsqrt(jnp.mean(x * x, axis=-1, keepdims=True) + RMS_EPS)
        o_ref[...] = (x * r * g_ref[...]).astype(BF16)

    return pl.pallas_call(
        body, name=name, grid=(S // ts,),
        in_specs=[pl.BlockSpec((ts, D), lambda i: (i, 0)), pl.BlockSpec((1, D), lambda i: (0, 0))],
        out_specs=pl.BlockSpec((ts, D), lambda i: (i, 0)),
        out_shape=jax.ShapeDtypeStruct((S, D), BF16),
        compiler_params=_params("parallel"),
    )(h, g)


def _rms_bwd_block(x, g, dy):
    r = lax.rsqrt(jnp.mean(x * x, axis=-1, keepdims=True) + RMS_EPS)
    xr = x * r
    t = dy * g
    dx = r * (t - xr * jnp.mean(t * xr, axis=-1, keepdims=True))
    return dx, dy * xr


def _rms_bwd(h, g, dhn, dh, name):
    S, D = h.shape
    ts = _tile(S, ROW_TILE)

    def body(h_ref, g_ref, dhn_ref, dh_ref, o_ref, dg_ref):
        dx, dgt = _rms_bwd_block(h_ref[...], g_ref[...], dhn_ref[...])
        o_ref[...] = dh_ref[...] + dx
        part = jnp.sum(dgt.reshape(ts // SUBLANES, SUBLANES, D), axis=0)

        @pl.when(pl.program_id(0) == 0)
        def _():
            dg_ref[...] = part

        @pl.when(pl.program_id(0) > 0)
        def _():
            dg_ref[...] += part

    row = pl.BlockSpec((ts, D), lambda i: (i, 0))
    return pl.pallas_call(
        body, name=name, grid=(S // ts,),
        in_specs=[row, pl.BlockSpec((1, D), lambda i: (0, 0)), row, row],
        out_specs=[row, pl.BlockSpec((SUBLANES, D), lambda i: (0, 0))],
        out_shape=[jax.ShapeDtypeStruct((S, D), F32), jax.ShapeDtypeStruct((SUBLANES, D), F32)],
        compiler_params=_params("arbitrary"),
    )(h, g, dhn, dh)


def _loss_head(h, g, target, name):
    S, D = h.shape
    ts = _tile(S, ROW_TILE)

    def body(h_ref, g_ref, t_ref, o_ref, loss_ref, dg_ref):
        x = h_ref[...]
        gg = g_ref[...]
        r = lax.rsqrt(jnp.mean(x * x, axis=-1, keepdims=True) + RMS_EPS)
        err = x * r * gg - t_ref[...]
        row_loss = 0.5 * jnp.mean(err * err, axis=-1, keepdims=True)
        dx, dgt = _rms_bwd_block(x, gg, err * (1.0 / D))
        o_ref[...] = dx
        part = jnp.sum(dgt.reshape(ts // SUBLANES, SUBLANES, D), axis=0)
        lpart = jnp.sum(jnp.broadcast_to(row_loss, (ts, LANES)).reshape(ts // SUBLANES, SUBLANES, LANES), axis=0)

        @pl.when(pl.program_id(0) == 0)
        def _():
            dg_ref[...] = part
            loss_ref[...] = lpart

        @pl.when(pl.program_id(0) > 0)
        def _():
            dg_ref[...] += part
            loss_ref[...] += lpart

    row = pl.BlockSpec((ts, D), lambda i: (i, 0))
    return pl.pallas_call(
        body, name=name, grid=(S // ts,),
        in_specs=[row, pl.BlockSpec((1, D), lambda i: (0, 0)), row],
        out_specs=[row, pl.BlockSpec((SUBLANES, LANES), lambda i: (0, 0)),
                   pl.BlockSpec((SUBLANES, D), lambda i: (0, 0))],
        out_shape=[jax.ShapeDtypeStruct((S, D), F32), jax.ShapeDtypeStruct((SUBLANES, LANES), F32),
                   jax.ShapeDtypeStruct((SUBLANES, D), F32)],
        compiler_params=_params("arbitrary"),
    )(h, g, target)


def _split3(x):
    hi = x.astype(BF16)
    r1 = x - hi.astype(F32)
    mid = r1.astype(BF16)
    lo = (r1 - mid.astype(F32)).astype(BF16)
    return hi, mid, lo


def _tri_sum(tri, x):
    hi, mid, lo = _split3(x)
    dot = functools.partial(jnp.dot, preferred_element_type=F32)
    return dot(tri, hi) + dot(tri, mid) + dot(tri, lo)


def _fgate_fwd(f, b_f, name):
    S = f.shape[0]
    tb = _tile(S, FGATE_ROWS)

    def body(f_ref, b_ref, c_ref, carry_ref):
        @pl.when(pl.program_id(0) == 0)
        def _():
            carry_ref[...] = jnp.zeros_like(carry_ref)

        x = f_ref[...] + b_ref[...]
        lf = jnp.minimum(x, 0.0) - jnp.log1p(jnp.exp(-jnp.abs(x)))
        r = lax.broadcasted_iota(jnp.int32, (tb, tb), 0)
        c = lax.broadcasted_iota(jnp.int32, (tb, tb), 1)
        tri = (c <= r).astype(BF16)
        c_ref[...] = _tri_sum(tri, lf) + carry_ref[0:1, :]
        carry_ref[...] += _tri_sum(jnp.ones((SUBLANES, tb), BF16), lf)

    return pl.pallas_call(
        body, name=name, grid=(S // tb,),
        in_specs=[pl.BlockSpec((tb, LANES), lambda i: (i, 0)), pl.BlockSpec((1, LANES), lambda i: (0, 0))],
        out_specs=pl.BlockSpec((tb, LANES), lambda i: (i, 0)),
        out_shape=jax.ShapeDtypeStruct((S, LANES), F32),
        scratch_shapes=[pltpu.VMEM((SUBLANES, LANES), F32)],
        compiler_params=_params("arbitrary"),
    )(f, b_f)


def _fgate_bwd(dc, f, b_f, name):
    S = f.shape[0]
    tb = _tile(S, FGATE_ROWS)
    nb = S // tb

    def body(dc_ref, f_ref, b_ref, df_ref, db_ref, carry_ref):
        @pl.when(pl.program_id(0) == 0)
        def _():
            carry_ref[...] = jnp.zeros_like(carry_ref)
            db_ref[...] = jnp.zeros_like(db_ref)

        d = dc_ref[...]
        r = lax.broadcasted_iota(jnp.int32, (tb, tb), 0)
        c = lax.broadcasted_iota(jnp.int32, (tb, tb), 1)
        tri = (c >= r).astype(BF16)
        dlf = _tri_sum(tri, d) + carry_ref[0:1, :]
        carry_ref[...] += _tri_sum(jnp.ones((SUBLANES, tb), BF16), d)
        df = dlf * _sigmoid(-(f_ref[...] + b_ref[...]))
        df_ref[...] = df
        db_ref[...] += jnp.sum(df.reshape(tb // SUBLANES, SUBLANES, LANES), axis=0)

    rev = pl.BlockSpec((tb, LANES), lambda i: (nb - 1 - i, 0))
    return pl.pallas_call(
        body, name=name, grid=(nb,),
        in_specs=[rev, rev, pl.BlockSpec((1, LANES), lambda i: (0, 0))],
        out_specs=[rev, pl.BlockSpec((SUBLANES, LANES), lambda i: (0, 0))],
        out_shape=[jax.ShapeDtypeStruct((S, LANES), F32), jax.ShapeDtypeStruct((SUBLANES, LANES), F32)],
        scratch_shapes=[pltpu.VMEM((SUBLANES, LANES), F32)],
        compiler_params=_params("arbitrary"),
    )(dc, f, b_f)


def _tri_tables(n, key_major):
    if key_major:
        pairs = [(qi, ki) for ki in range(n) for qi in range(ki, n)]
    else:
        pairs = [(qi, ki) for qi in range(n) for ki in range(qi + 1)]
    qt = np.array([p[0] for p in pairs], np.int32)
    kt = np.array([p[1] for p in pairs], np.int32)
    return jnp.asarray(qt), jnp.asarray(kt)


def _causal_mask(t):
    r = lax.broadcasted_iota(jnp.int32, (t, t), 0)
    c = lax.broadcasted_iota(jnp.int32, (t, t), 1)
    return r >= c


def _flash_fwd(qkv, gate, cneg, heads, name):
    S, W3 = qkv.shape
    W = W3 // 3
    dh = W // heads
    assert dh == LANES
    tq = _tile(S, FLASH_TILE)
    nq = S // tq
    qt, kt = _tri_tables(nq, key_major=False)
    scale = dh ** -0.5
    nt_dims = (((1,), (1,)), ((), ()))

    def body(qt_ref, kt_ref, q_ref, k_ref, v_ref, b_ref, g_ref, y_ref, o_ref, lse_ref, m_scr, l_scr, acc_scr):
        t = pl.program_id(1)
        qi, ki = qt_ref[t], kt_ref[t]

        @pl.when(ki == 0)
        def _():
            m_scr[...] = jnp.full_like(m_scr, NEG_BIG)
            l_scr[...] = jnp.zeros_like(l_scr)
            acc_scr[...] = jnp.zeros_like(acc_scr)

        def update(diagonal):
            s = lax.dot_general(q_ref[...], k_ref[...], nt_dims, preferred_element_type=F32) * scale + b_ref[0]
            if diagonal:
                s = jnp.where(_causal_mask(tq), s, NEG_BIG)
            m_prev = m_scr[...]
            m_next = jnp.maximum(m_prev, jnp.max(s, axis=1, keepdims=True))
            alpha = jnp.exp(m_prev - m_next)
            p = jnp.exp(s - m_next[:, :1])
            l_scr[...] = alpha * l_scr[...] + jnp.sum(p, axis=1, keepdims=True)
            acc_scr[...] = alpha * acc_scr[...] + jnp.dot(p.astype(BF16), v_ref[...], preferred_element_type=F32)
            m_scr[...] = m_next

        @pl.when(ki < qi)
        def _():
            update(False)

        @pl.when(ki == qi)
        def _():
            update(True)
            l = l_scr[...]
            o = acc_scr[...] / l
            g = g_ref[...]
            o_ref[...] = o
            lse_ref[...] = m_scr[...] + jnp.log(l)
            y_ref[...] = (o * (g * _sigmoid(g))).astype(BF16)

    def qmap(col):
        return lambda h, t, qt_ref, kt_ref: (qt_ref[t], col + h)

    def kmap(col):
        return lambda h, t, qt_ref, kt_ref: (kt_ref[t], col + h)

    qblk = functools.partial(pl.BlockSpec, (tq, dh))
    grid_spec = pltpu.PrefetchScalarGridSpec(
        num_scalar_prefetch=2, grid=(heads, int(qt.shape[0])),
        in_specs=[qblk(qmap(0)), qblk(kmap(heads)), qblk(kmap(2 * heads)),
                  pl.BlockSpec((1, 1, tq), lambda h, t, qt_ref, kt_ref: (h, 0, kt_ref[t])),
                  qblk(qmap(0))],
        out_specs=[qblk(qmap(0)), qblk(qmap(0)), qblk(qmap(0))],
        scratch_shapes=[pltpu.VMEM((tq, LANES), F32), pltpu.VMEM((tq, LANES), F32), pltpu.VMEM((tq, dh), F32)],
    )
    return pl.pallas_call(
        body, name=name, grid_spec=grid_spec,
        out_shape=[jax.ShapeDtypeStruct((S, W), BF16), jax.ShapeDtypeStruct((S, W), F32),
                   jax.ShapeDtypeStruct((S, W), F32)],
        compiler_params=_params("parallel", "arbitrary"),
    )(qt, kt, qkv, qkv, qkv, cneg, gate)


def _flash_bwd(qkv, do, o, lse, cneg, heads, name):
    S, W3 = qkv.shape
    W = W3 // 3
    dh = W // heads
    tq = _tile(S, FLASH_TILE)
    nq = S // tq
    qt, kt = _tri_tables(nq, key_major=True)
    n_pairs = int(qt.shape[0])
    scale = dh ** -0.5
    nt_dims = (((1,), (1,)), ((), ()))
    tn_dims = (((0,), (0,)), ((), ()))

    def body(qt_ref, kt_ref, q_ref, k_ref, v_ref, b_ref, do_ref, o_ref, lse_ref,
             dq_ref, dk_ref, dv_ref, db_ref, rs_ref, dq_scr, dk_scr, dv_scr, db_scr):
        t = pl.program_id(1)
        qi, ki = qt_ref[t], kt_ref[t]

        @pl.when(t == 0)
        def _():
            dq_scr[...] = jnp.zeros_like(dq_scr)
            rs_ref[...] = jnp.zeros_like(rs_ref)

        @pl.when(qi == ki)
        def _():
            dk_scr[...] = jnp.zeros_like(dk_scr)
            dv_scr[...] = jnp.zeros_like(dv_scr)
            db_scr[...] = jnp.zeros_like(db_scr)

        def update(diagonal):
            q, k, v, d_o = q_ref[...], k_ref[...], v_ref[...], do_ref[...]
            s = lax.dot_general(q, k, nt_dims, preferred_element_type=F32) * scale + b_ref[0]
            p = jnp.exp(s - lse_ref[...][:, :1])
            if diagonal:
                p = jnp.where(_causal_mask(tq), p, 0.0)
            delta = jnp.sum(d_o.astype(F32) * o_ref[...], axis=1, keepdims=True)
            dp = lax.dot_general(d_o, v, nt_dims, preferred_element_type=F32)
            ds = p * (dp - delta)
            dv_scr[...] += lax.dot_general(p.astype(BF16), d_o, tn_dims, preferred_element_type=F32)
            db_scr[...] += jnp.sum(ds.reshape(tq // SUBLANES, SUBLANES, tq), axis=0)
            dsb = (ds * scale).astype(BF16)
            dk_scr[...] += lax.dot_general(dsb, q, tn_dims, preferred_element_type=F32)
            rows = pl.ds(pl.multiple_of(qi * tq, tq), tq)
            dq_scr[rows, :] += jnp.dot(dsb, k, preferred_element_type=F32)
            rs_ref[rows, :] += jnp.sum(ds, axis=1, keepdims=True)

        @pl.when(qi > ki)
        def _():
            update(False)

        @pl.when(qi == ki)
        def _():
            update(True)

        @pl.when(qi == nq - 1)
        def _():
            dk_ref[...] = dk_scr[...].astype(BF16)
            dv_ref[...] = dv_scr[...].astype(BF16)
            db_ref[0] = jnp.sum(db_scr[...], axis=0, keepdims=True)

        @pl.when(t == n_pairs - 1)
        def _():
            dq_ref[...] = dq_scr[...].astype(BF16)

    def qmap(col):
        return lambda h, t, qt_ref, kt_ref: (qt_ref[t], col + h)

    def kmap(col):
        return lambda h, t, qt_ref, kt_ref: (kt_ref[t], col + h)

    blk = functools.partial(pl.BlockSpec, (tq, dh))
    bias_spec = pl.BlockSpec((1, 1, tq), lambda h, t, qt_ref, kt_ref: (h, 0, kt_ref[t]))
    grid_spec = pltpu.PrefetchScalarGridSpec(
        num_scalar_prefetch=2, grid=(heads, n_pairs),
        in_specs=[blk(qmap(0)), blk(kmap(heads)), blk(kmap(2 * heads)), bias_spec,
                  blk(qmap(0)), blk(qmap(0)), blk(qmap(0))],
        out_specs=[pl.BlockSpec((S, dh), lambda h, t, qt_ref, kt_ref: (0, h)),
                   blk(kmap(0)), blk(kmap(0)), bias_spec,
                   pl.BlockSpec((S, dh), lambda h, t, qt_ref, kt_ref: (0, h))],
        scratch_shapes=[pltpu.VMEM((S, dh), F32), pltpu.VMEM((tq, dh), F32), pltpu.VMEM((tq, dh), F32),
                        pltpu.VMEM((SUBLANES, tq), F32)],
    )
    return pl.pallas_call(
        body, name=name, grid_spec=grid_spec,
        out_shape=[jax.ShapeDtypeStruct((S, W), BF16), jax.ShapeDtypeStruct((S, W), BF16),
                   jax.ShapeDtypeStruct((S, W), BF16), jax.ShapeDtypeStruct((heads, 1, S), F32),
                   jax.ShapeDtypeStruct((S, W), F32)],
        compiler_params=_params("parallel", "arbitrary"),
    )(qt, kt, qkv, qkv, qkv, cneg, do, o, lse)


def _fox_gate_bwd(dy, o, gate, name):
    S, W = dy.shape
    ts = _tile(S, ROW_TILE)

    def body(dy_ref, o_ref, g_ref, do_ref, dg_ref):
        d, g = dy_ref[...], g_ref[...]
        sg = _sigmoid(g)
        do_ref[...] = (d * (g * sg)).astype(BF16)
        dg_ref[...] = (d * o_ref[...] * _dsilu(g, sg)).astype(BF16)

    row = pl.BlockSpec((ts, W), lambda i: (i, 0))
    return pl.pallas_call(
        body, name=name, grid=(S // ts,),
        in_specs=[row, row, row], out_specs=[row, row],
        out_shape=[jax.ShapeDtypeStruct((S, W), BF16), jax.ShapeDtypeStruct((S, W), BF16)],
        compiler_params=_params("parallel"),
    )(dy, o, gate)


def _shifted_copies(buf_ref, sh_ref, rows):
    for j in range(1, SUBLANES):
        sh_ref[j, 0:rows, :] = buf_ref[j:j + rows, :]


def _tap(buf_ref, sh_ref, r0, off, chunk):
    j, base = off % SUBLANES, off - off % SUBLANES
    if j == 0:
        return buf_ref[pl.ds(r0 + base, chunk), :]
    return sh_ref[j, pl.ds(r0 + base, chunk), :]


def _conv_fwd(proj, dw, dw_b, ln_g, ln_b, name):
    S, C3 = proj.shape
    C = C3 // 3
    K = dw.shape[0]
    assert K - 1 <= CONV_HALO - 2
    ts = _tile(S, CONV_ROWS)
    hb = ts // CONV_HALO
    nrows = ts + CONV_HALO
    lead = CONV_HALO - (K - 1)

    def body(a_ref, b_ref, ah_ref, bh_ref, g_ref, dw_ref, dwb_ref, lg_ref, lb_ref, y_ref, u1_ref, buf, sh):
        first = pl.program_id(0) == 0
        halo = ah_ref[...] * _sigmoid(bh_ref[...])
        buf[0:CONV_HALO, :] = jnp.where(first, 0.0, halo)

        def glu(r0):
            rows = pl.ds(r0, SUBLANES)
            buf[pl.ds(r0 + CONV_HALO, SUBLANES), :] = a_ref[rows, :] * _sigmoid(b_ref[rows, :])
        _rows(ts, SUBLANES, glu)
        _shifted_copies(buf, sh, nrows - SUBLANES)

        def chunk(r0):
            rows = pl.ds(r0, SUBLANES)
            acc = jnp.broadcast_to(dwb_ref[...], (SUBLANES, C))
            for k in range(K):
                acc = acc + dw_ref[k:k + 1, :] * _tap(buf, sh, r0, lead + k, SUBLANES)
            u1_ref[rows, :] = acc
            mu = jnp.mean(acc, axis=-1, keepdims=True)
            xc = acc - mu
            rstd = lax.rsqrt(jnp.mean(xc * xc, axis=-1, keepdims=True) + LN_EPS)
            z = xc * rstd * lg_ref[...] + lb_ref[...]
            g = g_ref[rows, :]
            y_ref[rows, :] = ((z * _sigmoid(z)) * (g * _sigmoid(g))).astype(BF16)
        _rows(ts, SUBLANES, chunk)

    row = lambda col: pl.BlockSpec((ts, C), lambda i: (i, col))
    halo = lambda col: pl.BlockSpec((CONV_HALO, C), lambda i: (jnp.maximum(i * hb - 1, 0), col))
    vec = pl.BlockSpec((1, C), lambda i: (0, 0))
    return pl.pallas_call(
        body, name=name, grid=(S // ts,),
        in_specs=[row(0), row(1), halo(0), halo(1), row(2), pl.BlockSpec((K, C), lambda i: (0, 0)), vec, vec, vec],
        out_specs=[row(0), row(0)],
        out_shape=[jax.ShapeDtypeStruct((S, C), BF16), jax.ShapeDtypeStruct((S, C), F32)],
        scratch_shapes=[pltpu.VMEM((nrows, C), F32), pltpu.VMEM((SUBLANES, nrows, C), F32)],
        compiler_params=_params("parallel"),
    )(proj, proj, proj, proj, proj, dw, dw_b, ln_g, ln_b)


def _conv_bwd_norm(dy, proj, u1, ln_g, ln_b, name):
    S, C = dy.shape
    ts = _tile(S, ROW_TILE)

    def body(dy_ref, g_ref, u1_ref, lg_ref, lb_ref, du1_ref, dg_ref, sums_ref):
        @pl.when(pl.program_id(0) == 0)
        def _():
            sums_ref[...] = jnp.zeros_like(sums_ref)

        def chunk(r0):
            rows = pl.ds(r0, SUBLANES)
            d, g, u1 = dy_ref[rows, :], g_ref[rows, :], u1_ref[rows, :]
            mu = jnp.mean(u1, axis=-1, keepdims=True)
            xc = u1 - mu
            rstd = lax.rsqrt(jnp.mean(xc * xc, axis=-1, keepdims=True) + LN_EPS)
            xh = xc * rstd
            z = xh * lg_ref[...] + lb_ref[...]
            sz, sg = _sigmoid(z), _sigmoid(g)
            dgate = d * (z * sz) * _dsilu(g, sg)
            dz = d * (g * sg) * _dsilu(z, sz)
            dxh = dz * lg_ref[...]
            du1 = rstd * (dxh - jnp.mean(dxh, axis=-1, keepdims=True)
                          - xh * jnp.mean(dxh * xh, axis=-1, keepdims=True))
            du1_ref[rows, :] = du1
            dg_ref[rows, :] = dgate.astype(BF16)
            sums_ref[0] += dz * xh
            sums_ref[1] += dz
            sums_ref[2] += du1
            sums_ref[3] += dgate
        _rows(ts, SUBLANES, chunk)

    row = pl.BlockSpec((ts, C), lambda i: (i, 0))
    vec = pl.BlockSpec((1, C), lambda i: (0, 0))
    return pl.pallas_call(
        body, name=name, grid=(S // ts,),
        in_specs=[row, pl.BlockSpec((ts, C), lambda i: (i, 2)), row, vec, vec],
        out_specs=[row, row, pl.BlockSpec((4, SUBLANES, C), lambda i: (0, 0, 0))],
        out_shape=[jax.ShapeDtypeStruct((S, C), F32), jax.ShapeDtypeStruct((S, C), BF16),
                   jax.ShapeDtypeStruct((4, SUBLANES, C), F32)],
        compiler_params=_params("arbitrary"),
    )(dy, proj, u1, ln_g, ln_b)


def _conv_bwd_taps(du1, proj, dw, name):
    S, C = du1.shape
    K = dw.shape[0]
    ts = _tile(S, CONV_ROWS)
    hb = ts // CONV_HALO
    nblk = S // ts
    last_halo = S // CONV_HALO - 1
    nrows = ts + CONV_HALO
    lead = CONV_HALO - (K - 1)

    def body(d_ref, dn_ref, a_ref, b_ref, ah_ref, bh_ref, dw_ref, da_ref, db_ref, ddw_ref, sums_ref,
             ubuf, ush, dbuf, dsh):
        i = pl.program_id(0)

        @pl.when(i == 0)
        def _():
            ddw_ref[...] = jnp.zeros_like(ddw_ref)
            sums_ref[...] = jnp.zeros_like(sums_ref)

        ubuf[0:CONV_HALO, :] = jnp.where(i == 0, 0.0, ah_ref[...] * _sigmoid(bh_ref[...]))
        dbuf[ts:nrows, :] = jnp.where(i == nblk - 1, 0.0, dn_ref[...])

        def fill(r0):
            rows = pl.ds(r0, SUBLANES)
            ubuf[pl.ds(r0 + CONV_HALO, SUBLANES), :] = a_ref[rows, :] * _sigmoid(b_ref[rows, :])
            dbuf[rows, :] = d_ref[rows, :]
        _rows(ts, SUBLANES, fill)
        _shifted_copies(ubuf, ush, nrows - SUBLANES)
        _shifted_copies(dbuf, dsh, nrows - SUBLANES)

        def chunk(r0):
            rows = pl.ds(r0, SUBLANES)
            acc = jnp.zeros((SUBLANES, C), F32)
            for k in range(K):
                acc = acc + dw_ref[k:k + 1, :] * _tap(dbuf, dsh, r0, K - 1 - k, SUBLANES)
            a, b = a_ref[rows, :], b_ref[rows, :]
            sb = _sigmoid(b)
            da = acc * sb
            db = acc * a * sb * (1.0 - sb)
            da_ref[rows, :] = da.astype(BF16)
            db_ref[rows, :] = db.astype(BF16)
            sums_ref[0] += da
            sums_ref[1] += db
        _rows(ts, SUBLANES, chunk)

        for k in range(K):
            def tap_sum(j, acc, k=k):
                r0 = pl.multiple_of(j * SUBLANES, SUBLANES)
                return acc + d_ref[pl.ds(r0, SUBLANES), :] * _tap(ubuf, ush, r0, lead + k, SUBLANES)
            ddw_ref[k] += lax.fori_loop(0, ts // SUBLANES, tap_sum, jnp.zeros((SUBLANES, C), F32))

    row = lambda col: pl.BlockSpec((ts, C), lambda i: (i, col))
    prev = lambda col: pl.BlockSpec((CONV_HALO, C), lambda i: (jnp.maximum(i * hb - 1, 0), col))
    nxt = pl.BlockSpec((CONV_HALO, C), lambda i: (jnp.minimum((i + 1) * hb, last_halo), 0))
    return pl.pallas_call(
        body, name=name, grid=(nblk,),
        in_specs=[row(0), nxt, row(0), row(1), prev(0), prev(1), pl.BlockSpec((K, C), lambda i: (0, 0))],
        out_specs=[row(0), row(0), pl.BlockSpec((K, SUBLANES, C), lambda i: (0, 0, 0)),
                   pl.BlockSpec((2, SUBLANES, C), lambda i: (0, 0, 0))],
        out_shape=[jax.ShapeDtypeStruct((S, C), BF16), jax.ShapeDtypeStruct((S, C), BF16),
                   jax.ShapeDtypeStruct((K, SUBLANES, C), F32), jax.ShapeDtypeStruct((2, SUBLANES, C), F32)],
        scratch_shapes=[pltpu.VMEM((nrows, C), F32), pltpu.VMEM((SUBLANES, nrows, C), F32),
                        pltpu.VMEM((nrows, C), F32), pltpu.VMEM((SUBLANES, nrows, C), F32)],
        compiler_params=_params("arbitrary"),
    )(du1, du1, proj, proj, proj, proj, dw)


def _adamw(parts, w, m, v, name):
    R, C = w.shape
    tr = _tile(R, 256)
    c1 = 1.0 - ADAM_B1 ** ADAM_STEP
    c2 = 1.0 - ADAM_B2 ** ADAM_STEP

    def body(p_ref, w_ref, m_ref, v_ref, g_ref, d_ref, nm_ref, nv_ref):
        g = p_ref[0].astype(F32)
        for d in range(1, N_DEV):
            g = g + p_ref[d].astype(F32)
        nm = ADAM_B1 * m_ref[...] + (1.0 - ADAM_B1) * g
        nv = ADAM_B2 * v_ref[...] + (1.0 - ADAM_B2) * (g * g)
        g_ref[...] = g
        nm_ref[...] = nm
        nv_ref[...] = nv
        d_ref[...] = -ADAM_LR * ((nm / c1) / (jnp.sqrt(nv / c2) + ADAM_EPS) + ADAM_WD * w_ref[...])

    row = pl.BlockSpec((tr, C), lambda i: (i, 0))
    out = jax.ShapeDtypeStruct((R, C), F32)
    return pl.pallas_call(
        body, name=name, grid=(R // tr,),
        in_specs=[pl.BlockSpec((N_DEV, tr, C), lambda i: (0, i, 0)), row, row, row],
        out_specs=[row, row, row, row], out_shape=[out, out, out, out],
        compiler_params=_params("parallel"),
    )(parts, w, m, v)


def _pad_lanes(a, width=LANES):
    return jnp.pad(a, ((0, 0), (0, width - a.shape[1])))


def _flat_rows(parts, width=LANES):
    flat = jnp.concatenate([p.reshape(-1) for p in parts])
    rows = -(-flat.shape[0] // width)
    rows = -(-rows // SUBLANES) * SUBLANES
    return jnp.pad(flat, (0, rows * width - flat.shape[0])).reshape(rows, width)


def _unflat(rows2d, shapes):
    flat = rows2d.reshape(-1)
    out, pos = [], 0
    for s in shapes:
        n = int(np.prod(s))
        out.append(flat[pos:pos + n].reshape(s))
        pos += n
    return out


def kernel(x, norm_g, fox_w_in, fox_b_f, fox_w_out, conv_w_in, conv_b_in, conv_dw, conv_dw_b, conv_ln_g, conv_ln_b, conv_w_out, final_norm_g, loss_target, m_norm_g, m_fox_w_in, m_fox_b_f, m_fox_w_out, m_conv_w_in, m_conv_b_in, m_conv_dw, m_conv_dw_b, m_conv_ln_g, m_conv_ln_b, m_conv_w_out, m_final_norm_g, v_norm_g, v_fox_w_in, v_fox_b_f, v_fox_w_out, v_conv_w_in, v_conv_b_in, v_conv_dw, v_conv_dw_b, v_conv_ln_g, v_conv_ln_b, v_conv_w_out, v_final_norm_g):
    h0 = x[0]
    target = loss_target[0]
    S, D = h0.shape
    depth = norm_g.shape[0]
    n_fox, _, fin_shard = fox_w_in.shape
    n_conv, _, cin_shard = conv_w_in.shape
    heads = fox_b_f.shape[1]
    W = fox_w_out.shape[1] * N_DEV
    C = conv_w_out.shape[1] * N_DEV
    K = conv_dw.shape[1]
    assert fin_shard * N_DEV == 4 * W + heads and cin_shard * N_DEV == 3 * C and heads <= LANES

    fin_all = _exchange(fox_w_in.astype(BF16), "ag_fox_w_in", True)
    fout_all = _exchange(fox_w_out.astype(BF16), "ag_fox_w_out", True)
    cin_all = _exchange(conv_w_in.astype(BF16), "ag_conv_w_in", True)
    cout_all = _exchange(conv_w_out.astype(BF16), "ag_conv_w_out", True)
    small_shapes = [conv_b_in.shape, conv_dw.shape, conv_dw_b.shape, conv_ln_g.shape, conv_ln_b.shape]
    small = _flat_rows([conv_b_in, conv_dw, conv_dw_b, conv_ln_g, conv_ln_b])
    small_all = _exchange(small, "ag_small", True)
    b_in_s, dw_s, dwb_s, lng_s, lnb_s = zip(*[_unflat(small_all[d], small_shapes) for d in range(N_DEV)])
    cat = lambda parts, axis: jnp.concatenate(parts, axis=axis)
    conv_b_in_f = cat(b_in_s, 1)
    conv_dw_f = cat(dw_s, 2)
    conv_dw_b_f = cat(dwb_s, 1)
    conv_ln_g_f = cat(lng_s, 1)
    conv_ln_b_f = cat(lnb_s, 1)

    def fox_weights(j):
        w = jnp.transpose(fin_all[:, j], (1, 0, 2)).reshape(D, N_DEV * fin_shard)
        return w[:, :4 * W], _pad_lanes(w[:, 4 * W:]), fout_all[:, j].reshape(W, D)

    def conv_weights(j):
        w = jnp.transpose(cin_all[:, j], (1, 0, 2)).reshape(D, 3 * C)
        return w, cout_all[:, j].reshape(C, D)

    h = h0
    saved = []
    for i in range(depth):
        j = i // 2
        g_i = norm_g[i:i + 1]
        hn = _rms_fwd(h, g_i, f"rms_fwd{i}")
        if i % 2 == 0:
            w_qkvg, w_f, w_out = fox_weights(j)
            qkv = _matmul(hn, w_qkvg, "nn", BF16, f"fox_qkv{i}", n_out=3 * W)
            gate = _matmul(hn, w_qkvg, "nn", F32, f"fox_gate{i}", b_col_off=3 * W, n_out=W)
            f = _matmul(hn, w_f, "nn", F32, f"fox_f{i}")
            b_f = _pad_lanes(fox_b_f[j:j + 1])
            c = _fgate_fwd(f, b_f, f"fgate_fwd{i}")
            cneg = (-c[:, :heads]).T.reshape(heads, 1, S)
            y, o, lse = _flash_fwd(qkv, gate, cneg, heads, f"flash_fwd{i}")
            saved.append(dict(h=h, hn=hn, qkv=qkv, gate=gate, f=f, b_f=b_f, cneg=cneg, y=y, o=o, lse=lse,
                              w_qkvg=w_qkvg, w_f=w_f, w_out=w_out))
        else:
            w_in, w_out = conv_weights(j)
            proj = _matmul(hn, w_in, "nn", F32, f"conv_in{i}", bias=conv_b_in_f[j:j + 1])
            y, u1 = _conv_fwd(proj, conv_dw_f[j], conv_dw_b_f[j:j + 1], conv_ln_g_f[j:j + 1],
                              conv_ln_b_f[j:j + 1], f"conv_fwd{i}")
            saved.append(dict(h=h, hn=hn, proj=proj, y=y, u1=u1, w_in=w_in, w_out=w_out))
        h = _matmul(y, w_out, "nn", F32, f"out_proj{i}", add=h)

    dh, loss_part, dg_final = _loss_head(h, final_norm_g[None, :], target, "loss_head")

    d_norm_g = [None] * depth
    d_fox_w_in = [None] * n_fox
    d_fox_w_out = [None] * n_fox
    d_fox_b_f = [None] * n_fox
    d_conv_w_in = [None] * n_conv
    d_conv_w_out = [None] * n_conv
    d_conv_small = [None] * n_conv
    for i in reversed(range(depth)):
        j = i // 2
        sv = saved[i]
        dy = _matmul(dh, sv["w_out"], "nt", F32, f"d_out_proj{i}")
        dw_out = _matmul(sv["y"], dh, "tn", F32, f"dw_out{i}")
        if i % 2 == 0:
            do, dgate = _fox_gate_bwd(dy, sv["o"], sv["gate"], f"fox_gate_bwd{i}")
            dq, dk, dv, dbias, rowsum = _flash_bwd(sv["qkv"], do, sv["o"], sv["lse"], sv["cneg"], heads,
                                                   f"flash_bwd{i}")
            dc = _pad_lanes(rowsum[:, ::LANES] - dbias.reshape(heads, S).T)
            df, dbf = _fgate_bwd(dc, sv["f"], sv["b_f"], f"fgate_bwd{i}")
            dproj = jnp.concatenate([dq, dk, dv, dgate], axis=1)
            dhn = _matmul(df, sv["w_f"], "nt", F32, f"d_fox_f{i}")
            dhn = _matmul(dproj, sv["w_qkvg"], "nt", F32, f"d_fox_in{i}", add=dhn)
            dw_qkvg = _matmul(sv["hn"], dproj, "tn", F32, f"dw_fox_in{i}")
            dw_f = _matmul(sv["hn"], df, "tn", F32, f"dw_fox_f{i}")
            d_fox_w_in[j] = jnp.concatenate([dw_qkvg, dw_f[:, :heads]], axis=1)
            d_fox_w_out[j] = dw_out
            d_fox_b_f[j] = jnp.sum(dbf, axis=0)[:heads]
        else:
            du1, dgate, nsums = _conv_bwd_norm(dy, sv["proj"], sv["u1"], conv_ln_g_f[j:j + 1],
                                               conv_ln_b_f[j:j + 1], f"conv_bwd_norm{i}")
            da, db, ddw, absums = _conv_bwd_taps(du1, sv["proj"], conv_dw_f[j], f"conv_bwd_taps{i}")
            dproj = jnp.concatenate([da, db, dgate], axis=1)
            dhn = _matmul(dproj, sv["w_in"], "nt", F32, f"d_conv_in{i}")
            d_conv_w_in[j] = _matmul(sv["hn"], dproj, "tn", F32, f"dw_conv_in{i}")
            d_conv_w_out[j] = dw_out
            nsum = jnp.sum(nsums, axis=1)
            absum = jnp.sum(absums, axis=1)
            d_conv_small[j] = dict(b_in=jnp.concatenate([absum[0], absum[1], nsum[3]]),
                                   dw=jnp.sum(ddw, axis=1), dw_b=nsum[2], ln_g=nsum[0], ln_b=nsum[1])
        dh, dg = _rms_bwd(sv["h"], norm_g[i:i + 1], dhn, dh, f"rms_bwd{i}")
        d_norm_g[i] = jnp.sum(dg, axis=0)
    grad_x = dh[None]

    def shard_cols(g, shard):
        return jnp.transpose(g.reshape(g.shape[0], N_DEV, shard), (1, 0, 2))

    p_fin = jnp.stack([shard_cols(g, fin_shard) for g in d_fox_w_in], axis=1).astype(BF16)
    p_fout = jnp.stack([g.reshape(N_DEV, W // N_DEV, D) for g in d_fox_w_out], axis=1).astype(BF16)
    p_cin = jnp.stack([shard_cols(g, cin_shard) for g in d_conv_w_in], axis=1).astype(BF16)
    p_cout = jnp.stack([g.reshape(N_DEV, C // N_DEV, D) for g in d_conv_w_out], axis=1).astype(BF16)

    def small_for(d):
        sl = lambda a, n: a[..., d * n:(d + 1) * n]
        return _flat_rows([
            jnp.stack([sl(s["b_in"], 3 * C // N_DEV) for s in d_conv_small]),
            jnp.stack([sl(s["dw"], C // N_DEV) for s in d_conv_small]),
            jnp.stack([sl(s["dw_b"], C // N_DEV) for s in d_conv_small]),
            jnp.stack([sl(s["ln_g"], C // N_DEV) for s in d_conv_small]),
            jnp.stack([sl(s["ln_b"], C // N_DEV) for s in d_conv_small])])
    p_small = jnp.stack([small_for(d) for d in range(N_DEV)])

    r_fin = _exchange(p_fin, "rs_fox_w_in", False)
    r_fout = _exchange(p_fout, "rs_fox_w_out", False)
    r_cin = _exchange(p_cin, "rs_conv_w_in", False)
    r_cout = _exchange(p_cout, "rs_conv_w_out", False)
    r_small = _exchange(p_small, "rs_small", False)

    rep_shapes = [norm_g.shape, fox_b_f.shape, final_norm_g.shape, (1,)]
    rep_part = _flat_rows([jnp.stack(d_norm_g), jnp.stack(d_fox_b_f), jnp.sum(dg_final, axis=0),
                           jnp.sum(loss_part[:, 0])[None]])
    r_rep = _exchange(rep_part, "ag_replicated", True)

    def update(parts, w, m, v, name):
        two_d = (-1, w.shape[-1])
        res = _adamw(parts.reshape((N_DEV,) + w.reshape(two_d).shape), w.reshape(two_d), m.reshape(two_d),
                     v.reshape(two_d), name)
        return [r.reshape(w.shape) for r in res]

    u_fin = update(r_fin, fox_w_in, m_fox_w_in, v_fox_w_in, "adamw_fox_w_in")
    u_fout = update(r_fout, fox_w_out, m_fox_w_out, v_fox_w_out, "adamw_fox_w_out")
    u_cin = update(r_cin, conv_w_in, m_conv_w_in, v_conv_w_in, "adamw_conv_w_in")
    u_cout = update(r_cout, conv_w_out, m_conv_w_out, v_conv_w_out, "adamw_conv_w_out")
    u_small = _adamw(r_small, small,
                     _flat_rows([m_conv_b_in, m_conv_dw, m_conv_dw_b, m_conv_ln_g, m_conv_ln_b]),
                     _flat_rows([v_conv_b_in, v_conv_dw, v_conv_dw_b, v_conv_ln_g, v_conv_ln_b]), "adamw_small")
    zero1 = jnp.zeros((1,), F32)
    u_rep = _adamw(r_rep, _flat_rows([norm_g, fox_b_f, final_norm_g, zero1]),
                   _flat_rows([m_norm_g, m_fox_b_f, m_final_norm_g, zero1]),
                   _flat_rows([v_norm_g, v_fox_b_f, v_final_norm_g, zero1]), "adamw_replicated")

    outs = []
    loss = None
    for kind in range(4):
        b_in_k, dw_k, dwb_k, lng_k, lnb_k = _unflat(u_small[kind], small_shapes)
        ng_k, bf_k, fg_k, loss_k = _unflat(u_rep[kind], rep_shapes)
        if kind == 0:
            loss = loss_k[0]
        outs += [ng_k, u_fin[kind], bf_k, u_fout[kind], u_cin[kind], b_in_k, dw_k, dwb_k, lng_k, lnb_k,
                 u_cout[kind], fg_k]
    return (loss, grad_x, *outs)
```

```python
import functools

import numpy as np
import jax
import jax.numpy as jnp
from jax import lax
from jax.experimental import pallas as pl
from jax.experimental.pallas import tpu as pltpu

F32 = jnp.float32
BF16 = jnp.bfloat16
MESH_ID = pl.DeviceIdType.MESH

N_DEV = 8
RMS_EPS = 1e-6
LN_EPS = 1e-5
ADAM_LR = 0.001
ADAM_B1 = 0.9
ADAM_B2 = 0.999
ADAM_EPS = 1e-08
ADAM_WD = 0.01
ADAM_STEP = 10

LANES = 128
SUBLANES = 8
VMEM_LIMIT = 56 * 1024 * 1024
NEG_BIG = -1e30
CONV_HALO = 32
FLASH_TILE = 512
CONV_ROWS = 128
CONV_CHUNK = 32
FGATE_ROWS = 256
ROW_TILE = 256


def _params(*sem):
    return pltpu.CompilerParams(dimension_semantics=sem if sem else None, vmem_limit_bytes=VMEM_LIMIT)


def _tile(n, pref):
    if n <= pref:
        return n
    t = pref
    while n % t:
        t //= 2
    return t


def _sigmoid(x):
    return 1.0 / (1.0 + jnp.exp(-x))


def _dsilu(x, s):
    return s * (1.0 + x * (1.0 - s))


def _rows(n, chunk, fn):
    def step(i, carry):
        fn(pl.multiple_of(i * chunk, chunk))
        return carry
    lax.fori_loop(0, n // chunk, step, 0)


def _peer(k):
    x, y, c = lax.axis_index("x"), lax.axis_index("y"), lax.axis_index("c")
    px = 1 - x if (k >> 2) & 1 else x
    py = 1 - y if (k >> 1) & 1 else y
    pc = 1 - c if k & 1 else c
    return (px, py, pc), 4 * px + 2 * py + pc


def _exchange_copies(x_ref, o_ref, send_sems, recv_sems, local_sem, gather):
    _, me = _peer(0)
    mine = x_ref if gather else x_ref.at[me]
    local = pltpu.make_async_copy(mine, o_ref.at[me], local_sem)
    sends, arrivals = [], []
    for k in range(1, N_DEV):
        peer, pidx = _peer(k)
        sems = dict(send_sem=send_sems.at[k - 1], recv_sem=recv_sems.at[k - 1], device_id=peer,
                    device_id_type=MESH_ID)
        sends.append(pltpu.make_async_remote_copy(src_ref=x_ref if gather else x_ref.at[pidx],
                                                  dst_ref=o_ref.at[me], **sems))
        arrivals.append(pltpu.make_async_remote_copy(src_ref=mine, dst_ref=o_ref.at[pidx], **sems))
    return local, sends, arrivals


def _exchange_start(*refs, gather):
    local, sends, _ = _exchange_copies(*refs, gather)
    local.start()
    for cp in sends:
        cp.start()


def _exchange_wait(*refs, gather):
    local, sends, arrivals = _exchange_copies(*refs, gather)
    for cp in arrivals:
        cp.wait_recv()
    for cp in sends:
        cp.wait_send()
    local.wait()


EXCHANGE_SCRATCH = [pltpu.SemaphoreType.DMA((N_DEV - 1,)), pltpu.SemaphoreType.DMA((N_DEV - 1,)),
                    pltpu.SemaphoreType.DMA]


def _exchange_shape(x, gather):
    return jax.ShapeDtypeStruct((N_DEV,) + x.shape if gather else x.shape, x.dtype)


def _exchange(x, name, gather):
    def body(*refs):
        _exchange_start(*refs, gather=gather)
        _exchange_wait(*refs, gather=gather)

    return pl.pallas_call(
        body, name=name,
        out_shape=_exchange_shape(x, gather),
        in_specs=[pl.BlockSpec(memory_space=pl.ANY)],
        out_specs=pl.BlockSpec(memory_space=pl.ANY),
        scratch_shapes=list(EXCHANGE_SCRATCH),
    )(x)


def _matmul(a, b, mode, out_dtype, name, bias=None, add=None, b_col_off=0, n_out=None, carry=None):
    if mode == "tn":
        K, M = a.shape
    else:
        M, K = a.shape
    N = n_out if n_out is not None else (b.shape[0] if mode == "nt" else b.shape[1])
    tm = _tile(M, 512)
    tn = _tile(N, 1024)
    tk = _tile(K, 2048 if mode != "tn" else 1024)
    nm, nn, nk = M // tm, N // tn, K // tk
    assert b_col_off % tn == 0
    joff = b_col_off // tn
    dims = {"nn": (((1,), (0,)), ((), ())), "nt": (((1,), (1,)), ((), ())), "tn": (((0,), (0,)), ((), ()))}[mode]
    n_in = 2 + (bias is not None) + (add is not None)

    def body(*refs):
        a_ref, b_ref = refs[0], refs[1]
        bias_ref = refs[2] if bias is not None else None
        add_ref = refs[n_in - 1] if add is not None else None
        pos = n_in
        x_ref = o_ref = x_out_ref = None
        if carry is not None:
            x_ref, o_ref, x_out_ref = refs[pos], refs[pos + 1], refs[pos + 2]
            pos += 3
        else:
            o_ref = refs[pos]
            pos += 1
        acc_ref = None
        if nk > 1:
            acc_ref = refs[pos]
            pos += 1
        exchange_refs = (x_ref, x_out_ref) + tuple(refs[pos:])
        i, j, kk = pl.program_id(0), pl.program_id(1), pl.program_id(2)

        if carry is not None:
            @pl.when((i == 0) & (j == 0) & (kk == 0))
            def _():
                _exchange_start(*exchange_refs, gather=carry[1])

        part = lax.dot_general(a_ref[...].astype(BF16), b_ref[...].astype(BF16), dims,
                               preferred_element_type=F32)

        def finish(r):
            if bias_ref is not None:
                r = r + bias_ref[...]
            if add_ref is not None:
                r = r + add_ref[...]
            o_ref[...] = r.astype(o_ref.dtype)

        if nk == 1:
            finish(part)
        else:
            @pl.when(kk == 0)
            def _():
                acc_ref[...] = part

            @pl.when(kk > 0)
            def _():
                acc_ref[...] += part

            @pl.when(kk == nk - 1)
            def _():
                finish(acc_ref[...])

        if carry is not None:
            @pl.when((i == nm - 1) & (j == nn - 1) & (kk == nk - 1))
            def _():
                _exchange_wait(*exchange_refs, gather=carry[1])

    if mode == "tn":
        a_spec = pl.BlockSpec((tk, tm), lambda i, j, k: (k, i))
    else:
        a_spec = pl.BlockSpec((tm, tk), lambda i, j, k: (i, k))
    if mode == "nt":
        b_spec = pl.BlockSpec((tn, tk), lambda i, j, k: (j, k))
    else:
        b_spec = pl.BlockSpec((tk, tn), lambda i, j, k: (k, j + joff))
    in_specs = [a_spec, b_spec]
    args = [a, b]
    if bias is not None:
        in_specs.append(pl.BlockSpec((1, tn), lambda i, j, k: (0, j)))
        args.append(bias)
    if add is not None:
        in_specs.append(pl.BlockSpec((tm, tn), lambda i, j, k: (i, j)))
        args.append(add)
    out_specs = [pl.BlockSpec((tm, tn), lambda i, j, k: (i, j))]
    out_shape = [jax.ShapeDtypeStruct((M, N), out_dtype)]
    scratch = [pltpu.VMEM((tm, tn), F32)] if nk > 1 else []
    if carry is not None:
        in_specs.append(pl.BlockSpec(memory_space=pl.ANY))
        args.append(carry[0])
        out_specs.append(pl.BlockSpec(memory_space=pl.ANY))
        out_shape.append(_exchange_shape(*carry))
        scratch += EXCHANGE_SCRATCH
    res = pl.pallas_call(
        body, name=name,
        grid=(nm, nn, nk),
        in_specs=in_specs, out_specs=out_specs, out_shape=out_shape, scratch_shapes=scratch,
        compiler_params=_params(*(("arbitrary",) * 3 if carry is not None else ("parallel", "parallel", "arbitrary"))),
    )(*args)
    return res if carry is not None else res[0]


def _rms_fwd(h, g, name):
    S, D = h.shape
    ts = _tile(S, 512)

    def body(h_ref, g_ref, o_ref):
        x = h_ref[...]
        r = lax.rsqrt(jnp.mean(x * x, axis=-1, keepdims=True) + RMS_EPS)
        o_ref[...] = (x * r * g_ref[...]).astype(BF16)

    return pl.pallas_call(
        body, name=name, grid=(S // ts,),
        in_specs=[pl.BlockSpec((ts, D), lambda i: (i, 0)), pl.BlockSpec((1, D), lambda i: (0, 0))],
        out_specs=pl.BlockSpec((ts, D), lambda i: (i, 0)),
        out_shape=jax.ShapeDtypeStruct((S, D), BF16),
        compiler_params=_params("parallel"),
    )(h, g)


def _rms_bwd_block(x, g, dy):
    r = lax.rsqrt(jnp.mean(x * x, axis=-1, keepdims=True) + RMS_EPS)
    xr = x * r
    t = dy * g
    dx = r * (t - xr * jnp.mean(t * xr, axis=-1, keepdims=True))
    return dx, dy * xr


def _rms_bwd(h, g, dhn, dh, name):
    S, D = h.shape
    ts = _tile(S, ROW_TILE)

    def body(h_ref, g_ref, dhn_ref, dh_ref, o_ref, dg_ref):
        dx, dgt = _rms_bwd_block(h_ref[...], g_ref[...], dhn_ref[...])
        o_ref[...] = dh_ref[...] + dx
        part = jnp.sum(dgt.reshape(ts // SUBLANES, SUBLANES, D), axis=0)

        @pl.when(pl.program_id(0) == 0)
        def _():
            dg_ref[...] = part

        @pl.when(pl.program_id(0) > 0)
        def _():
            dg_ref[...] += part

    row = pl.BlockSpec((ts, D), lambda i: (i, 0))
    return pl.pallas_call(
        body, name=name, grid=(S // ts,),
        in_specs=[row, pl.BlockSpec((1, D), lambda i: (0, 0)), row, row],
        out_specs=[row, pl.BlockSpec((SUBLANES, D), lambda i: (0, 0))],
        out_shape=[jax.ShapeDtypeStruct((S, D), F32), jax.ShapeDtypeStruct((SUBLANES, D), F32)],
        compiler_params=_params("arbitrary"),
    )(h, g, dhn, dh)


def _loss_head(h, g, target, name):
    S, D = h.shape
    ts = _tile(S, ROW_TILE)

    def body(h_ref, g_ref, t_ref, o_ref, loss_ref, dg_ref):
        x = h_ref[...]
        gg = g_ref[...]
        r = lax.rsqrt(jnp.mean(x * x, axis=-1, keepdims=True) + RMS_EPS)
        err = x * r * gg - t_ref[...]
        row_loss = 0.5 * jnp.mean(err * err, axis=-1, keepdims=True)
        dx, dgt = _rms_bwd_block(x, gg, err * (1.0 / D))
        o_ref[...] = dx
        part = jnp.sum(dgt.reshape(ts // SUBLANES, SUBLANES, D), axis=0)
        lpart = jnp.sum(jnp.broadcast_to(row_loss, (ts, LANES)).reshape(ts // SUBLANES, SUBLANES, LANES), axis=0)

        @pl.when(pl.program_id(0) == 0)
        def _():
            dg_ref[...] = part
            loss_ref[...] = lpart

        @pl.when(pl.program_id(0) > 0)
        def _():
            dg_ref[...] += part
            loss_ref[...] += lpart

    row = pl.BlockSpec((ts, D), lambda i: (i, 0))
    return pl.pallas_call(
        body, name=name, grid=(S // ts,),
        in_specs=[row, pl.BlockSpec((1, D), lambda i: (0, 0)), row],
        out_specs=[row, pl.BlockSpec((SUBLANES, LANES), lambda i: (0, 0)),
                   pl.BlockSpec((SUBLANES, D), lambda i: (0, 0))],
        out_shape=[jax.ShapeDtypeStruct((S, D), F32), jax.ShapeDtypeStruct((SUBLANES, LANES), F32),
                   jax.ShapeDtypeStruct((SUBLANES, D), F32)],
        compiler_params=_params("arbitrary"),
    )(h, g, target)


def _split3(x):
    hi = x.astype(BF16)
    r1 = x - hi.astype(F32)
    mid = r1.astype(BF16)
    lo = (r1 - mid.astype(F32)).astype(BF16)
    return hi, mid, lo


def _tri_sum(tri, x):
    hi, mid, lo = _split3(x)
    dot = functools.partial(jnp.dot, preferred_element_type=F32)
    return dot(tri, hi) + dot(tri, mid) + dot(tri, lo)


def _fgate_fwd(f, b_f, name):
    S = f.shape[0]
    tb = _tile(S, FGATE_ROWS)

    def body(f_ref, b_ref, c_ref, carry_ref):
        @pl.when(pl.program_id(0) == 0)
        def _():
            carry_ref[...] = jnp.zeros_like(carry_ref)

        x = f_ref[...] + b_ref[...]
        lf = jnp.minimum(x, 0.0) - jnp.log1p(jnp.exp(-jnp.abs(x)))
        r = lax.broadcasted_iota(jnp.int32, (tb, tb), 0)
        c = lax.broadcasted_iota(jnp.int32, (tb, tb), 1)
        tri = (c <= r).astype(BF16)
        c_ref[...] = _tri_sum(tri, lf) + carry_ref[0:1, :]
        carry_ref[...] += _tri_sum(jnp.ones((SUBLANES, tb), BF16), lf)

    return pl.pallas_call(
        body, name=name, grid=(S // tb,),
        in_specs=[pl.BlockSpec((tb, LANES), lambda i: (i, 0)), pl.BlockSpec((1, LANES), lambda i: (0, 0))],
        out_specs=pl.BlockSpec((tb, LANES), lambda i: (i, 0)),
        out_shape=jax.ShapeDtypeStruct((S, LANES), F32),
        scratch_shapes=[pltpu.VMEM((SUBLANES, LANES), F32)],
        compiler_params=_params("arbitrary"),
    )(f, b_f)


def _fgate_bwd(dc, f, b_f, name):
    S = f.shape[0]
    tb = _tile(S, FGATE_ROWS)
    nb = S // tb

    def body(dc_ref, f_ref, b_ref, df_ref, db_ref, carry_ref):
        @pl.when(pl.program_id(0) == 0)
        def _():
            carry_ref[...] = jnp.zeros_like(carry_ref)
            db_ref[...] = jnp.zeros_like(db_ref)

        d = dc_ref[...]
        r = lax.broadcasted_iota(jnp.int32, (tb, tb), 0)
        c = lax.broadcasted_iota(jnp.int32, (tb, tb), 1)
        tri = (c >= r).astype(BF16)
        dlf = _tri_sum(tri, d) + carry_ref[0:1, :]
        carry_ref[...] += _tri_sum(jnp.ones((SUBLANES, tb), BF16), d)
        df = dlf * _sigmoid(-(f_ref[...] + b_ref[...]))
        df_ref[...] = df
        db_ref[...] += jnp.sum(df.reshape(tb // SUBLANES, SUBLANES, LANES), axis=0)

    rev = pl.BlockSpec((tb, LANES), lambda i: (nb - 1 - i, 0))
    return pl.pallas_call(
        body, name=name, grid=(nb,),
        in_specs=[rev, rev, pl.BlockSpec((1, LANES), lambda i: (0, 0))],
        out_specs=[rev, pl.BlockSpec((SUBLANES, LANES), lambda i: (0, 0))],
        out_shape=[jax.ShapeDtypeStruct((S, LANES), F32), jax.ShapeDtypeStruct((SUBLANES, LANES), F32)],
        scratch_shapes=[pltpu.VMEM((SUBLANES, LANES), F32)],
        compiler_params=_params("arbitrary"),
    )(dc, f, b_f)


LOG2E = 1.4426950408889634
NT_DIMS = (((1,), (1,)), ((), ()))
TN_DIMS = (((0,), (0,)), ((), ()))


def _causal_mask(t):
    r = lax.broadcasted_iota(jnp.int32, (t, t), 0)
    c = lax.broadcasted_iota(jnp.int32, (t, t), 1)
    return r >= c


def _key_tiles(qi, tile):
    def pair(i, carry):
        tile(2 * i, False)
        tile(2 * i + 1, False)
        return carry
    lax.fori_loop(0, qi // 2, pair, 0)

    @pl.when(qi % 2 == 1)
    def _():
        tile(qi - 1, False)

    tile(qi, True)


def _flash_fwd(qkv, gate, cneg, heads, name):
    S, W3 = qkv.shape
    W = W3 // 3
    dh = W // heads
    assert dh == LANES
    tq = _tile(S, FLASH_TILE)
    nq = S // tq
    c1 = dh ** -0.5 * LOG2E

    def body(q_ref, k_ref, v_ref, b_ref, g_ref, y_ref, o_ref, lse_ref, m_scr, l_scr, acc_scr):
        qi = pl.program_id(1)
        m_scr[...] = jnp.full_like(m_scr, NEG_BIG)
        l_scr[...] = jnp.zeros_like(l_scr)
        acc_scr[...] = jnp.zeros_like(acc_scr)
        q = q_ref[...]

        def tile(j, diagonal):
            rows = pl.ds(pl.multiple_of(j * tq, tq), tq)
            t = lax.dot_general(q, k_ref[rows, :], NT_DIMS, preferred_element_type=F32) * c1 + b_ref[0, j] * LOG2E
            if diagonal:
                t = jnp.where(_causal_mask(tq), t, NEG_BIG)
            m_prev = m_scr[...]
            m_next = jnp.maximum(m_prev, jnp.max(t, axis=1, keepdims=True))
            alpha = jnp.exp2(m_prev - m_next)
            p = jnp.exp2(t - m_next[:, :1])
            l_scr[...] = alpha * l_scr[...] + jnp.sum(p, axis=1, keepdims=True)
            acc_scr[...] = alpha * acc_scr[...] + jnp.dot(p.astype(BF16), v_ref[rows, :],
                                                          preferred_element_type=F32)
            m_scr[...] = m_next

        _key_tiles(qi, tile)
        l = l_scr[...]
        o = acc_scr[...] / l
        g = g_ref[...]
        o_ref[...] = o
        lse_ref[...] = m_scr[...] + jnp.log2(l)
        y_ref[...] = (o * (g * _sigmoid(g))).astype(BF16)

    qblk = pl.BlockSpec((tq, dh), lambda h, i: (i, h))
    return pl.pallas_call(
        body, name=name, grid=(heads, nq),
        in_specs=[qblk,
                  pl.BlockSpec((S, dh), lambda h, i: (0, heads + h)),
                  pl.BlockSpec((S, dh), lambda h, i: (0, 2 * heads + h)),
                  pl.BlockSpec((1, nq, 1, tq), lambda h, i: (h, 0, 0, 0)),
                  qblk],
        out_specs=[qblk, qblk, qblk],
        out_shape=[jax.ShapeDtypeStruct((S, W), BF16), jax.ShapeDtypeStruct((S, W), F32),
                   jax.ShapeDtypeStruct((S, W), F32)],
        scratch_shapes=[pltpu.VMEM((tq, LANES), F32), pltpu.VMEM((tq, LANES), F32), pltpu.VMEM((tq, dh), F32)],
        compiler_params=_params("parallel", "arbitrary"),
    )(qkv, qkv, qkv, cneg, gate)


def _flash_bwd(qkv, do, delta, lse, cneg, heads, name):
    S, W3 = qkv.shape
    W = W3 // 3
    dh = W // heads
    tq = _tile(S, FLASH_TILE)
    nq = S // tq
    scale = dh ** -0.5
    c1 = scale * LOG2E

    def body(q_ref, k_ref, v_ref, b_ref, do_ref, delta_ref, lse_ref,
             dq_ref, dk_ref, dv_ref, db_ref, rs_ref, dq_scr, dk_scr, dv_scr):
        qi = pl.program_id(1)

        @pl.when(qi == 0)
        def _():
            dk_scr[...] = jnp.zeros_like(dk_scr)
            dv_scr[...] = jnp.zeros_like(dv_scr)
            db_ref[...] = jnp.zeros_like(db_ref)

        dq_scr[...] = jnp.zeros_like(dq_scr)
        rs_ref[...] = jnp.zeros_like(rs_ref)
        q, d_o = q_ref[...], do_ref[...]
        lse = lse_ref[...][:, :1]
        delta = delta_ref[...][:, :1]

        def tile(j, diagonal):
            rows = pl.ds(pl.multiple_of(j * tq, tq), tq)
            k, v = k_ref[rows, :], v_ref[rows, :]
            t = lax.dot_general(q, k, NT_DIMS, preferred_element_type=F32) * c1 + b_ref[0, j] * LOG2E
            p = jnp.exp2(t - lse)
            if diagonal:
                p = jnp.where(_causal_mask(tq), p, 0.0)
            dp = lax.dot_general(d_o, v, NT_DIMS, preferred_element_type=F32)
            ds = p * (dp - delta)
            dv_scr[rows, :] += lax.dot_general(p.astype(BF16), d_o, TN_DIMS, preferred_element_type=F32)
            db_ref[0, j] += jnp.sum(ds.reshape(tq // SUBLANES, SUBLANES, tq), axis=0)
            dsb = (ds * scale).astype(BF16)
            dk_scr[rows, :] += lax.dot_general(dsb, q, TN_DIMS, preferred_element_type=F32)
            dq_scr[...] += jnp.dot(dsb, k, preferred_element_type=F32)
            rs_ref[...] += jnp.sum(ds, axis=1, keepdims=True)

        _key_tiles(qi, tile)
        dq_ref[...] = dq_scr[...].astype(BF16)

        @pl.when(qi == nq - 1)
        def _():
            dk_ref[...] = dk_scr[...].astype(BF16)
            dv_ref[...] = dv_scr[...].astype(BF16)

    qblk = pl.BlockSpec((tq, dh), lambda h, i: (i, h))
    head = pl.BlockSpec((S, dh), lambda h, i: (0, h))
    return pl.pallas_call(
        body, name=name, grid=(heads, nq),
        in_specs=[qblk,
                  pl.BlockSpec((S, dh), lambda h, i: (0, heads + h)),
                  pl.BlockSpec((S, dh), lambda h, i: (0, 2 * heads + h)),
                  pl.BlockSpec((1, nq, 1, tq), lambda h, i: (h, 0, 0, 0)),
                  qblk, qblk, qblk],
        out_specs=[qblk, head, head,
                   pl.BlockSpec((1, nq, SUBLANES, tq), lambda h, i: (h, 0, 0, 0)),
                   qblk],
        out_shape=[jax.ShapeDtypeStruct((S, W), BF16), jax.ShapeDtypeStruct((S, W), BF16),
                   jax.ShapeDtypeStruct((S, W), BF16), jax.ShapeDtypeStruct((heads, nq, SUBLANES, tq), F32),
                   jax.ShapeDtypeStruct((S, W), F32)],
        scratch_shapes=[pltpu.VMEM((tq, dh), F32), pltpu.VMEM((S, dh), F32), pltpu.VMEM((S, dh), F32)],
        compiler_params=_params("parallel", "arbitrary"),
    )(qkv, qkv, qkv, cneg, do, delta, lse)


def _fox_gate_bwd(dy, o, gate, name):
    S, W = dy.shape
    ts = _tile(S, ROW_TILE)

    def body(dy_ref, o_ref, g_ref, do_ref, dg_ref, delta_ref):
        d, g, o_val = dy_ref[...], g_ref[...], o_ref[...]
        sg = _sigmoid(g)
        d_o = (d * (g * sg)).astype(BF16)
        do_ref[...] = d_o
        dg_ref[...] = (d * o_val * _dsilu(g, sg)).astype(BF16)
        prod = d_o.astype(F32) * o_val
        for h in range(W // LANES):
            cols = slice(h * LANES, (h + 1) * LANES)
            delta_ref[:, cols] = jnp.broadcast_to(jnp.sum(prod[:, cols], axis=1, keepdims=True), (ts, LANES))

    row = pl.BlockSpec((ts, W), lambda i: (i, 0))
    return pl.pallas_call(
        body, name=name, grid=(S // ts,),
        in_specs=[row, row, row], out_specs=[row, row, row],
        out_shape=[jax.ShapeDtypeStruct((S, W), BF16), jax.ShapeDtypeStruct((S, W), BF16),
                   jax.ShapeDtypeStruct((S, W), F32)],
        compiler_params=_params("parallel"),
    )(dy, o, gate)


def _shifted_copies(buf_ref, sh_ref, rows):
    for j in range(1, SUBLANES):
        sh_ref[j, 0:rows, :] = buf_ref[j:j + rows, :]


def _tap(buf_ref, sh_ref, r0, off, cols):
    j, base = off % SUBLANES, off - off % SUBLANES
    if j == 0:
        return buf_ref[pl.ds(r0 + base, SUBLANES), cols]
    return sh_ref[j, pl.ds(r0 + base, SUBLANES), cols]


def _tap_weights(dw_ref, cols):
    return [jnp.broadcast_to(dw_ref[k:k + 1, cols], (SUBLANES, LANES)) for k in range(dw_ref.shape[0])]


def _conv_fwd(proj, dw, dw_b, ln_g, ln_b, name):
    S, C3 = proj.shape
    C = C3 // 3
    K = dw.shape[0]
    assert K - 1 <= CONV_HALO - 2
    ts = _tile(S, CONV_ROWS)
    hb = ts // CONV_HALO
    nrows = ts + CONV_HALO
    lead = CONV_HALO - (K - 1)

    def body(a_ref, b_ref, ah_ref, bh_ref, g_ref, dw_ref, dwb_ref, lg_ref, lb_ref, y_ref, u1_ref, buf, sh):
        first = pl.program_id(0) == 0
        buf[0:CONV_HALO, :] = jnp.where(first, 0.0, ah_ref[...] * _sigmoid(bh_ref[...]))

        def glu(r0):
            rows = pl.ds(r0, CONV_CHUNK)
            buf[pl.ds(r0 + CONV_HALO, CONV_CHUNK), :] = a_ref[rows, :] * _sigmoid(b_ref[rows, :])
        _rows(ts, CONV_CHUNK, glu)
        _shifted_copies(buf, sh, nrows - SUBLANES)

        for s in range(C // LANES):
            cols = slice(s * LANES, (s + 1) * LANES)
            w = _tap_weights(dw_ref, cols)
            bias = jnp.broadcast_to(dwb_ref[:, cols], (SUBLANES, LANES))

            def taps(r0, w=w, bias=bias, cols=cols):
                for u in range(CONV_CHUNK // SUBLANES):
                    r = r0 + u * SUBLANES
                    acc = bias
                    for k in range(K):
                        acc = acc + w[k] * _tap(buf, sh, r, lead + k, cols)
                    u1_ref[pl.ds(r, SUBLANES), cols] = acc
            _rows(ts, CONV_CHUNK, taps)

        def norm(r0):
            rows = pl.ds(r0, CONV_CHUNK)
            u1 = u1_ref[rows, :]
            mu = jnp.mean(u1, axis=-1, keepdims=True)
            xc = u1 - mu
            rstd = lax.rsqrt(jnp.mean(xc * xc, axis=-1, keepdims=True) + LN_EPS)
            z = xc * rstd * lg_ref[...] + lb_ref[...]
            g = g_ref[rows, :]
            y_ref[rows, :] = ((z * _sigmoid(z)) * (g * _sigmoid(g))).astype(BF16)
        _rows(ts, CONV_CHUNK, norm)

    row = lambda col: pl.BlockSpec((ts, C), lambda i: (i, col))
    halo = lambda col: pl.BlockSpec((CONV_HALO, C), lambda i: (jnp.maximum(i * hb - 1, 0), col))
    vec = pl.BlockSpec((1, C), lambda i: (0, 0))
    return pl.pallas_call(
        body, name=name, grid=(S // ts,),
        in_specs=[row(0), row(1), halo(0), halo(1), row(2), pl.BlockSpec((K, C), lambda i: (0, 0)), vec, vec, vec],
        out_specs=[row(0), row(0)],
        out_shape=[jax.ShapeDtypeStruct((S, C), BF16), jax.ShapeDtypeStruct((S, C), F32)],
        scratch_shapes=[pltpu.VMEM((nrows, C), F32), pltpu.VMEM((SUBLANES, nrows, C), F32)],
        compiler_params=_params("parallel"),
    )(proj, proj, proj, proj, proj, dw, dw_b, ln_g, ln_b)


def _conv_bwd_norm(dy, proj, u1, ln_g, ln_b, name):
    S, C = dy.shape
    ts = _tile(S, ROW_TILE)
    groups = CONV_CHUNK // SUBLANES

    def fold(x):
        return jnp.sum(x.reshape(groups, SUBLANES, C), axis=0)

    def body(dy_ref, g_ref, u1_ref, lg_ref, lb_ref, du1_ref, dg_ref, sums_ref):
        @pl.when(pl.program_id(0) == 0)
        def _():
            sums_ref[...] = jnp.zeros_like(sums_ref)

        def chunk(r0):
            rows = pl.ds(r0, CONV_CHUNK)
            d, g, u1 = dy_ref[rows, :], g_ref[rows, :], u1_ref[rows, :]
            mu = jnp.mean(u1, axis=-1, keepdims=True)
            xc = u1 - mu
            rstd = lax.rsqrt(jnp.mean(xc * xc, axis=-1, keepdims=True) + LN_EPS)
            xh = xc * rstd
            z = xh * lg_ref[...] + lb_ref[...]
            sz, sg = _sigmoid(z), _sigmoid(g)
            dgate = d * (z * sz) * _dsilu(g, sg)
            dz = d * (g * sg) * _dsilu(z, sz)
            dxh = dz * lg_ref[...]
            du1 = rstd * (dxh - jnp.mean(dxh, axis=-1, keepdims=True)
                          - xh * jnp.mean(dxh * xh, axis=-1, keepdims=True))
            du1_ref[rows, :] = du1
            dg_ref[rows, :] = dgate.astype(BF16)
            sums_ref[0] += fold(dz * xh)
            sums_ref[1] += fold(dz)
            sums_ref[2] += fold(du1)
            sums_ref[3] += fold(dgate)
        _rows(ts, CONV_CHUNK, chunk)

    row = pl.BlockSpec((ts, C), lambda i: (i, 0))
    vec = pl.BlockSpec((1, C), lambda i: (0, 0))
    return pl.pallas_call(
        body, name=name, grid=(S // ts,),
        in_specs=[row, pl.BlockSpec((ts, C), lambda i: (i, 2)), row, vec, vec],
        out_specs=[row, row, pl.BlockSpec((4, SUBLANES, C), lambda i: (0, 0, 0))],
        out_shape=[jax.ShapeDtypeStruct((S, C), F32), jax.ShapeDtypeStruct((S, C), BF16),
                   jax.ShapeDtypeStruct((4, SUBLANES, C), F32)],
        compiler_params=_params("arbitrary"),
    )(dy, proj, u1, ln_g, ln_b)


def _conv_bwd_taps(du1, proj, dw, name):
    S, C = du1.shape
    K = dw.shape[0]
    ts = _tile(S, CONV_ROWS)
    hb = ts // CONV_HALO
    nblk = S // ts
    last_halo = S // CONV_HALO - 1
    nrows = ts + CONV_HALO
    lead = CONV_HALO - (K - 1)

    def body(d_ref, dn_ref, a_ref, b_ref, ah_ref, bh_ref, dw_ref, da_ref, db_ref, ddw_ref, sums_ref,
             ubuf, ush, dbuf, dsh):
        i = pl.program_id(0)

        @pl.when(i == 0)
        def _():
            ddw_ref[...] = jnp.zeros_like(ddw_ref)
            sums_ref[...] = jnp.zeros_like(sums_ref)

        ubuf[0:CONV_HALO, :] = jnp.where(i == 0, 0.0, ah_ref[...] * _sigmoid(bh_ref[...]))
        dbuf[ts:nrows, :] = jnp.where(i == nblk - 1, 0.0, dn_ref[...])

        def fill(r0):
            rows = pl.ds(r0, CONV_CHUNK)
            ubuf[pl.ds(r0 + CONV_HALO, CONV_CHUNK), :] = a_ref[rows, :] * _sigmoid(b_ref[rows, :])
            dbuf[rows, :] = d_ref[rows, :]
        _rows(ts, CONV_CHUNK, fill)
        _shifted_copies(ubuf, ush, nrows - SUBLANES)
        _shifted_copies(dbuf, dsh, nrows - SUBLANES)

        zero = jnp.zeros((SUBLANES, LANES), F32)
        for s in range(C // LANES):
            cols = slice(s * LANES, (s + 1) * LANES)
            w = _tap_weights(dw_ref, cols)

            def glu_bwd(j, sums, w=w, cols=cols):
                r0 = pl.multiple_of(j * CONV_CHUNK, CONV_CHUNK)
                parts = []
                for u in range(CONV_CHUNK // SUBLANES):
                    acc = zero
                    for k in range(K):
                        acc = acc + w[k] * _tap(dbuf, dsh, r0 + u * SUBLANES, K - 1 - k, cols)
                    parts.append(acc)
                du0 = jnp.concatenate(parts, axis=0)
                rows = pl.ds(r0, CONV_CHUNK)
                a, sb = a_ref[rows, cols], _sigmoid(b_ref[rows, cols])
                da = du0 * sb
                db = da * a * (1.0 - sb)
                da_ref[rows, cols] = da.astype(BF16)
                db_ref[rows, cols] = db.astype(BF16)
                fold = lambda x: jnp.sum(x.reshape(CONV_CHUNK // SUBLANES, SUBLANES, LANES), axis=0)
                return sums[0] + fold(da), sums[1] + fold(db)
            sa, sb_sum = lax.fori_loop(0, ts // CONV_CHUNK, glu_bwd, (zero, zero))
            sums_ref[0, :, cols] += sa
            sums_ref[1, :, cols] += sb_sum

            def tap_grads(j, accs, cols=cols):
                r0 = pl.multiple_of(j * CONV_CHUNK, CONV_CHUNK)
                rs = [r0 + u * SUBLANES for u in range(CONV_CHUNK // SUBLANES)]
                ds = [d_ref[pl.ds(r, SUBLANES), cols] for r in rs]
                out = []
                for k in range(K):
                    prods = [d * _tap(ubuf, ush, r, lead + k, cols) for d, r in zip(ds, rs)]
                    out.append(accs[k] + ((prods[0] + prods[1]) + (prods[2] + prods[3])))
                return tuple(out)
            assert CONV_CHUNK // SUBLANES == 4
            accs = lax.fori_loop(0, ts // CONV_CHUNK, tap_grads, (zero,) * K)
            for k in range(K):
                ddw_ref[k, :, cols] += accs[k]

    row = lambda col: pl.BlockSpec((ts, C), lambda i: (i, col))
    prev = lambda col: pl.BlockSpec((CONV_HALO, C), lambda i: (jnp.maximum(i * hb - 1, 0), col))
    nxt = pl.BlockSpec((CONV_HALO, C), lambda i: (jnp.minimum((i + 1) * hb, last_halo), 0))
    return pl.pallas_call(
        body, name=name, grid=(nblk,),
        in_specs=[row(0), nxt, row(0), row(1), prev(0), prev(1), pl.BlockSpec((K, C), lambda i: (0, 0))],
        out_specs=[row(0), row(0), pl.BlockSpec((K, SUBLANES, C), lambda i: (0, 0, 0)),
                   pl.BlockSpec((2, SUBLANES, C), lambda i: (0, 0, 0))],
        out_shape=[jax.ShapeDtypeStruct((S, C), BF16), jax.ShapeDtypeStruct((S, C), BF16),
                   jax.ShapeDtypeStruct((K, SUBLANES, C), F32), jax.ShapeDtypeStruct((2, SUBLANES, C), F32)],
        scratch_shapes=[pltpu.VMEM((nrows, C), F32), pltpu.VMEM((SUBLANES, nrows, C), F32),
                        pltpu.VMEM((nrows, C), F32), pltpu.VMEM((SUBLANES, nrows, C), F32)],
        compiler_params=_params("arbitrary"),
    )(du1, du1, proj, proj, proj, proj, dw)


def _adamw(parts, w, m, v, name):
    R, C = w.shape
    tr = _tile(R, 256)
    c1 = 1.0 - ADAM_B1 ** ADAM_STEP
    c2 = 1.0 - ADAM_B2 ** ADAM_STEP

    def body(p_ref, w_ref, m_ref, v_ref, g_ref, d_ref, nm_ref, nv_ref):
        g = p_ref[0].astype(F32)
        for d in range(1, N_DEV):
            g = g + p_ref[d].astype(F32)
        nm = ADAM_B1 * m_ref[...] + (1.0 - ADAM_B1) * g
        nv = ADAM_B2 * v_ref[...] + (1.0 - ADAM_B2) * (g * g)
        g_ref[...] = g
        nm_ref[...] = nm
        nv_ref[...] = nv
        d_ref[...] = -ADAM_LR * ((nm / c1) / (jnp.sqrt(nv / c2) + ADAM_EPS) + ADAM_WD * w_ref[...])

    row = pl.BlockSpec((tr, C), lambda i: (i, 0))
    out = jax.ShapeDtypeStruct((R, C), F32)
    return pl.pallas_call(
        body, name=name, grid=(R // tr,),
        in_specs=[pl.BlockSpec((N_DEV, tr, C), lambda i: (0, i, 0)), row, row, row],
        out_specs=[row, row, row, row], out_shape=[out, out, out, out],
        compiler_params=_params("parallel"),
    )(parts, w, m, v)


def _pad_lanes(a, width=LANES):
    return jnp.pad(a, ((0, 0), (0, width - a.shape[1])))


def _flat_rows(parts, width=LANES):
    flat = jnp.concatenate([p.reshape(-1) for p in parts])
    rows = -(-flat.shape[0] // width)
    rows = -(-rows // SUBLANES) * SUBLANES
    return jnp.pad(flat, (0, rows * width - flat.shape[0])).reshape(rows, width)


def _unflat(rows2d, shapes):
    flat = rows2d.reshape(-1)
    out, pos = [], 0
    for s in shapes:
        n = int(np.prod(s))
        out.append(flat[pos:pos + n].reshape(s))
        pos += n
    return out


def kernel(x, norm_g, fox_w_in, fox_b_f, fox_w_out, conv_w_in, conv_b_in, conv_dw, conv_dw_b, conv_ln_g, conv_ln_b, conv_w_out, final_norm_g, loss_target, m_norm_g, m_fox_w_in, m_fox_b_f, m_fox_w_out, m_conv_w_in, m_conv_b_in, m_conv_dw, m_conv_dw_b, m_conv_ln_g, m_conv_ln_b, m_conv_w_out, m_final_norm_g, v_norm_g, v_fox_w_in, v_fox_b_f, v_fox_w_out, v_conv_w_in, v_conv_b_in, v_conv_dw, v_conv_dw_b, v_conv_ln_g, v_conv_ln_b, v_conv_w_out, v_final_norm_g):
    h0 = x[0]
    target = loss_target[0]
    S, D = h0.shape
    depth = norm_g.shape[0]
    n_fox, _, fin_shard = fox_w_in.shape
    n_conv, _, cin_shard = conv_w_in.shape
    heads = fox_b_f.shape[1]
    W = fox_w_out.shape[1] * N_DEV
    C = conv_w_out.shape[1] * N_DEV
    assert fin_shard * N_DEV == 4 * W + heads and cin_shard * N_DEV == 3 * C and heads <= LANES
    is_fox = lambda i: i % 2 == 0

    def weight_shards(i):
        j = i // 2
        if is_fox(i):
            return fox_w_in[j].astype(BF16), fox_w_out[j].astype(BF16)
        return conv_w_in[j].astype(BF16), conv_w_out[j].astype(BF16)

    def full_weights(i, w_in_all, w_out_all):
        w = jnp.transpose(w_in_all, (1, 0, 2)).reshape(D, -1)
        w_out = w_out_all.reshape(-1, D)
        if is_fox(i):
            return dict(w_qkvg=w[:, :4 * W], w_f=_pad_lanes(w[:, 4 * W:]), w_out=w_out)
        return dict(w_in=w, w_out=w_out)

    small_shapes = [conv_b_in.shape, conv_dw.shape, conv_dw_b.shape, conv_ln_g.shape, conv_ln_b.shape]
    small = _flat_rows([conv_b_in, conv_dw, conv_dw_b, conv_ln_g, conv_ln_b])
    small_all = _exchange(small, "ag_small", True)
    b_in_s, dw_s, dwb_s, lng_s, lnb_s = zip(*[_unflat(small_all[d], small_shapes) for d in range(N_DEV)])
    cat = lambda parts, axis: jnp.concatenate(parts, axis=axis)
    conv_b_in_f = cat(b_in_s, 1)
    conv_dw_f = cat(dw_s, 2)
    conv_dw_b_f = cat(dwb_s, 1)
    conv_ln_g_f = cat(lng_s, 1)
    conv_ln_b_f = cat(lnb_s, 1)

    s_in, s_out = weight_shards(0)
    weights = full_weights(0, _exchange(s_in, "ag_w_in0", True), _exchange(s_out, "ag_w_out0", True))

    h = h0
    saved = []
    for i in range(depth):
        j = i // 2
        nxt = weight_shards(i + 1) if i + 1 < depth else None
        carry = lambda which: (nxt[which], True) if nxt is not None else None
        unpack = lambda r: r if nxt is not None else (r, None)
        hn = _rms_fwd(h, norm_g[i:i + 1], f"rms_fwd{i}")
        wt = weights
        if is_fox(i):
            qkv, nxt_in = unpack(_matmul(hn, wt["w_qkvg"], "nn", BF16, f"fox_qkv{i}", n_out=3 * W, carry=carry(0)))
            gate, nxt_out = unpack(_matmul(hn, wt["w_qkvg"], "nn", F32, f"fox_gate{i}", b_col_off=3 * W, n_out=W,
                                           carry=carry(1)))
            f = _matmul(hn, wt["w_f"], "nn", F32, f"fox_f{i}")
            b_f = _pad_lanes(fox_b_f[j:j + 1])
            c = _fgate_fwd(f, b_f, f"fgate_fwd{i}")
            tq = _tile(S, FLASH_TILE)
            cneg = (-c[:, :heads]).T.reshape(heads, S // tq, 1, tq)
            y, o, lse = _flash_fwd(qkv, gate, cneg, heads, f"flash_fwd{i}")
            h_next = _matmul(y, wt["w_out"], "nn", F32, f"out_proj{i}", add=h)
            saved.append(dict(h=h, hn=hn, qkv=qkv, gate=gate, f=f, b_f=b_f, cneg=cneg, y=y, o=o, lse=lse, **wt))
        else:
            proj, nxt_in = unpack(_matmul(hn, wt["w_in"], "nn", F32, f"conv_in{i}", bias=conv_b_in_f[j:j + 1],
                                          carry=carry(0)))
            y, u1 = _conv_fwd(proj, conv_dw_f[j], conv_dw_b_f[j:j + 1], conv_ln_g_f[j:j + 1],
                              conv_ln_b_f[j:j + 1], f"conv_fwd{i}")
            h_next, nxt_out = unpack(_matmul(y, wt["w_out"], "nn", F32, f"out_proj{i}", add=h, carry=carry(1)))
            saved.append(dict(h=h, hn=hn, proj=proj, y=y, u1=u1, **wt))
        h = h_next
        if nxt is not None:
            weights = full_weights(i + 1, nxt_in, nxt_out)

    dh, loss_part, dg_final = _loss_head(h, final_norm_g[None, :], target, "loss_head")

    def shard_cols(g, shard):
        return jnp.transpose(g.reshape(g.shape[0], N_DEV, shard), (1, 0, 2))

    d_norm_g = [None] * depth
    d_fox_b_f = [None] * n_fox
    d_conv_small = [None] * n_conv
    summed_in = [None] * depth
    summed_out = [None] * depth
    pending = None
    for i in reversed(range(depth)):
        j = i // 2
        sv = saved[i]
        carry = lambda which: (pending[which], False) if pending is not None else None
        unpack = lambda r: r if pending is not None else (r, None)
        dy = _matmul(dh, sv["w_out"], "nt", F32, f"d_out_proj{i}")
        dw_out, got_out = unpack(_matmul(sv["y"], dh, "tn", F32, f"dw_out{i}", carry=carry(1)))
        if is_fox(i):
            do, dgate, delta = _fox_gate_bwd(dy, sv["o"], sv["gate"], f"fox_gate_bwd{i}")
            dq, dk, dv, colsum, rowsum = _flash_bwd(sv["qkv"], do, delta, sv["lse"], sv["cneg"], heads,
                                                    f"flash_bwd{i}")
            rowsum = jnp.max(rowsum.reshape(S, heads, LANES), axis=2)
            dc = _pad_lanes(rowsum - jnp.sum(colsum, axis=2).reshape(heads, S).T)
            df, dbf = _fgate_bwd(dc, sv["f"], sv["b_f"], f"fgate_bwd{i}")
            dproj = jnp.concatenate([dq, dk, dv, dgate], axis=1)
            dhn = _matmul(df, sv["w_f"], "nt", F32, f"d_fox_f{i}")
            dhn, got_in = unpack(_matmul(dproj, sv["w_qkvg"], "nt", F32, f"d_fox_in{i}", add=dhn, carry=carry(0)))
            dw_qkvg = _matmul(sv["hn"], dproj, "tn", F32, f"dw_fox_in{i}")
            dw_f = _matmul(sv["hn"], df, "tn", F32, f"dw_fox_f{i}")
            dw_in = shard_cols(jnp.concatenate([dw_qkvg, dw_f[:, :heads]], axis=1), fin_shard)
            d_fox_b_f[j] = jnp.sum(dbf, axis=0)[:heads]
        else:
            du1, dgate, nsums = _conv_bwd_norm(dy, sv["proj"], sv["u1"], conv_ln_g_f[j:j + 1],
                                               conv_ln_b_f[j:j + 1], f"conv_bwd_norm{i}")
            da, db, ddw, absums = _conv_bwd_taps(du1, sv["proj"], conv_dw_f[j], f"conv_bwd_taps{i}")
            dproj = jnp.concatenate([da, db, dgate], axis=1)
            dhn = _matmul(dproj, sv["w_in"], "nt", F32, f"d_conv_in{i}")
            dw_in, got_in = unpack(_matmul(sv["hn"], dproj, "tn", F32, f"dw_conv_in{i}", carry=carry(0)))
            dw_in = shard_cols(dw_in, cin_shard)
            nsum = jnp.sum(nsums, axis=1)
            absum = jnp.sum(absums, axis=1)
            d_conv_small[j] = dict(b_in=jnp.concatenate([absum[0], absum[1], nsum[3]]),
                                   dw=jnp.sum(ddw, axis=1), dw_b=nsum[2], ln_g=nsum[0], ln_b=nsum[1])
        if pending is not None:
            summed_in[i + 1], summed_out[i + 1] = got_in, got_out
        pending = (dw_in.astype(BF16), dw_out.reshape(N_DEV, -1, D).astype(BF16))
        dh, dg = _rms_bwd(sv["h"], norm_g[i:i + 1], dhn, dh, f"rms_bwd{i}")
        d_norm_g[i] = jnp.sum(dg, axis=0)
    summed_in[0] = _exchange(pending[0], "rs_w_in0", False)
    summed_out[0] = _exchange(pending[1], "rs_w_out0", False)
    grad_x = dh[None]

    def small_for(d):
        sl = lambda a, n: a[..., d * n:(d + 1) * n]
        return _flat_rows([
            jnp.stack([sl(s["b_in"], 3 * C // N_DEV) for s in d_conv_small]),
            jnp.stack([sl(s["dw"], C // N_DEV) for s in d_conv_small]),
            jnp.stack([sl(s["dw_b"], C // N_DEV) for s in d_conv_small]),
            jnp.stack([sl(s["ln_g"], C // N_DEV) for s in d_conv_small]),
            jnp.stack([sl(s["ln_b"], C // N_DEV) for s in d_conv_small])])
    r_small = _exchange(jnp.stack([small_for(d) for d in range(N_DEV)]), "rs_small", False)

    rep_shapes = [norm_g.shape, fox_b_f.shape, final_norm_g.shape, (1,)]
    rep_part = _flat_rows([jnp.stack(d_norm_g), jnp.stack(d_fox_b_f), jnp.sum(dg_final, axis=0),
                           jnp.sum(loss_part[:, 0])[None]])
    r_rep = _exchange(rep_part, "ag_replicated", True)

    def update(parts, w, m, v, name):
        two_d = (-1, w.shape[-1])
        stacked = jnp.stack(parts, axis=1).reshape((N_DEV,) + w.reshape(two_d).shape)
        res = _adamw(stacked, w.reshape(two_d), m.reshape(two_d), v.reshape(two_d), name)
        return [r.reshape(w.shape) for r in res]

    fox_layers = [i for i in range(depth) if is_fox(i)]
    conv_layers = [i for i in range(depth) if not is_fox(i)]
    u_fin = update([summed_in[i] for i in fox_layers], fox_w_in, m_fox_w_in, v_fox_w_in, "adamw_fox_w_in")
    u_fout = update([summed_out[i] for i in fox_layers], fox_w_out, m_fox_w_out, v_fox_w_out, "adamw_fox_w_out")
    u_cin = update([summed_in[i] for i in conv_layers], conv_w_in, m_conv_w_in, v_conv_w_in, "adamw_conv_w_in")
    u_cout = update([summed_out[i] for i in conv_layers], conv_w_out, m_conv_w_out, v_conv_w_out,
                    "adamw_conv_w_out")
    u_small = _adamw(r_small, small,
                     _flat_rows([m_conv_b_in, m_conv_dw, m_conv_dw_b, m_conv_ln_g, m_conv_ln_b]),
                     _flat_rows([v_conv_b_in, v_conv_dw, v_conv_dw_b, v_conv_ln_g, v_conv_ln_b]), "adamw_small")
    zero1 = jnp.zeros((1,), F32)
    u_rep = _adamw(r_rep, _flat_rows([norm_g, fox_b_f, final_norm_g, zero1]),
                   _flat_rows([m_norm_g, m_fox_b_f, m_final_norm_g, zero1]),
                   _flat_rows([v_norm_g, v_fox_b_f, v_final_norm_g, zero1]), "adamw_replicated")

    outs = []
    loss = None
    for kind in range(4):
        b_in_k, dw_k, dwb_k, lng_k, lnb_k = _unflat(u_small[kind], small_shapes)
        ng_k, bf_k, fg_k, loss_k = _unflat(u_rep[kind], rep_shapes)
        if kind == 0:
            loss = loss_k[0]
        outs += [ng_k, u_fin[kind], bf_k, u_fout[kind], u_cin[kind], b_in_k, dw_k, dwb_k, lng_k, lnb_k,
                 u_cout[kind], fg_k]
    return (loss, grad_x, *outs)
```

```python
import functools

import numpy as np
import jax
import jax.numpy as jnp
from jax import lax
from jax.experimental import pallas as pl
from jax.experimental.pallas import tpu as pltpu

F32 = jnp.float32
BF16 = jnp.bfloat16
MESH_ID = pl.DeviceIdType.MESH

N_DEV = 8
RMS_EPS = 1e-6
LN_EPS = 1e-5
ADAM_LR = 0.001
ADAM_B1 = 0.9
ADAM_B2 = 0.999
ADAM_EPS = 1e-08
ADAM_WD = 0.01
ADAM_STEP = 10

LANES = 128
SUBLANES = 8
VMEM_LIMIT = 56 * 1024 * 1024
NEG_BIG = -1e30
CONV_HALO = 32
FLASH_TILE = 512
CONV_ROWS = 128
CONV_CHUNK = 32
FGATE_ROWS = 256
ROW_TILE = 256


def _params(*sem):
    return pltpu.CompilerParams(dimension_semantics=sem if sem else None, vmem_limit_bytes=VMEM_LIMIT)


def _tile(n, pref):
    if n <= pref:
        return n
    t = pref
    while n % t:
        t //= 2
    return t


def _sigmoid(x):
    return 1.0 / (1.0 + jnp.exp(-x))


def _dsilu(x, s):
    return s * (1.0 + x * (1.0 - s))


def _rows(n, chunk, fn):
    def step(i, carry):
        fn(pl.multiple_of(i * chunk, chunk))
        return carry
    lax.fori_loop(0, n // chunk, step, 0)


def _peer(k):
    x, y, c = lax.axis_index("x"), lax.axis_index("y"), lax.axis_index("c")
    px = 1 - x if (k >> 2) & 1 else x
    py = 1 - y if (k >> 1) & 1 else y
    pc = 1 - c if k & 1 else c
    return (px, py, pc), 4 * px + 2 * py + pc


def _exchange_copies(x_ref, o_ref, send_sems, recv_sems, local_sem, gather):
    _, me = _peer(0)
    mine = x_ref if gather else x_ref.at[me]
    local = pltpu.make_async_copy(mine, o_ref.at[me], local_sem)
    sends, arrivals = [], []
    for k in range(1, N_DEV):
        peer, pidx = _peer(k)
        sems = dict(send_sem=send_sems.at[k - 1], recv_sem=recv_sems.at[k - 1], device_id=peer,
                    device_id_type=MESH_ID)
        sends.append(pltpu.make_async_remote_copy(src_ref=x_ref if gather else x_ref.at[pidx],
                                                  dst_ref=o_ref.at[me], **sems))
        arrivals.append(pltpu.make_async_remote_copy(src_ref=mine, dst_ref=o_ref.at[pidx], **sems))
    return local, sends, arrivals


def _exchange_start(*refs, gather):
    local, sends, _ = _exchange_copies(*refs, gather)
    local.start()
    for cp in sends:
        cp.start()


def _exchange_wait(*refs, gather):
    local, sends, arrivals = _exchange_copies(*refs, gather)
    for cp in arrivals:
        cp.wait_recv()
    for cp in sends:
        cp.wait_send()
    local.wait()


EXCHANGE_SCRATCH = [pltpu.SemaphoreType.DMA((N_DEV - 1,)), pltpu.SemaphoreType.DMA((N_DEV - 1,)),
                    pltpu.SemaphoreType.DMA]


def _exchange_shape(x, gather):
    return jax.ShapeDtypeStruct((N_DEV,) + x.shape if gather else x.shape, x.dtype)


def _exchange(x, name, gather):
    def body(*refs):
        _exchange_start(*refs, gather=gather)
        _exchange_wait(*refs, gather=gather)

    return pl.pallas_call(
        body, name=name,
        out_shape=_exchange_shape(x, gather),
        in_specs=[pl.BlockSpec(memory_space=pl.ANY)],
        out_specs=pl.BlockSpec(memory_space=pl.ANY),
        scratch_shapes=list(EXCHANGE_SCRATCH),
    )(x)


def _matmul(a, b, mode, out_dtype, name, bias=None, add=None, b_col_off=0, n_out=None, carry=None):
    M, K = a.shape
    N = n_out if n_out is not None else (b.shape[0] if mode == "nt" else b.shape[1])
    tm = _tile(M, 512)
    tn = _tile(N, 1024)
    k_cap = 4096 if a.dtype.itemsize == 2 and b.dtype.itemsize == 2 else 2048
    tk = next(K // d for d in range(1, K + 1) if K % d == 0 and K // d <= k_cap and (K // d) % LANES == 0)
    nm, nn, nk = M // tm, N // tn, K // tk
    assert b_col_off % tn == 0
    joff = b_col_off // tn
    dims = {"nn": (((1,), (0,)), ((), ())), "nt": (((1,), (1,)), ((), ()))}[mode]
    n_in = 2 + (bias is not None) + (add is not None)

    def body(*refs):
        a_ref, b_ref = refs[0], refs[1]
        bias_ref = refs[2] if bias is not None else None
        add_ref = refs[n_in - 1] if add is not None else None
        pos = n_in
        x_ref = o_ref = x_out_ref = None
        if carry is not None:
            x_ref, o_ref, x_out_ref = refs[pos], refs[pos + 1], refs[pos + 2]
            pos += 3
        else:
            o_ref = refs[pos]
            pos += 1
        acc_ref = None
        if nk > 1:
            acc_ref = refs[pos]
            pos += 1
        exchange_refs = (x_ref, x_out_ref) + tuple(refs[pos:])
        i, j, kk = pl.program_id(0), pl.program_id(1), pl.program_id(2)

        if carry is not None:
            @pl.when((i == 0) & (j == 0) & (kk == 0))
            def _():
                _exchange_start(*exchange_refs, gather=carry[1])

        part = lax.dot_general(a_ref[...].astype(BF16), b_ref[...].astype(BF16), dims,
                               preferred_element_type=F32)

        def finish(r):
            if bias_ref is not None:
                r = r + bias_ref[...]
            if add_ref is not None:
                r = r + add_ref[...]
            o_ref[...] = r.astype(o_ref.dtype)

        if nk == 1:
            finish(part)
        else:
            @pl.when(kk == 0)
            def _():
                acc_ref[...] = part

            @pl.when(kk > 0)
            def _():
                acc_ref[...] += part

            @pl.when(kk == nk - 1)
            def _():
                finish(acc_ref[...])

        if carry is not None:
            @pl.when((i == nm - 1) & (j == nn - 1) & (kk == nk - 1))
            def _():
                _exchange_wait(*exchange_refs, gather=carry[1])

    a_spec = pl.BlockSpec((tm, tk), lambda i, j, k: (i, k))
    if mode == "nt":
        b_spec = pl.BlockSpec((tn, tk), lambda i, j, k: (j, k))
    else:
        b_spec = pl.BlockSpec((tk, tn), lambda i, j, k: (k, j + joff))
    in_specs = [a_spec, b_spec]
    args = [a, b]
    if bias is not None:
        in_specs.append(pl.BlockSpec((1, tn), lambda i, j, k: (0, j)))
        args.append(bias)
    if add is not None:
        in_specs.append(pl.BlockSpec((tm, tn), lambda i, j, k: (i, j)))
        args.append(add)
    out_specs = [pl.BlockSpec((tm, tn), lambda i, j, k: (i, j))]
    out_shape = [jax.ShapeDtypeStruct((M, N), out_dtype)]
    scratch = [pltpu.VMEM((tm, tn), F32)] if nk > 1 else []
    if carry is not None:
        in_specs.append(pl.BlockSpec(memory_space=pl.ANY))
        args.append(carry[0])
        out_specs.append(pl.BlockSpec(memory_space=pl.ANY))
        out_shape.append(_exchange_shape(*carry))
        scratch += EXCHANGE_SCRATCH
    res = pl.pallas_call(
        body, name=name,
        grid=(nm, nn, nk),
        in_specs=in_specs, out_specs=out_specs, out_shape=out_shape, scratch_shapes=scratch,
        compiler_params=_params(*(("arbitrary",) * 3 if carry is not None else ("parallel", "parallel", "arbitrary"))),
    )(*args)
    return res if carry is not None else res[0]


def _rms_fwd(h, g, name):
    S, D = h.shape
    ts = _tile(S, 512)

    def body(h_ref, g_ref, o_ref):
        x = h_ref[...]
        r = lax.rsqrt(jnp.mean(x * x, axis=-1, keepdims=True) + RMS_EPS)
        o_ref[...] = (x * r * g_ref[...]).astype(BF16)

    return pl.pallas_call(
        body, name=name, grid=(S // ts,),
        in_specs=[pl.BlockSpec((ts, D), lambda i: (i, 0)), pl.BlockSpec((1, D), lambda i: (0, 0))],
        out_specs=pl.BlockSpec((ts, D), lambda i: (i, 0)),
        out_shape=jax.ShapeDtypeStruct((S, D), BF16),
        compiler_params=_params("parallel"),
    )(h, g)


def _rms_bwd_block(x, g, dy):
    r = lax.rsqrt(jnp.mean(x * x, axis=-1, keepdims=True) + RMS_EPS)
    xr = x * r
    t = dy * g
    dx = r * (t - xr * jnp.mean(t * xr, axis=-1, keepdims=True))
    return dx, dy * xr


def _rms_bwd(h, g, dhn, dh, name):
    S, D = h.shape
    ts = _tile(S, ROW_TILE)

    def body(h_ref, g_ref, dhn_ref, dh_ref, o_ref, o16_ref, dg_ref):
        dx, dgt = _rms_bwd_block(h_ref[...], g_ref[...], dhn_ref[...])
        out = dh_ref[...] + dx
        o_ref[...] = out
        o16_ref[...] = out.astype(BF16)
        part = jnp.sum(dgt.reshape(ts // SUBLANES, SUBLANES, D), axis=0)

        @pl.when(pl.program_id(0) == 0)
        def _():
            dg_ref[...] = part

        @pl.when(pl.program_id(0) > 0)
        def _():
            dg_ref[...] += part

    row = pl.BlockSpec((ts, D), lambda i: (i, 0))
    return pl.pallas_call(
        body, name=name, grid=(S // ts,),
        in_specs=[row, pl.BlockSpec((1, D), lambda i: (0, 0)), row, row],
        out_specs=[row, row, pl.BlockSpec((SUBLANES, D), lambda i: (0, 0))],
        out_shape=[jax.ShapeDtypeStruct((S, D), F32), jax.ShapeDtypeStruct((S, D), BF16),
                   jax.ShapeDtypeStruct((SUBLANES, D), F32)],
        compiler_params=_params("arbitrary"),
    )(h, g, dhn, dh)


def _loss_head(h, g, target, name):
    S, D = h.shape
    ts = _tile(S, ROW_TILE)

    def body(h_ref, g_ref, t_ref, o_ref, o16_ref, loss_ref, dg_ref):
        x = h_ref[...]
        gg = g_ref[...]
        r = lax.rsqrt(jnp.mean(x * x, axis=-1, keepdims=True) + RMS_EPS)
        err = x * r * gg - t_ref[...]
        row_loss = 0.5 * jnp.mean(err * err, axis=-1, keepdims=True)
        dx, dgt = _rms_bwd_block(x, gg, err * (1.0 / D))
        o_ref[...] = dx
        o16_ref[...] = dx.astype(BF16)
        part = jnp.sum(dgt.reshape(ts // SUBLANES, SUBLANES, D), axis=0)
        lpart = jnp.sum(jnp.broadcast_to(row_loss, (ts, LANES)).reshape(ts // SUBLANES, SUBLANES, LANES), axis=0)

        @pl.when(pl.program_id(0) == 0)
        def _():
            dg_ref[...] = part
            loss_ref[...] = lpart

        @pl.when(pl.program_id(0) > 0)
        def _():
            dg_ref[...] += part
            loss_ref[...] += lpart

    row = pl.BlockSpec((ts, D), lambda i: (i, 0))
    return pl.pallas_call(
        body, name=name, grid=(S // ts,),
        in_specs=[row, pl.BlockSpec((1, D), lambda i: (0, 0)), row],
        out_specs=[row, row, pl.BlockSpec((SUBLANES, LANES), lambda i: (0, 0)),
                   pl.BlockSpec((SUBLANES, D), lambda i: (0, 0))],
        out_shape=[jax.ShapeDtypeStruct((S, D), F32), jax.ShapeDtypeStruct((S, D), BF16),
                   jax.ShapeDtypeStruct((SUBLANES, LANES), F32), jax.ShapeDtypeStruct((SUBLANES, D), F32)],
        compiler_params=_params("arbitrary"),
    )(h, g, target)


def _split3(x):
    hi = x.astype(BF16)
    r1 = x - hi.astype(F32)
    mid = r1.astype(BF16)
    lo = (r1 - mid.astype(F32)).astype(BF16)
    return hi, mid, lo


def _tri_sum(tri, x):
    hi, mid, lo = _split3(x)
    dot = functools.partial(jnp.dot, preferred_element_type=F32)
    return dot(tri, hi) + dot(tri, mid) + dot(tri, lo)


def _fgate_fwd(f, b_f, name):
    S = f.shape[0]
    tb = _tile(S, FGATE_ROWS)

    def body(f_ref, b_ref, c_ref, carry_ref):
        @pl.when(pl.program_id(0) == 0)
        def _():
            carry_ref[...] = jnp.zeros_like(carry_ref)

        x = f_ref[...] + b_ref[...]
        lf = jnp.minimum(x, 0.0) - jnp.log1p(jnp.exp(-jnp.abs(x)))
        r = lax.broadcasted_iota(jnp.int32, (tb, tb), 0)
        c = lax.broadcasted_iota(jnp.int32, (tb, tb), 1)
        tri = (c <= r).astype(BF16)
        c_ref[...] = _tri_sum(tri, lf) + carry_ref[0:1, :]
        carry_ref[...] += _tri_sum(jnp.ones((SUBLANES, tb), BF16), lf)

    return pl.pallas_call(
        body, name=name, grid=(S // tb,),
        in_specs=[pl.BlockSpec((tb, LANES), lambda i: (i, 0)), pl.BlockSpec((1, LANES), lambda i: (0, 0))],
        out_specs=pl.BlockSpec((tb, LANES), lambda i: (i, 0)),
        out_shape=jax.ShapeDtypeStruct((S, LANES), F32),
        scratch_shapes=[pltpu.VMEM((SUBLANES, LANES), F32)],
        compiler_params=_params("arbitrary"),
    )(f, b_f)


def _fgate_bwd(dc, f, b_f, name):
    S = f.shape[0]
    tb = _tile(S, FGATE_ROWS)
    nb = S // tb

    def body(dc_ref, f_ref, b_ref, df_ref, db_ref, carry_ref):
        @pl.when(pl.program_id(0) == 0)
        def _():
            carry_ref[...] = jnp.zeros_like(carry_ref)
            db_ref[...] = jnp.zeros_like(db_ref)

        d = dc_ref[...]
        r = lax.broadcasted_iota(jnp.int32, (tb, tb), 0)
        c = lax.broadcasted_iota(jnp.int32, (tb, tb), 1)
        tri = (c >= r).astype(BF16)
        dlf = _tri_sum(tri, d) + carry_ref[0:1, :]
        carry_ref[...] += _tri_sum(jnp.ones((SUBLANES, tb), BF16), d)
        df = dlf * _sigmoid(-(f_ref[...] + b_ref[...]))
        df_ref[...] = df
        db_ref[...] += jnp.sum(df.reshape(tb // SUBLANES, SUBLANES, LANES), axis=0)

    rev = pl.BlockSpec((tb, LANES), lambda i: (nb - 1 - i, 0))
    return pl.pallas_call(
        body, name=name, grid=(nb,),
        in_specs=[rev, rev, pl.BlockSpec((1, LANES), lambda i: (0, 0))],
        out_specs=[rev, pl.BlockSpec((SUBLANES, LANES), lambda i: (0, 0))],
        out_shape=[jax.ShapeDtypeStruct((S, LANES), F32), jax.ShapeDtypeStruct((SUBLANES, LANES), F32)],
        scratch_shapes=[pltpu.VMEM((SUBLANES, LANES), F32)],
        compiler_params=_params("arbitrary"),
    )(dc, f, b_f)


LOG2E = 1.4426950408889634
NT_DIMS = (((1,), (1,)), ((), ()))
TN_DIMS = (((0,), (0,)), ((), ()))


def _causal_mask(t):
    r = lax.broadcasted_iota(jnp.int32, (t, t), 0)
    c = lax.broadcasted_iota(jnp.int32, (t, t), 1)
    return r >= c


def _key_tiles(qi, tile):
    def pair(i, carry):
        tile(2 * i, False)
        tile(2 * i + 1, False)
        return carry
    lax.fori_loop(0, qi // 2, pair, 0)

    @pl.when(qi % 2 == 1)
    def _():
        tile(qi - 1, False)

    tile(qi, True)


def _flash_fwd(qkv, gate, cneg, heads, name):
    S, W3 = qkv.shape
    W = W3 // 3
    dh = W // heads
    assert dh == LANES
    tq = _tile(S, FLASH_TILE)
    nq = S // tq
    c1 = dh ** -0.5 * LOG2E

    def body(q_ref, k_ref, v_ref, b_ref, g_ref, y_ref, o_ref, lse_ref, m_scr, l_scr, acc_scr):
        qi = pl.program_id(1)
        m_scr[...] = jnp.full_like(m_scr, NEG_BIG)
        l_scr[...] = jnp.zeros_like(l_scr)
        acc_scr[...] = jnp.zeros_like(acc_scr)
        q = q_ref[...]

        def tile(j, diagonal):
            rows = pl.ds(pl.multiple_of(j * tq, tq), tq)
            t = lax.dot_general(q, k_ref[rows, :], NT_DIMS, preferred_element_type=F32) * c1 + b_ref[0, j] * LOG2E
            if diagonal:
                t = jnp.where(_causal_mask(tq), t, NEG_BIG)
            m_prev = m_scr[...]
            m_next = jnp.maximum(m_prev, jnp.max(t, axis=1, keepdims=True))
            alpha = jnp.exp2(m_prev - m_next)
            p = jnp.exp2(t - m_next[:, :1])
            l_scr[...] = alpha * l_scr[...] + jnp.sum(p, axis=1, keepdims=True)
            acc_scr[...] = alpha * acc_scr[...] + jnp.dot(p.astype(BF16), v_ref[rows, :],
                                                          preferred_element_type=F32)
            m_scr[...] = m_next

        _key_tiles(qi, tile)
        l = l_scr[...]
        o = acc_scr[...] / l
        g = g_ref[...]
        o_ref[...] = o
        lse_ref[...] = m_scr[...] + jnp.log2(l)
        y_ref[...] = (o * (g * _sigmoid(g))).astype(BF16)

    qblk = pl.BlockSpec((tq, dh), lambda h, i: (i, h))
    return pl.pallas_call(
        body, name=name, grid=(heads, nq),
        in_specs=[qblk,
                  pl.BlockSpec((S, dh), lambda h, i: (0, heads + h)),
                  pl.BlockSpec((S, dh), lambda h, i: (0, 2 * heads + h)),
                  pl.BlockSpec((1, nq, 1, tq), lambda h, i: (h, 0, 0, 0)),
                  qblk],
        out_specs=[qblk, qblk, qblk],
        out_shape=[jax.ShapeDtypeStruct((S, W), BF16), jax.ShapeDtypeStruct((S, W), F32),
                   jax.ShapeDtypeStruct((S, W), F32)],
        scratch_shapes=[pltpu.VMEM((tq, LANES), F32), pltpu.VMEM((tq, LANES), F32), pltpu.VMEM((tq, dh), F32)],
        compiler_params=_params("parallel", "arbitrary"),
    )(qkv, qkv, qkv, cneg, gate)


def _flash_bwd(qkv, do, delta, lse, cneg, heads, name):
    S, W3 = qkv.shape
    W = W3 // 3
    dh = W // heads
    tq = _tile(S, FLASH_TILE)
    nq = S // tq
    scale = dh ** -0.5
    c1 = scale * LOG2E

    def body(q_ref, k_ref, v_ref, b_ref, do_ref, delta_ref, lse_ref,
             dq_ref, dk_ref, dv_ref, db_ref, rs_ref, dq_scr, dk_scr, dv_scr):
        qi = pl.program_id(1)

        @pl.when(qi == 0)
        def _():
            dk_scr[...] = jnp.zeros_like(dk_scr)
            dv_scr[...] = jnp.zeros_like(dv_scr)
            db_ref[...] = jnp.zeros_like(db_ref)

        dq_scr[...] = jnp.zeros_like(dq_scr)
        rs_ref[...] = jnp.zeros_like(rs_ref)
        q, d_o = q_ref[...], do_ref[...]
        lse = lse_ref[...][:, :1]
        delta = delta_ref[...][:, :1]

        def tile(j, diagonal):
            rows = pl.ds(pl.multiple_of(j * tq, tq), tq)
            k, v = k_ref[rows, :], v_ref[rows, :]
            t = lax.dot_general(q, k, NT_DIMS, preferred_element_type=F32) * c1 + b_ref[0, j] * LOG2E
            p = jnp.exp2(t - lse)
            if diagonal:
                p = jnp.where(_causal_mask(tq), p, 0.0)
            dp = lax.dot_general(d_o, v, NT_DIMS, preferred_element_type=F32)
            ds = p * (dp - delta)
            dv_scr[rows, :] += lax.dot_general(p.astype(BF16), d_o, TN_DIMS, preferred_element_type=F32)
            db_ref[0, j] += jnp.sum(ds.reshape(tq // SUBLANES, SUBLANES, tq), axis=0)
            dsb = (ds * scale).astype(BF16)
            dk_scr[rows, :] += lax.dot_general(dsb, q, TN_DIMS, preferred_element_type=F32)
            dq_scr[...] += jnp.dot(dsb, k, preferred_element_type=F32)
            rs_ref[...] += jnp.sum(ds, axis=1, keepdims=True)

        _key_tiles(qi, tile)
        dq_ref[...] = dq_scr[...].astype(BF16)

        @pl.when(qi == nq - 1)
        def _():
            dk_ref[...] = dk_scr[...].astype(BF16)
            dv_ref[...] = dv_scr[...].astype(BF16)

    qblk = pl.BlockSpec((tq, dh), lambda h, i: (i, h))
    head = pl.BlockSpec((S, dh), lambda h, i: (0, h))
    return pl.pallas_call(
        body, name=name, grid=(heads, nq),
        in_specs=[qblk,
                  pl.BlockSpec((S, dh), lambda h, i: (0, heads + h)),
                  pl.BlockSpec((S, dh), lambda h, i: (0, 2 * heads + h)),
                  pl.BlockSpec((1, nq, 1, tq), lambda h, i: (h, 0, 0, 0)),
                  qblk, qblk, qblk],
        out_specs=[qblk, head, head,
                   pl.BlockSpec((1, nq, SUBLANES, tq), lambda h, i: (h, 0, 0, 0)),
                   qblk],
        out_shape=[jax.ShapeDtypeStruct((S, W), BF16), jax.ShapeDtypeStruct((S, W), BF16),
                   jax.ShapeDtypeStruct((S, W), BF16), jax.ShapeDtypeStruct((heads, nq, SUBLANES, tq), F32),
                   jax.ShapeDtypeStruct((S, W), F32)],
        scratch_shapes=[pltpu.VMEM((tq, dh), F32), pltpu.VMEM((S, dh), F32), pltpu.VMEM((S, dh), F32)],
        compiler_params=_params("parallel", "arbitrary"),
    )(qkv, qkv, qkv, cneg, do, delta, lse)


def _fox_gate_bwd(dy, o, gate, name):
    S, W = dy.shape
    ts = _tile(S, ROW_TILE)

    def body(dy_ref, o_ref, g_ref, do_ref, dg_ref, delta_ref):
        d, g, o_val = dy_ref[...], g_ref[...], o_ref[...]
        sg = _sigmoid(g)
        d_o = (d * (g * sg)).astype(BF16)
        do_ref[...] = d_o
        dg_ref[...] = (d * o_val * _dsilu(g, sg)).astype(BF16)
        prod = d_o.astype(F32) * o_val
        for h in range(W // LANES):
            cols = slice(h * LANES, (h + 1) * LANES)
            delta_ref[:, cols] = jnp.broadcast_to(jnp.sum(prod[:, cols], axis=1, keepdims=True), (ts, LANES))

    row = pl.BlockSpec((ts, W), lambda i: (i, 0))
    return pl.pallas_call(
        body, name=name, grid=(S // ts,),
        in_specs=[row, row, row], out_specs=[row, row, row],
        out_shape=[jax.ShapeDtypeStruct((S, W), BF16), jax.ShapeDtypeStruct((S, W), BF16),
                   jax.ShapeDtypeStruct((S, W), F32)],
        compiler_params=_params("parallel"),
    )(dy, o, gate)


def _shifted_copies(buf_ref, sh_ref, rows):
    for j in range(1, SUBLANES):
        sh_ref[j, 0:rows, :] = buf_ref[j:j + rows, :]


def _tap(buf_ref, sh_ref, r0, off, cols):
    j, base = off % SUBLANES, off - off % SUBLANES
    if j == 0:
        return buf_ref[pl.ds(r0 + base, SUBLANES), cols]
    return sh_ref[j, pl.ds(r0 + base, SUBLANES), cols]


def _tap_weights(dw_ref, cols):
    return [jnp.broadcast_to(dw_ref[k:k + 1, cols], (SUBLANES, LANES)) for k in range(dw_ref.shape[0])]


def _conv_fwd(proj, dw, dw_b, ln_g, ln_b, name):
    S, C3 = proj.shape
    C = C3 // 3
    K = dw.shape[0]
    assert K - 1 <= CONV_HALO - 2
    ts = _tile(S, CONV_ROWS)
    hb = ts // CONV_HALO
    nrows = ts + CONV_HALO
    lead = CONV_HALO - (K - 1)

    def body(a_ref, b_ref, ah_ref, bh_ref, g_ref, dw_ref, dwb_ref, lg_ref, lb_ref, y_ref, u1_ref, buf, sh):
        first = pl.program_id(0) == 0
        buf[0:CONV_HALO, :] = jnp.where(first, 0.0, ah_ref[...] * _sigmoid(bh_ref[...]))

        def glu(r0):
            rows = pl.ds(r0, CONV_CHUNK)
            buf[pl.ds(r0 + CONV_HALO, CONV_CHUNK), :] = a_ref[rows, :] * _sigmoid(b_ref[rows, :])
        _rows(ts, CONV_CHUNK, glu)
        _shifted_copies(buf, sh, nrows - SUBLANES)

        for s in range(C // LANES):
            cols = slice(s * LANES, (s + 1) * LANES)
            w = _tap_weights(dw_ref, cols)
            bias = jnp.broadcast_to(dwb_ref[:, cols], (SUBLANES, LANES))

            def taps(r0, w=w, bias=bias, cols=cols):
                for u in range(CONV_CHUNK // SUBLANES):
                    r = r0 + u * SUBLANES
                    acc = bias
                    for k in range(K):
                        acc = acc + w[k] * _tap(buf, sh, r, lead + k, cols)
                    u1_ref[pl.ds(r, SUBLANES), cols] = acc
            _rows(ts, CONV_CHUNK, taps)

        def norm(r0):
            rows = pl.ds(r0, CONV_CHUNK)
            u1 = u1_ref[rows, :]
            mu = jnp.mean(u1, axis=-1, keepdims=True)
            xc = u1 - mu
            rstd = lax.rsqrt(jnp.mean(xc * xc, axis=-1, keepdims=True) + LN_EPS)
            z = xc * rstd * lg_ref[...] + lb_ref[...]
            g = g_ref[rows, :]
            y_ref[rows, :] = ((z * _sigmoid(z)) * (g * _sigmoid(g))).astype(BF16)
        _rows(ts, CONV_CHUNK, norm)

    row = lambda col: pl.BlockSpec((ts, C), lambda i: (i, col))
    halo = lambda col: pl.BlockSpec((CONV_HALO, C), lambda i: (jnp.maximum(i * hb - 1, 0), col))
    vec = pl.BlockSpec((1, C), lambda i: (0, 0))
    return pl.pallas_call(
        body, name=name, grid=(S // ts,),
        in_specs=[row(0), row(1), halo(0), halo(1), row(2), pl.BlockSpec((K, C), lambda i: (0, 0)), vec, vec, vec],
        out_specs=[row(0), row(0)],
        out_shape=[jax.ShapeDtypeStruct((S, C), BF16), jax.ShapeDtypeStruct((S, C), F32)],
        scratch_shapes=[pltpu.VMEM((nrows, C), F32), pltpu.VMEM((SUBLANES, nrows, C), F32)],
        compiler_params=_params("parallel"),
    )(proj, proj, proj, proj, proj, dw, dw_b, ln_g, ln_b)


def _conv_bwd_norm(dy, proj, u1, ln_g, ln_b, name):
    S, C = dy.shape
    ts = _tile(S, ROW_TILE)
    groups = CONV_CHUNK // SUBLANES

    def fold(x):
        return jnp.sum(x.reshape(groups, SUBLANES, C), axis=0)

    def body(dy_ref, g_ref, u1_ref, lg_ref, lb_ref, du1_ref, dg_ref, sums_ref):
        @pl.when(pl.program_id(0) == 0)
        def _():
            sums_ref[...] = jnp.zeros_like(sums_ref)

        def chunk(r0):
            rows = pl.ds(r0, CONV_CHUNK)
            d, g, u1 = dy_ref[rows, :], g_ref[rows, :], u1_ref[rows, :]
            mu = jnp.mean(u1, axis=-1, keepdims=True)
            xc = u1 - mu
            rstd = lax.rsqrt(jnp.mean(xc * xc, axis=-1, keepdims=True) + LN_EPS)
            xh = xc * rstd
            z = xh * lg_ref[...] + lb_ref[...]
            sz, sg = _sigmoid(z), _sigmoid(g)
            dgate = d * (z * sz) * _dsilu(g, sg)
            dz = d * (g * sg) * _dsilu(z, sz)
            dxh = dz * lg_ref[...]
            du1 = rstd * (dxh - jnp.mean(dxh, axis=-1, keepdims=True)
                          - xh * jnp.mean(dxh * xh, axis=-1, keepdims=True))
            du1_ref[rows, :] = du1
            dg_ref[rows, :] = dgate.astype(BF16)
            sums_ref[0] += fold(dz * xh)
            sums_ref[1] += fold(dz)
            sums_ref[2] += fold(du1)
            sums_ref[3] += fold(dgate)
        _rows(ts, CONV_CHUNK, chunk)

    row = pl.BlockSpec((ts, C), lambda i: (i, 0))
    vec = pl.BlockSpec((1, C), lambda i: (0, 0))
    return pl.pallas_call(
        body, name=name, grid=(S // ts,),
        in_specs=[row, pl.BlockSpec((ts, C), lambda i: (i, 2)), row, vec, vec],
        out_specs=[row, row, pl.BlockSpec((4, SUBLANES, C), lambda i: (0, 0, 0))],
        out_shape=[jax.ShapeDtypeStruct((S, C), F32), jax.ShapeDtypeStruct((S, C), BF16),
                   jax.ShapeDtypeStruct((4, SUBLANES, C), F32)],
        compiler_params=_params("arbitrary"),
    )(dy, proj, u1, ln_g, ln_b)


def _conv_bwd_taps(du1, proj, dw, name):
    S, C = du1.shape
    K = dw.shape[0]
    ts = _tile(S, CONV_ROWS)
    hb = ts // CONV_HALO
    nblk = S // ts
    last_halo = S // CONV_HALO - 1
    nrows = ts + CONV_HALO
    lead = CONV_HALO - (K - 1)

    def body(d_ref, dn_ref, a_ref, b_ref, ah_ref, bh_ref, dw_ref, da_ref, db_ref, ddw_ref, sums_ref,
             ubuf, ush, dbuf, dsh):
        i = pl.program_id(0)

        @pl.when(i == 0)
        def _():
            ddw_ref[...] = jnp.zeros_like(ddw_ref)
            sums_ref[...] = jnp.zeros_like(sums_ref)

        ubuf[0:CONV_HALO, :] = jnp.where(i == 0, 0.0, ah_ref[...] * _sigmoid(bh_ref[...]))
        dbuf[ts:nrows, :] = jnp.where(i == nblk - 1, 0.0, dn_ref[...])

        def fill(r0):
            rows = pl.ds(r0, CONV_CHUNK)
            ubuf[pl.ds(r0 + CONV_HALO, CONV_CHUNK), :] = a_ref[rows, :] * _sigmoid(b_ref[rows, :])
            dbuf[rows, :] = d_ref[rows, :]
        _rows(ts, CONV_CHUNK, fill)
        _shifted_copies(ubuf, ush, nrows - SUBLANES)
        _shifted_copies(dbuf, dsh, nrows - SUBLANES)

        zero = jnp.zeros((SUBLANES, LANES), F32)
        for s in range(C // LANES):
            cols = slice(s * LANES, (s + 1) * LANES)
            w = _tap_weights(dw_ref, cols)

            def glu_bwd(j, sums, w=w, cols=cols):
                r0 = pl.multiple_of(j * CONV_CHUNK, CONV_CHUNK)
                parts = []
                for u in range(CONV_CHUNK // SUBLANES):
                    acc = zero
                    for k in range(K):
                        acc = acc + w[k] * _tap(dbuf, dsh, r0 + u * SUBLANES, K - 1 - k, cols)
                    parts.append(acc)
                du0 = jnp.concatenate(parts, axis=0)
                rows = pl.ds(r0, CONV_CHUNK)
                a, sb = a_ref[rows, cols], _sigmoid(b_ref[rows, cols])
                da = du0 * sb
                db = da * a * (1.0 - sb)
                da_ref[rows, cols] = da.astype(BF16)
                db_ref[rows, cols] = db.astype(BF16)
                fold = lambda x: jnp.sum(x.reshape(CONV_CHUNK // SUBLANES, SUBLANES, LANES), axis=0)
                return sums[0] + fold(da), sums[1] + fold(db)
            sa, sb_sum = lax.fori_loop(0, ts // CONV_CHUNK, glu_bwd, (zero, zero))
            sums_ref[0, :, cols] += sa
            sums_ref[1, :, cols] += sb_sum

            def tap_grads(j, accs, cols=cols):
                r0 = pl.multiple_of(j * CONV_CHUNK, CONV_CHUNK)
                rs = [r0 + u * SUBLANES for u in range(CONV_CHUNK // SUBLANES)]
                ds = [d_ref[pl.ds(r, SUBLANES), cols] for r in rs]
                out = []
                for k in range(K):
                    prods = [d * _tap(ubuf, ush, r, lead + k, cols) for d, r in zip(ds, rs)]
                    out.append(accs[k] + ((prods[0] + prods[1]) + (prods[2] + prods[3])))
                return tuple(out)
            assert CONV_CHUNK // SUBLANES == 4
            accs = lax.fori_loop(0, ts // CONV_CHUNK, tap_grads, (zero,) * K)
            for k in range(K):
                ddw_ref[k, :, cols] += accs[k]

    row = lambda col: pl.BlockSpec((ts, C), lambda i: (i, col))
    prev = lambda col: pl.BlockSpec((CONV_HALO, C), lambda i: (jnp.maximum(i * hb - 1, 0), col))
    nxt = pl.BlockSpec((CONV_HALO, C), lambda i: (jnp.minimum((i + 1) * hb, last_halo), 0))
    return pl.pallas_call(
        body, name=name, grid=(nblk,),
        in_specs=[row(0), nxt, row(0), row(1), prev(0), prev(1), pl.BlockSpec((K, C), lambda i: (0, 0))],
        out_specs=[row(0), row(0), pl.BlockSpec((K, SUBLANES, C), lambda i: (0, 0, 0)),
                   pl.BlockSpec((2, SUBLANES, C), lambda i: (0, 0, 0))],
        out_shape=[jax.ShapeDtypeStruct((S, C), BF16), jax.ShapeDtypeStruct((S, C), BF16),
                   jax.ShapeDtypeStruct((K, SUBLANES, C), F32), jax.ShapeDtypeStruct((2, SUBLANES, C), F32)],
        scratch_shapes=[pltpu.VMEM((nrows, C), F32), pltpu.VMEM((SUBLANES, nrows, C), F32),
                        pltpu.VMEM((nrows, C), F32), pltpu.VMEM((SUBLANES, nrows, C), F32)],
        compiler_params=_params("arbitrary"),
    )(du1, du1, proj, proj, proj, proj, dw)


def _adamw(parts, w, m, v, name):
    R, C = w.shape
    tr = _tile(R, 256)
    c1 = 1.0 - ADAM_B1 ** ADAM_STEP
    c2 = 1.0 - ADAM_B2 ** ADAM_STEP

    def body(p_ref, w_ref, m_ref, v_ref, g_ref, d_ref, nm_ref, nv_ref):
        g = p_ref[0].astype(F32)
        for d in range(1, N_DEV):
            g = g + p_ref[d].astype(F32)
        nm = ADAM_B1 * m_ref[...] + (1.0 - ADAM_B1) * g
        nv = ADAM_B2 * v_ref[...] + (1.0 - ADAM_B2) * (g * g)
        g_ref[...] = g
        nm_ref[...] = nm
        nv_ref[...] = nv
        d_ref[...] = -ADAM_LR * ((nm / c1) / (jnp.sqrt(nv / c2) + ADAM_EPS) + ADAM_WD * w_ref[...])

    row = pl.BlockSpec((tr, C), lambda i: (i, 0))
    out = jax.ShapeDtypeStruct((R, C), F32)
    return pl.pallas_call(
        body, name=name, grid=(R // tr,),
        in_specs=[pl.BlockSpec((N_DEV, tr, C), lambda i: (0, i, 0)), row, row, row],
        out_specs=[row, row, row, row], out_shape=[out, out, out, out],
        compiler_params=_params("parallel"),
    )(parts, w, m, v)


def _pad_lanes(a, width=LANES):
    return jnp.pad(a, ((0, 0), (0, width - a.shape[1])))


def _flat_rows(parts, width=LANES):
    flat = jnp.concatenate([p.reshape(-1) for p in parts])
    rows = -(-flat.shape[0] // width)
    rows = -(-rows // SUBLANES) * SUBLANES
    return jnp.pad(flat, (0, rows * width - flat.shape[0])).reshape(rows, width)


def _unflat(rows2d, shapes):
    flat = rows2d.reshape(-1)
    out, pos = [], 0
    for s in shapes:
        n = int(np.prod(s))
        out.append(flat[pos:pos + n].reshape(s))
        pos += n
    return out


def kernel(x, norm_g, fox_w_in, fox_b_f, fox_w_out, conv_w_in, conv_b_in, conv_dw, conv_dw_b, conv_ln_g, conv_ln_b, conv_w_out, final_norm_g, loss_target, m_norm_g, m_fox_w_in, m_fox_b_f, m_fox_w_out, m_conv_w_in, m_conv_b_in, m_conv_dw, m_conv_dw_b, m_conv_ln_g, m_conv_ln_b, m_conv_w_out, m_final_norm_g, v_norm_g, v_fox_w_in, v_fox_b_f, v_fox_w_out, v_conv_w_in, v_conv_b_in, v_conv_dw, v_conv_dw_b, v_conv_ln_g, v_conv_ln_b, v_conv_w_out, v_final_norm_g):
    h0 = x[0]
    target = loss_target[0]
    S, D = h0.shape
    depth = norm_g.shape[0]
    n_fox, _, fin_shard = fox_w_in.shape
    n_conv, _, cin_shard = conv_w_in.shape
    heads = fox_b_f.shape[1]
    W = fox_w_out.shape[1] * N_DEV
    C = conv_w_out.shape[1] * N_DEV
    assert fin_shard * N_DEV == 4 * W + heads and cin_shard * N_DEV == 3 * C and heads <= LANES
    is_fox = lambda i: i % 2 == 0

    def weight_shards(i):
        j = i // 2
        if is_fox(i):
            return fox_w_in[j].astype(BF16), fox_w_out[j].astype(BF16)
        return conv_w_in[j].astype(BF16), conv_w_out[j].astype(BF16)

    def full_weights(i, w_in_all, w_out_all):
        w = jnp.transpose(w_in_all, (1, 0, 2)).reshape(D, -1)
        w_out = w_out_all.reshape(-1, D)
        if is_fox(i):
            return dict(w_qkvg=w[:, :4 * W], w_f=_pad_lanes(w[:, 4 * W:]), w_out=w_out)
        return dict(w_in=w, w_out=w_out)

    small_shapes = [conv_b_in.shape, conv_dw.shape, conv_dw_b.shape, conv_ln_g.shape, conv_ln_b.shape]
    small = _flat_rows([conv_b_in, conv_dw, conv_dw_b, conv_ln_g, conv_ln_b])
    small_all = _exchange(small, "ag_small", True)
    b_in_s, dw_s, dwb_s, lng_s, lnb_s = zip(*[_unflat(small_all[d], small_shapes) for d in range(N_DEV)])
    cat = lambda parts, axis: jnp.concatenate(parts, axis=axis)
    conv_b_in_f = cat(b_in_s, 1)
    conv_dw_f = cat(dw_s, 2)
    conv_dw_b_f = cat(dwb_s, 1)
    conv_ln_g_f = cat(lng_s, 1)
    conv_ln_b_f = cat(lnb_s, 1)

    s_in, s_out = weight_shards(0)
    weights = full_weights(0, _exchange(s_in, "ag_w_in0", True), _exchange(s_out, "ag_w_out0", True))

    h = h0
    saved = []
    for i in range(depth):
        j = i // 2
        nxt = weight_shards(i + 1) if i + 1 < depth else None
        carry = lambda which: (nxt[which], True) if nxt is not None else None
        unpack = lambda r: r if nxt is not None else (r, None)
        hn = _rms_fwd(h, norm_g[i:i + 1], f"rms_fwd{i}")
        wt = weights
        if is_fox(i):
            qkv, nxt_in = unpack(_matmul(hn, wt["w_qkvg"], "nn", BF16, f"fox_qkv{i}", n_out=3 * W, carry=carry(0)))
            gate, nxt_out = unpack(_matmul(hn, wt["w_qkvg"], "nn", F32, f"fox_gate{i}", b_col_off=3 * W, n_out=W,
                                           carry=carry(1)))
            f = _matmul(hn, wt["w_f"], "nn", F32, f"fox_f{i}")
            b_f = _pad_lanes(fox_b_f[j:j + 1])
            c = _fgate_fwd(f, b_f, f"fgate_fwd{i}")
            tq = _tile(S, FLASH_TILE)
            cneg = (-c[:, :heads]).T.reshape(heads, S // tq, 1, tq)
            y, o, lse = _flash_fwd(qkv, gate, cneg, heads, f"flash_fwd{i}")
            h_next = _matmul(y, wt["w_out"], "nn", F32, f"out_proj{i}", add=h)
            saved.append(dict(h=h, hn_t=hn.T, qkv=qkv, gate=gate, f=f, b_f=b_f, cneg=cneg, y_t=y.T, o=o, lse=lse,
                              **wt))
        else:
            proj, nxt_in = unpack(_matmul(hn, wt["w_in"], "nn", F32, f"conv_in{i}", bias=conv_b_in_f[j:j + 1],
                                          carry=carry(0)))
            y, u1 = _conv_fwd(proj, conv_dw_f[j], conv_dw_b_f[j:j + 1], conv_ln_g_f[j:j + 1],
                              conv_ln_b_f[j:j + 1], f"conv_fwd{i}")
            h_next, nxt_out = unpack(_matmul(y, wt["w_out"], "nn", F32, f"out_proj{i}", add=h, carry=carry(1)))
            saved.append(dict(h=h, hn_t=hn.T, proj=proj, y_t=y.T, u1=u1, **wt))
        h = h_next
        if nxt is not None:
            weights = full_weights(i + 1, nxt_in, nxt_out)

    dh, dh16, loss_part, dg_final = _loss_head(h, final_norm_g[None, :], target, "loss_head")

    def shard_cols(g, shard):
        return jnp.transpose(g.reshape(g.shape[0], N_DEV, shard), (1, 0, 2))

    d_norm_g = [None] * depth
    d_fox_b_f = [None] * n_fox
    d_conv_small = [None] * n_conv
    summed_in = [None] * depth
    summed_out = [None] * depth
    pending = None
    for i in reversed(range(depth)):
        j = i // 2
        sv = saved[i]
        carry = lambda which: (pending[which], False) if pending is not None else None
        unpack = lambda r: r if pending is not None else (r, None)
        dy = _matmul(dh16, sv["w_out"], "nt", F32, f"d_out_proj{i}")
        dw_out, got_out = unpack(_matmul(sv["y_t"], dh16, "nn", F32, f"dw_out{i}", carry=carry(1)))
        if is_fox(i):
            do, dgate, delta = _fox_gate_bwd(dy, sv["o"], sv["gate"], f"fox_gate_bwd{i}")
            dq, dk, dv, colsum, rowsum = _flash_bwd(sv["qkv"], do, delta, sv["lse"], sv["cneg"], heads,
                                                    f"flash_bwd{i}")
            rowsum = jnp.max(rowsum.reshape(S, heads, LANES), axis=2)
            dc = _pad_lanes(rowsum - jnp.sum(colsum, axis=2).reshape(heads, S).T)
            df, dbf = _fgate_bwd(dc, sv["f"], sv["b_f"], f"fgate_bwd{i}")
            dproj = jnp.concatenate([dq, dk, dv, dgate], axis=1)
            dhn = _matmul(df, sv["w_f"], "nt", F32, f"d_fox_f{i}")
            dhn, got_in = unpack(_matmul(dproj, sv["w_qkvg"], "nt", F32, f"d_fox_in{i}", add=dhn, carry=carry(0)))
            dw_qkvg = _matmul(sv["hn_t"], dproj, "nn", F32, f"dw_fox_in{i}")
            dw_f = _matmul(sv["hn_t"], df, "nn", F32, f"dw_fox_f{i}")
            dw_in = shard_cols(jnp.concatenate([dw_qkvg, dw_f[:, :heads]], axis=1), fin_shard)
            d_fox_b_f[j] = jnp.sum(dbf, axis=0)[:heads]
        else:
            du1, dgate, nsums = _conv_bwd_norm(dy, sv["proj"], sv["u1"], conv_ln_g_f[j:j + 1],
                                               conv_ln_b_f[j:j + 1], f"conv_bwd_norm{i}")
            da, db, ddw, absums = _conv_bwd_taps(du1, sv["proj"], conv_dw_f[j], f"conv_bwd_taps{i}")
            dproj = jnp.concatenate([da, db, dgate], axis=1)
            dhn = _matmul(dproj, sv["w_in"], "nt", F32, f"d_conv_in{i}")
            dw_in, got_in = unpack(_matmul(sv["hn_t"], dproj, "nn", F32, f"dw_conv_in{i}", carry=carry(0)))
            dw_in = shard_cols(dw_in, cin_shard)
            nsum = jnp.sum(nsums, axis=1)
            absum = jnp.sum(absums, axis=1)
            d_conv_small[j] = dict(b_in=jnp.concatenate([absum[0], absum[1], nsum[3]]),
                                   dw=jnp.sum(ddw, axis=1), dw_b=nsum[2], ln_g=nsum[0], ln_b=nsum[1])
        if pending is not None:
            summed_in[i + 1], summed_out[i + 1] = got_in, got_out
        pending = (dw_in.astype(BF16), dw_out.reshape(N_DEV, -1, D).astype(BF16))
        dh, dh16, dg = _rms_bwd(sv["h"], norm_g[i:i + 1], dhn, dh, f"rms_bwd{i}")
        d_norm_g[i] = jnp.sum(dg, axis=0)
    summed_in[0] = _exchange(pending[0], "rs_w_in0", False)
    summed_out[0] = _exchange(pending[1], "rs_w_out0", False)
    grad_x = dh[None]

    def small_for(d):
        sl = lambda a, n: a[..., d * n:(d + 1) * n]
        return _flat_rows([
            jnp.stack([sl(s["b_in"], 3 * C // N_DEV) for s in d_conv_small]),
            jnp.stack([sl(s["dw"], C // N_DEV) for s in d_conv_small]),
            jnp.stack([sl(s["dw_b"], C // N_DEV) for s in d_conv_small]),
            jnp.stack([sl(s["ln_g"], C // N_DEV) for s in d_conv_small]),
            jnp.stack([sl(s["ln_b"], C // N_DEV) for s in d_conv_small])])
    r_small = _exchange(jnp.stack([small_for(d) for d in range(N_DEV)]), "rs_small", False)

    rep_shapes = [norm_g.shape, fox_b_f.shape, final_norm_g.shape, (1,)]
    rep_part = _flat_rows([jnp.stack(d_norm_g), jnp.stack(d_fox_b_f), jnp.sum(dg_final, axis=0),
                           jnp.sum(loss_part[:, 0])[None]])
    r_rep = _exchange(rep_part, "ag_replicated", True)

    def update(parts, w, m, v, name):
        two_d = (-1, w.shape[-1])
        stacked = jnp.stack(parts, axis=1).reshape((N_DEV,) + w.reshape(two_d).shape)
        res = _adamw(stacked, w.reshape(two_d), m.reshape(two_d), v.reshape(two_d), name)
        return [r.reshape(w.shape) for r in res]

    fox_layers = [i for i in range(depth) if is_fox(i)]
    conv_layers = [i for i in range(depth) if not is_fox(i)]
    u_fin = update([summed_in[i] for i in fox_layers], fox_w_in, m_fox_w_in, v_fox_w_in, "adamw_fox_w_in")
    u_fout = update([summed_out[i] for i in fox_layers], fox_w_out, m_fox_w_out, v_fox_w_out, "adamw_fox_w_out")
    u_cin = update([summed_in[i] for i in conv_layers], conv_w_in, m_conv_w_in, v_conv_w_in, "adamw_conv_w_in")
    u_cout = update([summed_out[i] for i in conv_layers], conv_w_out, m_conv_w_out, v_conv_w_out,
                    "adamw_conv_w_out")
    u_small = _adamw(r_small, small,
                     _flat_rows([m_conv_b_in, m_conv_dw, m_conv_dw_b, m_conv_ln_g, m_conv_ln_b]),
                     _flat_rows([v_conv_b_in, v_conv_dw, v_conv_dw_b, v_conv_ln_g, v_conv_ln_b]), "adamw_small")
    zero1 = jnp.zeros((1,), F32)
    u_rep = _adamw(r_rep, _flat_rows([norm_g, fox_b_f, final_norm_g, zero1]),
                   _flat_rows([m_norm_g, m_fox_b_f, m_final_norm_g, zero1]),
                   _flat_rows([v_norm_g, v_fox_b_f, v_final_norm_g, zero1]), "adamw_replicated")

    outs = []
    loss = None
    for kind in range(4):
        b_in_k, dw_k, dwb_k, lng_k, lnb_k = _unflat(u_small[kind], small_shapes)
        ng_k, bf_k, fg_k, loss_k = _unflat(u_rep[kind], rep_shapes)
        if kind == 0:
            loss = loss_k[0]
        outs += [ng_k, u_fin[kind], bf_k, u_fout[kind], u_cin[kind], b_in_k, dw_k, dwb_k, lng_k, lnb_k,
                 u_cout[kind], fg_k]
    return (loss, grad_x, *outs)
```

```python
import functools

import numpy as np
import jax
import jax.numpy as jnp
from jax import lax
from jax.experimental import pallas as pl
from jax.experimental.pallas import tpu as pltpu

F32 = jnp.float32
BF16 = jnp.bfloat16
MESH_ID = pl.DeviceIdType.MESH

N_DEV = 8
RMS_EPS = 1e-6
LN_EPS = 1e-5
ADAM_LR = 0.001
ADAM_B1 = 0.9
ADAM_B2 = 0.999
ADAM_EPS = 1e-08
ADAM_WD = 0.01
ADAM_STEP = 10

LANES = 128
SUBLANES = 8
VMEM_LIMIT = 56 * 1024 * 1024
NEG_BIG = -1e30
CONV_HALO = 32
FLASH_TILE = 512
CONV_ROWS = 128
CONV_CHUNK = 32
FGATE_ROWS = 256
ROW_TILE = 256


def _params(*sem):
    return pltpu.CompilerParams(dimension_semantics=sem if sem else None, vmem_limit_bytes=VMEM_LIMIT)


def _tile(n, pref):
    if n <= pref:
        return n
    t = pref
    while n % t:
        t //= 2
    return t


def _sigmoid(x):
    return 1.0 / (1.0 + jnp.exp(-x))


def _dsilu(x, s):
    return s * (1.0 + x * (1.0 - s))


def _rows(n, chunk, fn):
    def step(i, carry):
        fn(pl.multiple_of(i * chunk, chunk))
        return carry
    lax.fori_loop(0, n // chunk, step, 0)


def _peer(k):
    x, y, c = lax.axis_index("x"), lax.axis_index("y"), lax.axis_index("c")
    px = 1 - x if (k >> 2) & 1 else x
    py = 1 - y if (k >> 1) & 1 else y
    pc = 1 - c if k & 1 else c
    return (px, py, pc), 4 * px + 2 * py + pc


def _exchange_copies(x_ref, o_ref, send_sems, recv_sems, local_sem, gather):
    _, me = _peer(0)
    mine = x_ref if gather else x_ref.at[me]
    local = pltpu.make_async_copy(mine, o_ref.at[me], local_sem)
    sends, arrivals = [], []
    for k in range(1, N_DEV):
        peer, pidx = _peer(k)
        sems = dict(send_sem=send_sems.at[k - 1], recv_sem=recv_sems.at[k - 1], device_id=peer,
                    device_id_type=MESH_ID)
        sends.append(pltpu.make_async_remote_copy(src_ref=x_ref if gather else x_ref.at[pidx],
                                                  dst_ref=o_ref.at[me], **sems))
        arrivals.append(pltpu.make_async_remote_copy(src_ref=mine, dst_ref=o_ref.at[pidx], **sems))
    return local, sends, arrivals


def _exchange_start(*refs, gather):
    local, sends, _ = _exchange_copies(*refs, gather)
    local.start()
    for cp in sends:
        cp.start()


def _exchange_wait(*refs, gather):
    local, sends, arrivals = _exchange_copies(*refs, gather)
    for cp in arrivals:
        cp.wait_recv()
    for cp in sends:
        cp.wait_send()
    local.wait()


EXCHANGE_SCRATCH = [pltpu.SemaphoreType.DMA((N_DEV - 1,)), pltpu.SemaphoreType.DMA((N_DEV - 1,)),
                    pltpu.SemaphoreType.DMA]


def _exchange_shape(x, gather):
    return jax.ShapeDtypeStruct((N_DEV,) + x.shape if gather else x.shape, x.dtype)


def _exchange(x, name, gather):
    def body(*refs):
        _exchange_start(*refs, gather=gather)
        _exchange_wait(*refs, gather=gather)

    return pl.pallas_call(
        body, name=name,
        out_shape=_exchange_shape(x, gather),
        in_specs=[pl.BlockSpec(memory_space=pl.ANY)],
        out_specs=pl.BlockSpec(memory_space=pl.ANY),
        scratch_shapes=list(EXCHANGE_SCRATCH),
    )(x)


def _matmul(a, b, mode, out_dtype, name, bias=None, add=None, b_col_off=0, n_out=None, carry=None):
    M, K = a.shape
    N = n_out if n_out is not None else (b.shape[0] if mode == "nt" else b.shape[1])
    tm = _tile(M, 512)
    tn = _tile(N, 1024)
    k_cap = 4096 if a.dtype.itemsize == 2 and b.dtype.itemsize == 2 else 2048
    tk = next(K // d for d in range(1, K + 1) if K % d == 0 and K // d <= k_cap and (K // d) % LANES == 0)
    nm, nn, nk = M // tm, N // tn, K // tk
    assert b_col_off % tn == 0
    joff = b_col_off // tn
    dims = {"nn": (((1,), (0,)), ((), ())), "nt": (((1,), (1,)), ((), ()))}[mode]
    n_in = 2 + (bias is not None) + (add is not None)

    def body(*refs):
        a_ref, b_ref = refs[0], refs[1]
        bias_ref = refs[2] if bias is not None else None
        add_ref = refs[n_in - 1] if add is not None else None
        pos = n_in
        x_ref = o_ref = x_out_ref = None
        if carry is not None:
            x_ref, o_ref, x_out_ref = refs[pos], refs[pos + 1], refs[pos + 2]
            pos += 3
        else:
            o_ref = refs[pos]
            pos += 1
        acc_ref = None
        if nk > 1:
            acc_ref = refs[pos]
            pos += 1
        exchange_refs = (x_ref, x_out_ref) + tuple(refs[pos:])
        i, j, kk = pl.program_id(0), pl.program_id(1), pl.program_id(2)

        if carry is not None:
            @pl.when((i == 0) & (j == 0) & (kk == 0))
            def _():
                _exchange_start(*exchange_refs, gather=carry[1])

        part = lax.dot_general(a_ref[...].astype(BF16), b_ref[...].astype(BF16), dims,
                               preferred_element_type=F32)

        def finish(r):
            if bias_ref is not None:
                r = r + bias_ref[...]
            if add_ref is not None:
                r = r + add_ref[...]
            o_ref[...] = r.astype(o_ref.dtype)

        if nk == 1:
            finish(part)
        else:
            @pl.when(kk == 0)
            def _():
                acc_ref[...] = part

            @pl.when(kk > 0)
            def _():
                acc_ref[...] += part

            @pl.when(kk == nk - 1)
            def _():
                finish(acc_ref[...])

        if carry is not None:
            @pl.when((i == nm - 1) & (j == nn - 1) & (kk == nk - 1))
            def _():
                _exchange_wait(*exchange_refs, gather=carry[1])

    a_spec = pl.BlockSpec((tm, tk), lambda i, j, k: (i, k))
    if mode == "nt":
        b_spec = pl.BlockSpec((tn, tk), lambda i, j, k: (j, k))
    else:
        b_spec = pl.BlockSpec((tk, tn), lambda i, j, k: (k, j + joff))
    in_specs = [a_spec, b_spec]
    args = [a, b]
    if bias is not None:
        in_specs.append(pl.BlockSpec((1, tn), lambda i, j, k: (0, j)))
        args.append(bias)
    if add is not None:
        in_specs.append(pl.BlockSpec((tm, tn), lambda i, j, k: (i, j)))
        args.append(add)
    out_specs = [pl.BlockSpec((tm, tn), lambda i, j, k: (i, j))]
    out_shape = [jax.ShapeDtypeStruct((M, N), out_dtype)]
    scratch = [pltpu.VMEM((tm, tn), F32)] if nk > 1 else []
    if carry is not None:
        in_specs.append(pl.BlockSpec(memory_space=pl.ANY))
        args.append(carry[0])
        out_specs.append(pl.BlockSpec(memory_space=pl.ANY))
        out_shape.append(_exchange_shape(*carry))
        scratch += EXCHANGE_SCRATCH
    res = pl.pallas_call(
        body, name=name,
        grid=(nm, nn, nk),
        in_specs=in_specs, out_specs=out_specs, out_shape=out_shape, scratch_shapes=scratch,
        compiler_params=_params(*(("arbitrary",) * 3 if carry is not None else ("parallel", "parallel", "arbitrary"))),
    )(*args)
    return res if carry is not None else res[0]


def _rms_fwd(h, g, name):
    S, D = h.shape
    ts = _tile(S, 512)

    def body(h_ref, g_ref, o_ref, ot_ref):
        x = h_ref[...]
        r = lax.rsqrt(jnp.mean(x * x, axis=-1, keepdims=True) + RMS_EPS)
        y = x * r * g_ref[...]
        o_ref[...] = y.astype(BF16)
        ot_ref[...] = y.T.astype(BF16)

    return pl.pallas_call(
        body, name=name, grid=(S // ts,),
        in_specs=[pl.BlockSpec((ts, D), lambda i: (i, 0)), pl.BlockSpec((1, D), lambda i: (0, 0))],
        out_specs=[pl.BlockSpec((ts, D), lambda i: (i, 0)), pl.BlockSpec((D, ts), lambda i: (0, i))],
        out_shape=[jax.ShapeDtypeStruct((S, D), BF16), jax.ShapeDtypeStruct((D, S), BF16)],
        compiler_params=_params("parallel"),
    )(h, g)


def _rms_bwd_block(x, g, dy):
    r = lax.rsqrt(jnp.mean(x * x, axis=-1, keepdims=True) + RMS_EPS)
    xr = x * r
    t = dy * g
    dx = r * (t - xr * jnp.mean(t * xr, axis=-1, keepdims=True))
    return dx, dy * xr


def _rms_bwd(h, g, dhn, dh, name):
    S, D = h.shape
    ts = _tile(S, ROW_TILE)

    def body(h_ref, g_ref, dhn_ref, dh_ref, o_ref, o16_ref, o16t_ref, dg_ref):
        dx, dgt = _rms_bwd_block(h_ref[...], g_ref[...], dhn_ref[...])
        out = dh_ref[...] + dx
        o_ref[...] = out
        o16_ref[...] = out.astype(BF16)
        o16t_ref[...] = out.T.astype(BF16)
        part = jnp.sum(dgt.reshape(ts // SUBLANES, SUBLANES, D), axis=0)

        @pl.when(pl.program_id(0) == 0)
        def _():
            dg_ref[...] = part

        @pl.when(pl.program_id(0) > 0)
        def _():
            dg_ref[...] += part

    row = pl.BlockSpec((ts, D), lambda i: (i, 0))
    return pl.pallas_call(
        body, name=name, grid=(S // ts,),
        in_specs=[row, pl.BlockSpec((1, D), lambda i: (0, 0)), row, row],
        out_specs=[row, row, pl.BlockSpec((D, ts), lambda i: (0, i)),
                   pl.BlockSpec((SUBLANES, D), lambda i: (0, 0))],
        out_shape=[jax.ShapeDtypeStruct((S, D), F32), jax.ShapeDtypeStruct((S, D), BF16),
                   jax.ShapeDtypeStruct((D, S), BF16), jax.ShapeDtypeStruct((SUBLANES, D), F32)],
        compiler_params=_params("arbitrary"),
    )(h, g, dhn, dh)


def _loss_head(h, g, target, name):
    S, D = h.shape
    ts = _tile(S, ROW_TILE)

    def body(h_ref, g_ref, t_ref, o_ref, o16_ref, o16t_ref, loss_ref, dg_ref):
        x = h_ref[...]
        gg = g_ref[...]
        r = lax.rsqrt(jnp.mean(x * x, axis=-1, keepdims=True) + RMS_EPS)
        err = x * r * gg - t_ref[...]
        row_loss = 0.5 * jnp.mean(err * err, axis=-1, keepdims=True)
        dx, dgt = _rms_bwd_block(x, gg, err * (1.0 / D))
        o_ref[...] = dx
        o16_ref[...] = dx.astype(BF16)
        o16t_ref[...] = dx.T.astype(BF16)
        part = jnp.sum(dgt.reshape(ts // SUBLANES, SUBLANES, D), axis=0)
        lpart = jnp.sum(jnp.broadcast_to(row_loss, (ts, LANES)).reshape(ts // SUBLANES, SUBLANES, LANES), axis=0)

        @pl.when(pl.program_id(0) == 0)
        def _():
            dg_ref[...] = part
            loss_ref[...] = lpart

        @pl.when(pl.program_id(0) > 0)
        def _():
            dg_ref[...] += part
            loss_ref[...] += lpart

    row = pl.BlockSpec((ts, D), lambda i: (i, 0))
    return pl.pallas_call(
        body, name=name, grid=(S // ts,),
        in_specs=[row, pl.BlockSpec((1, D), lambda i: (0, 0)), row],
        out_specs=[row, row, pl.BlockSpec((D, ts), lambda i: (0, i)),
                   pl.BlockSpec((SUBLANES, LANES), lambda i: (0, 0)), pl.BlockSpec((SUBLANES, D), lambda i: (0, 0))],
        out_shape=[jax.ShapeDtypeStruct((S, D), F32), jax.ShapeDtypeStruct((S, D), BF16),
                   jax.ShapeDtypeStruct((D, S), BF16), jax.ShapeDtypeStruct((SUBLANES, LANES), F32),
                   jax.ShapeDtypeStruct((SUBLANES, D), F32)],
        compiler_params=_params("arbitrary"),
    )(h, g, target)


def _split3(x):
    hi = x.astype(BF16)
    r1 = x - hi.astype(F32)
    mid = r1.astype(BF16)
    lo = (r1 - mid.astype(F32)).astype(BF16)
    return hi, mid, lo


def _tri_sum(tri, x):
    hi, mid, lo = _split3(x)
    dot = functools.partial(jnp.dot, preferred_element_type=F32)
    return dot(tri, hi) + dot(tri, mid) + dot(tri, lo)


def _fgate_fwd(f, b_f, name):
    S = f.shape[0]
    tb = _tile(S, FGATE_ROWS)

    def body(f_ref, b_ref, c_ref, carry_ref):
        @pl.when(pl.program_id(0) == 0)
        def _():
            carry_ref[...] = jnp.zeros_like(carry_ref)

        x = f_ref[...] + b_ref[...]
        lf = jnp.minimum(x, 0.0) - jnp.log1p(jnp.exp(-jnp.abs(x)))
        r = lax.broadcasted_iota(jnp.int32, (tb, tb), 0)
        c = lax.broadcasted_iota(jnp.int32, (tb, tb), 1)
        tri = (c <= r).astype(BF16)
        c_ref[...] = _tri_sum(tri, lf) + carry_ref[0:1, :]
        carry_ref[...] += _tri_sum(jnp.ones((SUBLANES, tb), BF16), lf)

    return pl.pallas_call(
        body, name=name, grid=(S // tb,),
        in_specs=[pl.BlockSpec((tb, LANES), lambda i: (i, 0)), pl.BlockSpec((1, LANES), lambda i: (0, 0))],
        out_specs=pl.BlockSpec((tb, LANES), lambda i: (i, 0)),
        out_shape=jax.ShapeDtypeStruct((S, LANES), F32),
        scratch_shapes=[pltpu.VMEM((SUBLANES, LANES), F32)],
        compiler_params=_params("arbitrary"),
    )(f, b_f)


def _fgate_bwd(dc, f, b_f, name):
    S = f.shape[0]
    tb = _tile(S, FGATE_ROWS)
    nb = S // tb

    def body(dc_ref, f_ref, b_ref, df_ref, db_ref, carry_ref):
        @pl.when(pl.program_id(0) == 0)
        def _():
            carry_ref[...] = jnp.zeros_like(carry_ref)
            db_ref[...] = jnp.zeros_like(db_ref)

        d = dc_ref[...]
        r = lax.broadcasted_iota(jnp.int32, (tb, tb), 0)
        c = lax.broadcasted_iota(jnp.int32, (tb, tb), 1)
        tri = (c >= r).astype(BF16)
        dlf = _tri_sum(tri, d) + carry_ref[0:1, :]
        carry_ref[...] += _tri_sum(jnp.ones((SUBLANES, tb), BF16), d)
        df = dlf * _sigmoid(-(f_ref[...] + b_ref[...]))
        df_ref[...] = df
        db_ref[...] += jnp.sum(df.reshape(tb // SUBLANES, SUBLANES, LANES), axis=0)

    rev = pl.BlockSpec((tb, LANES), lambda i: (nb - 1 - i, 0))
    return pl.pallas_call(
        body, name=name, grid=(nb,),
        in_specs=[rev, rev, pl.BlockSpec((1, LANES), lambda i: (0, 0))],
        out_specs=[rev, pl.BlockSpec((SUBLANES, LANES), lambda i: (0, 0))],
        out_shape=[jax.ShapeDtypeStruct((S, LANES), F32), jax.ShapeDtypeStruct((SUBLANES, LANES), F32)],
        scratch_shapes=[pltpu.VMEM((SUBLANES, LANES), F32)],
        compiler_params=_params("arbitrary"),
    )(dc, f, b_f)


LOG2E = 1.4426950408889634
NT_DIMS = (((1,), (1,)), ((), ()))
TN_DIMS = (((0,), (0,)), ((), ()))


def _causal_mask(t):
    r = lax.broadcasted_iota(jnp.int32, (t, t), 0)
    c = lax.broadcasted_iota(jnp.int32, (t, t), 1)
    return r >= c


def _key_tiles(qi, tile, group):
    def several(i, carry):
        for u in range(group):
            tile(group * i + u, False)
        return carry
    lax.fori_loop(0, qi // group, several, 0)

    def single(j, carry):
        tile(j, False)
        return carry
    lax.fori_loop((qi // group) * group, qi, single, 0)
    tile(qi, True)


def _flash_fwd(qkv, gate, cneg, heads, name):
    S, W3 = qkv.shape
    W = W3 // 3
    dh = W // heads
    assert dh == LANES
    tq = _tile(S, FLASH_TILE)
    nq = S // tq
    c1 = dh ** -0.5 * LOG2E

    def body(q_ref, k_ref, v_ref, b_ref, g_ref, y_ref, o_ref, lse_ref, m_scr, l_scr, acc_scr):
        qi = pl.program_id(1)
        m_scr[...] = jnp.full_like(m_scr, NEG_BIG)
        l_scr[...] = jnp.zeros_like(l_scr)
        acc_scr[...] = jnp.zeros_like(acc_scr)
        q = q_ref[...]

        def tile(j, diagonal):
            rows = pl.ds(pl.multiple_of(j * tq, tq), tq)
            t = lax.dot_general(q, k_ref[rows, :], NT_DIMS, preferred_element_type=F32) * c1 + b_ref[0, j] * LOG2E
            if diagonal:
                t = jnp.where(_causal_mask(tq), t, NEG_BIG)
            m_prev = m_scr[...]
            m_next = jnp.maximum(m_prev, jnp.max(t, axis=1, keepdims=True))
            alpha = jnp.exp2(m_prev - m_next)
            p = jnp.exp2(t - m_next[:, :1])
            l_scr[...] = alpha * l_scr[...] + jnp.sum(p, axis=1, keepdims=True)
            acc_scr[...] = alpha * acc_scr[...] + jnp.dot(p.astype(BF16), v_ref[rows, :],
                                                          preferred_element_type=F32)
            m_scr[...] = m_next

        _key_tiles(qi, tile, 4)
        l = l_scr[...]
        o = acc_scr[...] / l
        g = g_ref[...]
        o_ref[...] = o
        lse_ref[...] = m_scr[...] + jnp.log2(l)
        y_ref[...] = (o * (g * _sigmoid(g))).astype(BF16)

    qblk = pl.BlockSpec((tq, dh), lambda h, i: (i, h))
    return pl.pallas_call(
        body, name=name, grid=(heads, nq),
        in_specs=[qblk,
                  pl.BlockSpec((S, dh), lambda h, i: (0, heads + h)),
                  pl.BlockSpec((S, dh), lambda h, i: (0, 2 * heads + h)),
                  pl.BlockSpec((1, nq, 1, tq), lambda h, i: (h, 0, 0, 0)),
                  qblk],
        out_specs=[qblk, qblk, qblk],
        out_shape=[jax.ShapeDtypeStruct((S, W), BF16), jax.ShapeDtypeStruct((S, W), F32),
                   jax.ShapeDtypeStruct((S, W), F32)],
        scratch_shapes=[pltpu.VMEM((tq, LANES), F32), pltpu.VMEM((tq, LANES), F32), pltpu.VMEM((tq, dh), F32)],
        compiler_params=_params("parallel", "arbitrary"),
    )(qkv, qkv, qkv, cneg, gate)


def _flash_bwd(qkv, do, delta, lse, cneg, heads, name):
    S, W3 = qkv.shape
    W = W3 // 3
    dh = W // heads
    tq = _tile(S, FLASH_TILE)
    nq = S // tq
    scale = dh ** -0.5
    c1 = scale * LOG2E

    def body(q_ref, k_ref, v_ref, b_ref, do_ref, delta_ref, lse_ref,
             dq_ref, dk_ref, dv_ref, db_ref, rs_ref, dq_scr, dk_scr, dv_scr):
        qi = pl.program_id(1)

        @pl.when(qi == 0)
        def _():
            dk_scr[...] = jnp.zeros_like(dk_scr)
            dv_scr[...] = jnp.zeros_like(dv_scr)
            db_ref[...] = jnp.zeros_like(db_ref)

        dq_scr[...] = jnp.zeros_like(dq_scr)
        rs_ref[...] = jnp.zeros_like(rs_ref)
        q, d_o = q_ref[...], do_ref[...]
        lse = lse_ref[...][:, :1]
        delta = delta_ref[...][:, :1]

        def tile(j, diagonal):
            rows = pl.ds(pl.multiple_of(j * tq, tq), tq)
            k, v = k_ref[rows, :], v_ref[rows, :]
            t = lax.dot_general(q, k, NT_DIMS, preferred_element_type=F32) * c1 + b_ref[0, j] * LOG2E
            p = jnp.exp2(t - lse)
            if diagonal:
                p = jnp.where(_causal_mask(tq), p, 0.0)
            dp = lax.dot_general(d_o, v, NT_DIMS, preferred_element_type=F32)
            ds = p * (dp - delta)
            dv_scr[rows, :] += lax.dot_general(p.astype(BF16), d_o, TN_DIMS, preferred_element_type=F32)
            db_ref[0, j] += jnp.sum(ds.reshape(tq // SUBLANES, SUBLANES, tq), axis=0)
            dsb = (ds * scale).astype(BF16)
            dk_scr[rows, :] += lax.dot_general(dsb, q, TN_DIMS, preferred_element_type=F32)
            dq_scr[...] += jnp.dot(dsb, k, preferred_element_type=F32)
            rs_ref[...] += jnp.sum(ds, axis=1, keepdims=True)

        _key_tiles(qi, tile, 2)
        dq_ref[...] = dq_scr[...].astype(BF16)

        @pl.when(qi == nq - 1)
        def _():
            dk_ref[...] = dk_scr[...].astype(BF16)
            dv_ref[...] = dv_scr[...].astype(BF16)

    qblk = pl.BlockSpec((tq, dh), lambda h, i: (i, h))
    head = pl.BlockSpec((S, dh), lambda h, i: (0, h))
    return pl.pallas_call(
        body, name=name, grid=(heads, nq),
        in_specs=[qblk,
                  pl.BlockSpec((S, dh), lambda h, i: (0, heads + h)),
                  pl.BlockSpec((S, dh), lambda h, i: (0, 2 * heads + h)),
                  pl.BlockSpec((1, nq, 1, tq), lambda h, i: (h, 0, 0, 0)),
                  qblk, qblk, qblk],
        out_specs=[qblk, head, head,
                   pl.BlockSpec((1, nq, SUBLANES, tq), lambda h, i: (h, 0, 0, 0)),
                   qblk],
        out_shape=[jax.ShapeDtypeStruct((S, W), BF16), jax.ShapeDtypeStruct((S, W), BF16),
                   jax.ShapeDtypeStruct((S, W), BF16), jax.ShapeDtypeStruct((heads, nq, SUBLANES, tq), F32),
                   jax.ShapeDtypeStruct((S, W), F32)],
        scratch_shapes=[pltpu.VMEM((tq, dh), F32), pltpu.VMEM((S, dh), F32), pltpu.VMEM((S, dh), F32)],
        compiler_params=_params("parallel", "arbitrary"),
    )(qkv, qkv, qkv, cneg, do, delta, lse)


def _fox_gate_bwd(dy, o, gate, name):
    S, W = dy.shape
    ts = _tile(S, ROW_TILE)

    def body(dy_ref, o_ref, g_ref, do_ref, dg_ref, delta_ref):
        d, g, o_val = dy_ref[...], g_ref[...], o_ref[...]
        sg = _sigmoid(g)
        d_o = (d * (g * sg)).astype(BF16)
        do_ref[...] = d_o
        dg_ref[...] = (d * o_val * _dsilu(g, sg)).astype(BF16)
        prod = d_o.astype(F32) * o_val
        for h in range(W // LANES):
            cols = slice(h * LANES, (h + 1) * LANES)
            delta_ref[:, cols] = jnp.broadcast_to(jnp.sum(prod[:, cols], axis=1, keepdims=True), (ts, LANES))

    row = pl.BlockSpec((ts, W), lambda i: (i, 0))
    return pl.pallas_call(
        body, name=name, grid=(S // ts,),
        in_specs=[row, row, row], out_specs=[row, row, row],
        out_shape=[jax.ShapeDtypeStruct((S, W), BF16), jax.ShapeDtypeStruct((S, W), BF16),
                   jax.ShapeDtypeStruct((S, W), F32)],
        compiler_params=_params("parallel"),
    )(dy, o, gate)


def _shifted_copies(buf_ref, sh_ref, rows):
    for j in range(1, SUBLANES):
        sh_ref[j, 0:rows, :] = buf_ref[j:j + rows, :]


def _tap(buf_ref, sh_ref, r0, off, cols):
    j, base = off % SUBLANES, off - off % SUBLANES
    if j == 0:
        return buf_ref[pl.ds(r0 + base, SUBLANES), cols]
    return sh_ref[j, pl.ds(r0 + base, SUBLANES), cols]


def _tap_weights(dw_ref, cols):
    return [jnp.broadcast_to(dw_ref[k:k + 1, cols], (SUBLANES, LANES)) for k in range(dw_ref.shape[0])]


def _conv_fwd(proj, dw, dw_b, ln_g, ln_b, name):
    S, C3 = proj.shape
    C = C3 // 3
    K = dw.shape[0]
    assert K - 1 <= CONV_HALO - 2
    ts = _tile(S, CONV_ROWS)
    hb = ts // CONV_HALO
    nrows = ts + CONV_HALO
    lead = CONV_HALO - (K - 1)

    def body(a_ref, b_ref, ah_ref, bh_ref, g_ref, dw_ref, dwb_ref, lg_ref, lb_ref, y_ref, u1_ref, buf, sh):
        first = pl.program_id(0) == 0
        buf[0:CONV_HALO, :] = jnp.where(first, 0.0, ah_ref[...] * _sigmoid(bh_ref[...]))

        def glu(r0):
            rows = pl.ds(r0, CONV_CHUNK)
            buf[pl.ds(r0 + CONV_HALO, CONV_CHUNK), :] = a_ref[rows, :] * _sigmoid(b_ref[rows, :])
        _rows(ts, CONV_CHUNK, glu)
        _shifted_copies(buf, sh, nrows - SUBLANES)

        for s in range(C // LANES):
            cols = slice(s * LANES, (s + 1) * LANES)
            w = _tap_weights(dw_ref, cols)
            bias = jnp.broadcast_to(dwb_ref[:, cols], (SUBLANES, LANES))

            def taps(r0, w=w, bias=bias, cols=cols):
                for u in range(CONV_CHUNK // SUBLANES):
                    r = r0 + u * SUBLANES
                    acc = bias
                    for k in range(K):
                        acc = acc + w[k] * _tap(buf, sh, r, lead + k, cols)
                    u1_ref[pl.ds(r, SUBLANES), cols] = acc
            _rows(ts, CONV_CHUNK, taps)

        def norm(r0):
            rows = pl.ds(r0, CONV_CHUNK)
            u1 = u1_ref[rows, :]
            mu = jnp.mean(u1, axis=-1, keepdims=True)
            xc = u1 - mu
            rstd = lax.rsqrt(jnp.mean(xc * xc, axis=-1, keepdims=True) + LN_EPS)
            z = xc * rstd * lg_ref[...] + lb_ref[...]
            g = g_ref[rows, :]
            y_ref[rows, :] = ((z * _sigmoid(z)) * (g * _sigmoid(g))).astype(BF16)
        _rows(ts, CONV_CHUNK, norm)

    row = lambda col: pl.BlockSpec((ts, C), lambda i: (i, col))
    halo = lambda col: pl.BlockSpec((CONV_HALO, C), lambda i: (jnp.maximum(i * hb - 1, 0), col))
    vec = pl.BlockSpec((1, C), lambda i: (0, 0))
    return pl.pallas_call(
        body, name=name, grid=(S // ts,),
        in_specs=[row(0), row(1), halo(0), halo(1), row(2), pl.BlockSpec((K, C), lambda i: (0, 0)), vec, vec, vec],
        out_specs=[row(0), row(0)],
        out_shape=[jax.ShapeDtypeStruct((S, C), BF16), jax.ShapeDtypeStruct((S, C), F32)],
        scratch_shapes=[pltpu.VMEM((nrows, C), F32), pltpu.VMEM((SUBLANES, nrows, C), F32)],
        compiler_params=_params("parallel"),
    )(proj, proj, proj, proj, proj, dw, dw_b, ln_g, ln_b)


def _conv_bwd_norm(dy, proj, u1, ln_g, ln_b, name):
    S, C = dy.shape
    ts = _tile(S, ROW_TILE)
    groups = CONV_CHUNK // SUBLANES

    def fold(x):
        return jnp.sum(x.reshape(groups, SUBLANES, C), axis=0)

    def body(dy_ref, g_ref, u1_ref, lg_ref, lb_ref, du1_ref, dg_ref, sums_ref):
        @pl.when(pl.program_id(0) == 0)
        def _():
            sums_ref[...] = jnp.zeros_like(sums_ref)

        def chunk(r0):
            rows = pl.ds(r0, CONV_CHUNK)
            d, g, u1 = dy_ref[rows, :], g_ref[rows, :], u1_ref[rows, :]
            mu = jnp.mean(u1, axis=-1, keepdims=True)
            xc = u1 - mu
            rstd = lax.rsqrt(jnp.mean(xc * xc, axis=-1, keepdims=True) + LN_EPS)
            xh = xc * rstd
            z = xh * lg_ref[...] + lb_ref[...]
            sz, sg = _sigmoid(z), _sigmoid(g)
            dgate = d * (z * sz) * _dsilu(g, sg)
            dz = d * (g * sg) * _dsilu(z, sz)
            dxh = dz * lg_ref[...]
            du1 = rstd * (dxh - jnp.mean(dxh, axis=-1, keepdims=True)
                          - xh * jnp.mean(dxh * xh, axis=-1, keepdims=True))
            du1_ref[rows, :] = du1
            dg_ref[rows, :] = dgate.astype(BF16)
            sums_ref[0] += fold(dz * xh)
            sums_ref[1] += fold(dz)
            sums_ref[2] += fold(du1)
            sums_ref[3] += fold(dgate)
        _rows(ts, CONV_CHUNK, chunk)

    row = pl.BlockSpec((ts, C), lambda i: (i, 0))
    vec = pl.BlockSpec((1, C), lambda i: (0, 0))
    return pl.pallas_call(
        body, name=name, grid=(S // ts,),
        in_specs=[row, pl.BlockSpec((ts, C), lambda i: (i, 2)), row, vec, vec],
        out_specs=[row, row, pl.BlockSpec((4, SUBLANES, C), lambda i: (0, 0, 0))],
        out_shape=[jax.ShapeDtypeStruct((S, C), F32), jax.ShapeDtypeStruct((S, C), BF16),
                   jax.ShapeDtypeStruct((4, SUBLANES, C), F32)],
        compiler_params=_params("arbitrary"),
    )(dy, proj, u1, ln_g, ln_b)


def _conv_bwd_taps(du1, proj, dw, name):
    S, C = du1.shape
    K = dw.shape[0]
    ts = _tile(S, CONV_ROWS)
    hb = ts // CONV_HALO
    nblk = S // ts
    last_halo = S // CONV_HALO - 1
    nrows = ts + CONV_HALO
    lead = CONV_HALO - (K - 1)

    def body(d_ref, dn_ref, a_ref, b_ref, ah_ref, bh_ref, dw_ref, da_ref, db_ref, ddw_ref, sums_ref,
             ubuf, ush, dbuf, dsh):
        i = pl.program_id(0)

        @pl.when(i == 0)
        def _():
            ddw_ref[...] = jnp.zeros_like(ddw_ref)
            sums_ref[...] = jnp.zeros_like(sums_ref)

        ubuf[0:CONV_HALO, :] = jnp.where(i == 0, 0.0, ah_ref[...] * _sigmoid(bh_ref[...]))
        dbuf[ts:nrows, :] = jnp.where(i == nblk - 1, 0.0, dn_ref[...])

        def fill(r0):
            rows = pl.ds(r0, CONV_CHUNK)
            ubuf[pl.ds(r0 + CONV_HALO, CONV_CHUNK), :] = a_ref[rows, :] * _sigmoid(b_ref[rows, :])
            dbuf[rows, :] = d_ref[rows, :]
        _rows(ts, CONV_CHUNK, fill)
        _shifted_copies(ubuf, ush, nrows - SUBLANES)
        _shifted_copies(dbuf, dsh, nrows - SUBLANES)

        zero = jnp.zeros((SUBLANES, LANES), F32)
        for s in range(C // LANES):
            cols = slice(s * LANES, (s + 1) * LANES)
            w = _tap_weights(dw_ref, cols)

            def glu_bwd(j, sums, w=w, cols=cols):
                r0 = pl.multiple_of(j * CONV_CHUNK, CONV_CHUNK)
                parts = []
                for u in range(CONV_CHUNK // SUBLANES):
                    acc = zero
                    for k in range(K):
                        acc = acc + w[k] * _tap(dbuf, dsh, r0 + u * SUBLANES, K - 1 - k, cols)
                    parts.append(acc)
                du0 = jnp.concatenate(parts, axis=0)
                rows = pl.ds(r0, CONV_CHUNK)
                a, sb = a_ref[rows, cols], _sigmoid(b_ref[rows, cols])
                da = du0 * sb
                db = da * a * (1.0 - sb)
                da_ref[rows, cols] = da.astype(BF16)
                db_ref[rows, cols] = db.astype(BF16)
                fold = lambda x: jnp.sum(x.reshape(CONV_CHUNK // SUBLANES, SUBLANES, LANES), axis=0)
                return sums[0] + fold(da), sums[1] + fold(db)
            sa, sb_sum = lax.fori_loop(0, ts // CONV_CHUNK, glu_bwd, (zero, zero))
            sums_ref[0, :, cols] += sa
            sums_ref[1, :, cols] += sb_sum

            def tap_grads(j, accs, cols=cols):
                r0 = pl.multiple_of(j * CONV_CHUNK, CONV_CHUNK)
                rs = [r0 + u * SUBLANES for u in range(CONV_CHUNK // SUBLANES)]
                ds = [d_ref[pl.ds(r, SUBLANES), cols] for r in rs]
                out = []
                for k in range(K):
                    prods = [d * _tap(ubuf, ush, r, lead + k, cols) for d, r in zip(ds, rs)]
                    out.append(accs[k] + ((prods[0] + prods[1]) + (prods[2] + prods[3])))
                return tuple(out)
            assert CONV_CHUNK // SUBLANES == 4
            accs = lax.fori_loop(0, ts // CONV_CHUNK, tap_grads, (zero,) * K)
            for k in range(K):
                ddw_ref[k, :, cols] += accs[k]

    row = lambda col: pl.BlockSpec((ts, C), lambda i: (i, col))
    prev = lambda col: pl.BlockSpec((CONV_HALO, C), lambda i: (jnp.maximum(i * hb - 1, 0), col))
    nxt = pl.BlockSpec((CONV_HALO, C), lambda i: (jnp.minimum((i + 1) * hb, last_halo), 0))
    return pl.pallas_call(
        body, name=name, grid=(nblk,),
        in_specs=[row(0), nxt, row(0), row(1), prev(0), prev(1), pl.BlockSpec((K, C), lambda i: (0, 0))],
        out_specs=[row(0), row(0), pl.BlockSpec((K, SUBLANES, C), lambda i: (0, 0, 0)),
                   pl.BlockSpec((2, SUBLANES, C), lambda i: (0, 0, 0))],
        out_shape=[jax.ShapeDtypeStruct((S, C), BF16), jax.ShapeDtypeStruct((S, C), BF16),
                   jax.ShapeDtypeStruct((K, SUBLANES, C), F32), jax.ShapeDtypeStruct((2, SUBLANES, C), F32)],
        scratch_shapes=[pltpu.VMEM((nrows, C), F32), pltpu.VMEM((SUBLANES, nrows, C), F32),
                        pltpu.VMEM((nrows, C), F32), pltpu.VMEM((SUBLANES, nrows, C), F32)],
        compiler_params=_params("arbitrary"),
    )(du1, du1, proj, proj, proj, proj, dw)


def _adamw(parts, w, m, v, name):
    R, C = w.shape
    tr = _tile(R, 256)
    c1 = 1.0 - ADAM_B1 ** ADAM_STEP
    c2 = 1.0 - ADAM_B2 ** ADAM_STEP

    def body(p_ref, w_ref, m_ref, v_ref, g_ref, d_ref, nm_ref, nv_ref):
        g = p_ref[0].astype(F32)
        for d in range(1, N_DEV):
            g = g + p_ref[d].astype(F32)
        nm = ADAM_B1 * m_ref[...] + (1.0 - ADAM_B1) * g
        nv = ADAM_B2 * v_ref[...] + (1.0 - ADAM_B2) * (g * g)
        g_ref[...] = g
        nm_ref[...] = nm
        nv_ref[...] = nv
        d_ref[...] = -ADAM_LR * ((nm / c1) / (jnp.sqrt(nv / c2) + ADAM_EPS) + ADAM_WD * w_ref[...])

    row = pl.BlockSpec((tr, C), lambda i: (i, 0))
    out = jax.ShapeDtypeStruct((R, C), F32)
    return pl.pallas_call(
        body, name=name, grid=(R // tr,),
        in_specs=[pl.BlockSpec((N_DEV, tr, C), lambda i: (0, i, 0)), row, row, row],
        out_specs=[row, row, row, row], out_shape=[out, out, out, out],
        compiler_params=_params("parallel"),
    )(parts, w, m, v)


def _pad_lanes(a, width=LANES):
    return jnp.pad(a, ((0, 0), (0, width - a.shape[1])))


def _flat_rows(parts, width=LANES):
    flat = jnp.concatenate([p.reshape(-1) for p in parts])
    rows = -(-flat.shape[0] // width)
    rows = -(-rows // SUBLANES) * SUBLANES
    return jnp.pad(flat, (0, rows * width - flat.shape[0])).reshape(rows, width)


def _unflat(rows2d, shapes):
    flat = rows2d.reshape(-1)
    out, pos = [], 0
    for s in shapes:
        n = int(np.prod(s))
        out.append(flat[pos:pos + n].reshape(s))
        pos += n
    return out


def kernel(x, norm_g, fox_w_in, fox_b_f, fox_w_out, conv_w_in, conv_b_in, conv_dw, conv_dw_b, conv_ln_g, conv_ln_b, conv_w_out, final_norm_g, loss_target, m_norm_g, m_fox_w_in, m_fox_b_f, m_fox_w_out, m_conv_w_in, m_conv_b_in, m_conv_dw, m_conv_dw_b, m_conv_ln_g, m_conv_ln_b, m_conv_w_out, m_final_norm_g, v_norm_g, v_fox_w_in, v_fox_b_f, v_fox_w_out, v_conv_w_in, v_conv_b_in, v_conv_dw, v_conv_dw_b, v_conv_ln_g, v_conv_ln_b, v_conv_w_out, v_final_norm_g):
    h0 = x[0]
    target = loss_target[0]
    S, D = h0.shape
    depth = norm_g.shape[0]
    n_fox, _, fin_shard = fox_w_in.shape
    n_conv, _, cin_shard = conv_w_in.shape
    heads = fox_b_f.shape[1]
    W = fox_w_out.shape[1] * N_DEV
    C = conv_w_out.shape[1] * N_DEV
    assert fin_shard * N_DEV == 4 * W + heads and cin_shard * N_DEV == 3 * C and heads <= LANES
    is_fox = lambda i: i % 2 == 0

    shards = {}
    for i in range(depth):
        j = i // 2
        w_in, w_out = (fox_w_in, fox_w_out) if is_fox(i) else (conv_w_in, conv_w_out)
        shards[i] = (w_in[j].astype(BF16), w_out[j].astype(BF16))
    split = (5 * D) // 8

    def carried(*args, carry=None, **kw):
        r = _matmul(*args, carry=carry, **kw)
        return r if carry is not None else (r, None)

    small_shapes = [conv_b_in.shape, conv_dw.shape, conv_dw_b.shape, conv_ln_g.shape, conv_ln_b.shape]
    small = _flat_rows([conv_b_in, conv_dw, conv_dw_b, conv_ln_g, conv_ln_b])
    small_all = _exchange(small, "ag_small", True)
    b_in_s, dw_s, dwb_s, lng_s, lnb_s = zip(*[_unflat(small_all[d], small_shapes) for d in range(N_DEV)])
    cat = lambda parts, axis: jnp.concatenate(parts, axis=axis)
    conv_b_in_f = cat(b_in_s, 1)
    conv_dw_f = cat(dw_s, 2)
    conv_dw_b_f = cat(dwb_s, 1)
    conv_ln_g_f = cat(lng_s, 1)
    conv_ln_b_f = cat(lnb_s, 1)

    gathered_in = {0: _exchange(shards[0][0], "ag_w_in0", True)}
    gathered_out = {}

    h = h0
    saved = []
    for i in range(depth):
        j = i // 2
        has_next = i + 1 < depth
        gather = lambda x: (x, True) if has_next else None
        hn, hn_t = _rms_fwd(h, norm_g[i:i + 1], f"rms_fwd{i}")
        w = jnp.transpose(gathered_in.pop(i), (1, 0, 2)).reshape(D, -1)
        if is_fox(i):
            w_qkvg, w_f = w[:, :4 * W], _pad_lanes(w[:, 4 * W:])
            qkv, got_in = carried(hn, w_qkvg, "nn", BF16, f"fox_qkv{i}", n_out=3 * W,
                                  carry=gather(shards[i + 1][0] if has_next else None))
            gate, got = carried(hn, w_qkvg, "nn", F32, f"fox_gate{i}", b_col_off=3 * W, n_out=W,
                                carry=(shards[i][1], True))
            w_out = got.reshape(W, D)
            f = _matmul(hn, w_f, "nn", F32, f"fox_f{i}")
            b_f = _pad_lanes(fox_b_f[j:j + 1])
            c = _fgate_fwd(f, b_f, f"fgate_fwd{i}")
            tq = _tile(S, FLASH_TILE)
            cneg = (-c[:, :heads]).T.reshape(heads, S // tq, 1, tq)
            y, o, lse = _flash_fwd(qkv, gate, cneg, heads, f"flash_fwd{i}")
            h_next, got_out = carried(y, w_out, "nn", F32, f"out_proj{i}", add=h,
                                      carry=gather(shards[i + 1][1] if has_next else None))
            if has_next:
                gathered_in[i + 1], gathered_out[i + 1] = got_in, got_out
            saved.append(dict(h=h, hn_t=hn_t, qkv=qkv, gate=gate, f=f, b_f=b_f, cneg=cneg, y=y, o=o, lse=lse,
                              w_qkvg=w_qkvg, w_f=w_f, w_out=w_out))
        else:
            w_out = gathered_out.pop(i).reshape(C, D)
            nxt = shards[i + 1][0] if has_next else None
            proj, got_a = carried(hn, w, "nn", F32, f"conv_in{i}", bias=conv_b_in_f[j:j + 1],
                                  carry=gather(nxt[:split] if has_next else None))
            y, u1 = _conv_fwd(proj, conv_dw_f[j], conv_dw_b_f[j:j + 1], conv_ln_g_f[j:j + 1],
                              conv_ln_b_f[j:j + 1], f"conv_fwd{i}")
            h_next, got_b = carried(y, w_out, "nn", F32, f"out_proj{i}", add=h,
                                    carry=gather(nxt[split:] if has_next else None))
            if has_next:
                gathered_in[i + 1] = jnp.concatenate([got_a, got_b], axis=1)
            saved.append(dict(h=h, hn_t=hn_t, proj=proj, y=y, u1=u1, w_in=w, w_out=w_out))
        h = h_next

    dh, dh16, dh16_t, loss_part, dg_final = _loss_head(h, final_norm_g[None, :], target, "loss_head")

    def shard_cols(g, shard):
        return jnp.transpose(g.reshape(g.shape[0], N_DEV, shard), (1, 0, 2))

    d_norm_g = [None] * depth
    d_fox_b_f = [None] * n_fox
    d_conv_small = [None] * n_conv
    summed_in = [None] * depth
    summed_out = [None] * depth
    pend_in = pend_out = None
    for i in reversed(range(depth)):
        j = i // 2
        sv = saved[i]
        scatter = lambda x: (x, False) if x is not None else None
        dy = _matmul(dh16, sv["w_out"], "nt", F32, f"d_out_proj{i}")
        dw_out_t, got = carried(dh16_t, sv["y"], "nn", F32, f"dw_out{i}", carry=scatter(pend_out))
        if pend_out is not None:
            summed_out[i + 1] = got
        own_out = dw_out_t.T.reshape(N_DEV, -1, D).astype(BF16)
        if is_fox(i):
            do, dgate, delta = _fox_gate_bwd(dy, sv["o"], sv["gate"], f"fox_gate_bwd{i}")
            dq, dk, dv, colsum, rowsum = _flash_bwd(sv["qkv"], do, delta, sv["lse"], sv["cneg"], heads,
                                                    f"flash_bwd{i}")
            rowsum = jnp.max(rowsum.reshape(S, heads, LANES), axis=2)
            dc = _pad_lanes(rowsum - jnp.sum(colsum, axis=2).reshape(heads, S).T)
            df, dbf = _fgate_bwd(dc, sv["f"], sv["b_f"], f"fgate_bwd{i}")
            dproj = jnp.concatenate([dq, dk, dv, dgate], axis=1)
            dhn = _matmul(df, sv["w_f"], "nt", F32, f"d_fox_f{i}")
            dhn, got = carried(dproj, sv["w_qkvg"], "nt", F32, f"d_fox_in{i}", add=dhn, carry=scatter(pend_in))
            if pend_in is not None:
                summed_in[i + 1] = got
            dw_qkvg, summed_out[i] = carried(sv["hn_t"], dproj, "nn", F32, f"dw_fox_in{i}", carry=(own_out, False))
            dw_f = _matmul(sv["hn_t"], df, "nn", F32, f"dw_fox_f{i}")
            pend_in = shard_cols(jnp.concatenate([dw_qkvg, dw_f[:, :heads]], axis=1), fin_shard).astype(BF16)
            pend_out = None
            d_fox_b_f[j] = jnp.sum(dbf, axis=0)[:heads]
        else:
            du1, dgate, nsums = _conv_bwd_norm(dy, sv["proj"], sv["u1"], conv_ln_g_f[j:j + 1],
                                               conv_ln_b_f[j:j + 1], f"conv_bwd_norm{i}")
            da, db, ddw, absums = _conv_bwd_taps(du1, sv["proj"], conv_dw_f[j], f"conv_bwd_taps{i}")
            dproj = jnp.concatenate([da, db, dgate], axis=1)
            half = D // 2
            dhn, got_a = carried(dproj, sv["w_in"], "nt", F32, f"d_conv_in{i}",
                                 carry=scatter(pend_in[:, :half] if pend_in is not None else None))
            dw_in, got_b = carried(sv["hn_t"], dproj, "nn", F32, f"dw_conv_in{i}",
                                   carry=scatter(pend_in[:, half:] if pend_in is not None else None))
            if pend_in is not None:
                summed_in[i + 1] = jnp.concatenate([got_a, got_b], axis=1)
            pend_in, pend_out = shard_cols(dw_in, cin_shard).astype(BF16), own_out
            nsum = jnp.sum(nsums, axis=1)
            absum = jnp.sum(absums, axis=1)
            d_conv_small[j] = dict(b_in=jnp.concatenate([absum[0], absum[1], nsum[3]]),
                                   dw=jnp.sum(ddw, axis=1), dw_b=nsum[2], ln_g=nsum[0], ln_b=nsum[1])
        dh, dh16, dh16_t, dg = _rms_bwd(sv["h"], norm_g[i:i + 1], dhn, dh, f"rms_bwd{i}")
        d_norm_g[i] = jnp.sum(dg, axis=0)
    summed_in[0] = _exchange(pend_in, "rs_w_in0", False)
    if pend_out is not None:
        summed_out[0] = _exchange(pend_out, "rs_w_out0", False)
    grad_x = dh[None]

    def small_for(d):
        sl = lambda a, n: a[..., d * n:(d + 1) * n]
        return _flat_rows([
            jnp.stack([sl(s["b_in"], 3 * C // N_DEV) for s in d_conv_small]),
            jnp.stack([sl(s["dw"], C // N_DEV) for s in d_conv_small]),
            jnp.stack([sl(s["dw_b"], C // N_DEV) for s in d_conv_small]),
            jnp.stack([sl(s["ln_g"], C // N_DEV) for s in d_conv_small]),
            jnp.stack([sl(s["ln_b"], C // N_DEV) for s in d_conv_small])])
    r_small = _exchange(jnp.stack([small_for(d) for d in range(N_DEV)]), "rs_small", False)

    rep_shapes = [norm_g.shape, fox_b_f.shape, final_norm_g.shape, (1,)]
    rep_part = _flat_rows([jnp.stack(d_norm_g), jnp.stack(d_fox_b_f), jnp.sum(dg_final, axis=0),
                           jnp.sum(loss_part[:, 0])[None]])
    r_rep = _exchange(rep_part, "ag_replicated", True)

    def update(parts, w, m, v, name):
        two_d = (-1, w.shape[-1])
        stacked = jnp.stack(parts, axis=1).reshape((N_DEV,) + w.reshape(two_d).shape)
        res = _adamw(stacked, w.reshape(two_d), m.reshape(two_d), v.reshape(two_d), name)
        return [r.reshape(w.shape) for r in res]

    fox_layers = [i for i in range(depth) if is_fox(i)]
    conv_layers = [i for i in range(depth) if not is_fox(i)]
    u_fin = update([summed_in[i] for i in fox_layers], fox_w_in, m_fox_w_in, v_fox_w_in, "adamw_fox_w_in")
    u_fout = update([summed_out[i] for i in fox_layers], fox_w_out, m_fox_w_out, v_fox_w_out, "adamw_fox_w_out")
    u_cin = update([summed_in[i] for i in conv_layers], conv_w_in, m_conv_w_in, v_conv_w_in, "adamw_conv_w_in")
    u_cout = update([summed_out[i] for i in conv_layers], conv_w_out, m_conv_w_out, v_conv_w_out,
                    "adamw_conv_w_out")
    u_small = _adamw(r_small, small,
                     _flat_rows([m_conv_b_in, m_conv_dw, m_conv_dw_b, m_conv_ln_g, m_conv_ln_b]),
                     _flat_rows([v_conv_b_in, v_conv_dw, v_conv_dw_b, v_conv_ln_g, v_conv_ln_b]), "adamw_small")
    zero1 = jnp.zeros((1,), F32)
    u_rep = _adamw(r_rep, _flat_rows([norm_g, fox_b_f, final_norm_g, zero1]),
                   _flat_rows([m_norm_g, m_fox_b_f, m_final_norm_g, zero1]),
                   _flat_rows([v_norm_g, v_fox_b_f, v_final_norm_g, zero1]), "adamw_replicated")

    outs = []
    loss = None
    for kind in range(4):
        b_in_k, dw_k, dwb_k, lng_k, lnb_k = _unflat(u_small[kind], small_shapes)
        ng_k, bf_k, fg_k, loss_k = _unflat(u_rep[kind], rep_shapes)
        if kind == 0:
            loss = loss_k[0]
        outs += [ng_k, u_fin[kind], bf_k, u_fout[kind], u_cin[kind], b_in_k, dw_k, dwb_k, lng_k, lnb_k,
                 u_cout[kind], fg_k]
    return (loss, grad_x, *outs)
```

```python
import functools

import numpy as np
import jax
import jax.numpy as jnp
from jax import lax
from jax.experimental import pallas as pl
from jax.experimental.pallas import tpu as pltpu

F32 = jnp.float32
BF16 = jnp.bfloat16
MESH_ID = pl.DeviceIdType.MESH

N_DEV = 8
RMS_EPS = 1e-6
LN_EPS = 1e-5
ADAM_LR = 0.001
ADAM_B1 = 0.9
ADAM_B2 = 0.999
ADAM_EPS = 1e-08
ADAM_WD = 0.01
ADAM_STEP = 10

LANES = 128
SUBLANES = 8
VMEM_LIMIT = 56 * 1024 * 1024
NEG_BIG = -1e30
CONV_HALO = 32
FLASH_TILE = 512
CONV_ROWS = 128
CONV_CHUNK = 32
FGATE_ROWS = 256
ROW_TILE = 256


def _params(*sem):
    return pltpu.CompilerParams(dimension_semantics=sem if sem else None, vmem_limit_bytes=VMEM_LIMIT)


def _tile(n, pref):
    if n <= pref:
        return n
    t = pref
    while n % t:
        t //= 2
    return t


def _sigmoid(x):
    return 1.0 / (1.0 + jnp.exp(-x))


def _dsilu(x, s):
    return s * (1.0 + x * (1.0 - s))


def _rows(n, chunk, fn):
    def step(i, carry):
        fn(pl.multiple_of(i * chunk, chunk))
        return carry
    lax.fori_loop(0, n // chunk, step, 0)


def _peer(k):
    x, y, c = lax.axis_index("x"), lax.axis_index("y"), lax.axis_index("c")
    px = 1 - x if (k >> 2) & 1 else x
    py = 1 - y if (k >> 1) & 1 else y
    pc = 1 - c if k & 1 else c
    return (px, py, pc), 4 * px + 2 * py + pc


def _exchange_copies(x_ref, o_ref, send_sems, recv_sems, local_sem, gather):
    _, me = _peer(0)
    mine = x_ref if gather else x_ref.at[me]
    local = pltpu.make_async_copy(mine, o_ref.at[me], local_sem)
    sends, arrivals = [], []
    for k in range(1, N_DEV):
        peer, pidx = _peer(k)
        sems = dict(send_sem=send_sems.at[k - 1], recv_sem=recv_sems.at[k - 1], device_id=peer,
                    device_id_type=MESH_ID)
        sends.append(pltpu.make_async_remote_copy(src_ref=x_ref if gather else x_ref.at[pidx],
                                                  dst_ref=o_ref.at[me], **sems))
        arrivals.append(pltpu.make_async_remote_copy(src_ref=mine, dst_ref=o_ref.at[pidx], **sems))
    return local, sends, arrivals


def _exchange_start(*refs, gather):
    local, sends, _ = _exchange_copies(*refs, gather)
    local.start()
    for cp in sends:
        cp.start()


def _exchange_wait(*refs, gather):
    local, sends, arrivals = _exchange_copies(*refs, gather)
    for cp in arrivals:
        cp.wait_recv()
    for cp in sends:
        cp.wait_send()
    local.wait()


EXCHANGE_SCRATCH = [pltpu.SemaphoreType.DMA((N_DEV - 1,)), pltpu.SemaphoreType.DMA((N_DEV - 1,)),
                    pltpu.SemaphoreType.DMA]


def _exchange_shape(x, gather):
    return jax.ShapeDtypeStruct((N_DEV,) + x.shape if gather else x.shape, x.dtype)


def _exchange(x, name, gather):
    def body(*refs):
        _exchange_start(*refs, gather=gather)
        _exchange_wait(*refs, gather=gather)

    return pl.pallas_call(
        body, name=name,
        out_shape=_exchange_shape(x, gather),
        in_specs=[pl.BlockSpec(memory_space=pl.ANY)],
        out_specs=pl.BlockSpec(memory_space=pl.ANY),
        scratch_shapes=list(EXCHANGE_SCRATCH),
    )(x)


def _matmul(a, b, mode, out_dtype, name, bias=None, add=None, b_col_off=0, n_out=None, carry=None):
    M, K = a.shape
    N = n_out if n_out is not None else (b.shape[0] if mode == "nt" else b.shape[1])
    tm = _tile(M, 512)
    tn = _tile(N, 1024)
    k_cap = 4096 if a.dtype.itemsize == 2 and b.dtype.itemsize == 2 else 2048
    tk = next(K // d for d in range(1, K + 1) if K % d == 0 and K // d <= k_cap and (K // d) % LANES == 0)
    nm, nn, nk = M // tm, N // tn, K // tk
    assert b_col_off % tn == 0
    joff = b_col_off // tn
    dims = {"nn": (((1,), (0,)), ((), ())), "nt": (((1,), (1,)), ((), ()))}[mode]
    n_in = 2 + (bias is not None) + (add is not None)

    def body(*refs):
        a_ref, b_ref = refs[0], refs[1]
        bias_ref = refs[2] if bias is not None else None
        add_ref = refs[n_in - 1] if add is not None else None
        pos = n_in
        x_ref = o_ref = x_out_ref = None
        if carry is not None:
            x_ref, o_ref, x_out_ref = refs[pos], refs[pos + 1], refs[pos + 2]
            pos += 3
        else:
            o_ref = refs[pos]
            pos += 1
        acc_ref = None
        if nk > 1:
            acc_ref = refs[pos]
            pos += 1
        exchange_refs = (x_ref, x_out_ref) + tuple(refs[pos:])
        i, j, kk = pl.program_id(0), pl.program_id(1), pl.program_id(2)

        if carry is not None:
            @pl.when((i == 0) & (j == 0) & (kk == 0))
            def _():
                _exchange_start(*exchange_refs, gather=carry[1])

        part = lax.dot_general(a_ref[...].astype(BF16), b_ref[...].astype(BF16), dims,
                               preferred_element_type=F32)

        def finish(r):
            if bias_ref is not None:
                r = r + bias_ref[...]
            if add_ref is not None:
                r = r + add_ref[...]
            o_ref[...] = r.astype(o_ref.dtype)

        if nk == 1:
            finish(part)
        else:
            @pl.when(kk == 0)
            def _():
                acc_ref[...] = part

            @pl.when(kk > 0)
            def _():
                acc_ref[...] += part

            @pl.when(kk == nk - 1)
            def _():
                finish(acc_ref[...])

        if carry is not None:
            @pl.when((i == nm - 1) & (j == nn - 1) & (kk == nk - 1))
            def _():
                _exchange_wait(*exchange_refs, gather=carry[1])

    a_spec = pl.BlockSpec((tm, tk), lambda i, j, k: (i, k))
    if mode == "nt":
        b_spec = pl.BlockSpec((tn, tk), lambda i, j, k: (j, k))
    else:
        b_spec = pl.BlockSpec((tk, tn), lambda i, j, k: (k, j + joff))
    in_specs = [a_spec, b_spec]
    args = [a, b]
    if bias is not None:
        in_specs.append(pl.BlockSpec((1, tn), lambda i, j, k: (0, j)))
        args.append(bias)
    if add is not None:
        in_specs.append(pl.BlockSpec((tm, tn), lambda i, j, k: (i, j)))
        args.append(add)
    out_specs = [pl.BlockSpec((tm, tn), lambda i, j, k: (i, j))]
    out_shape = [jax.ShapeDtypeStruct((M, N), out_dtype)]
    scratch = [pltpu.VMEM((tm, tn), F32)] if nk > 1 else []
    if carry is not None:
        in_specs.append(pl.BlockSpec(memory_space=pl.ANY))
        args.append(carry[0])
        out_specs.append(pl.BlockSpec(memory_space=pl.ANY))
        out_shape.append(_exchange_shape(*carry))
        scratch += EXCHANGE_SCRATCH
    res = pl.pallas_call(
        body, name=name,
        grid=(nm, nn, nk),
        in_specs=in_specs, out_specs=out_specs, out_shape=out_shape, scratch_shapes=scratch,
        compiler_params=_params(*(("arbitrary",) * 3 if carry is not None else ("parallel", "parallel", "arbitrary"))),
    )(*args)
    return res if carry is not None else res[0]


def _rms_fwd(h, g, name):
    S, D = h.shape
    ts = _tile(S, 512)

    def body(h_ref, g_ref, o_ref, ot_ref):
        x = h_ref[...]
        r = lax.rsqrt(jnp.mean(x * x, axis=-1, keepdims=True) + RMS_EPS)
        y = x * r * g_ref[...]
        o_ref[...] = y.astype(BF16)
        ot_ref[...] = y.T.astype(BF16)

    return pl.pallas_call(
        body, name=name, grid=(S // ts,),
        in_specs=[pl.BlockSpec((ts, D), lambda i: (i, 0)), pl.BlockSpec((1, D), lambda i: (0, 0))],
        out_specs=[pl.BlockSpec((ts, D), lambda i: (i, 0)), pl.BlockSpec((D, ts), lambda i: (0, i))],
        out_shape=[jax.ShapeDtypeStruct((S, D), BF16), jax.ShapeDtypeStruct((D, S), BF16)],
        compiler_params=_params("parallel"),
    )(h, g)


def _rms_bwd_block(x, g, dy):
    r = lax.rsqrt(jnp.mean(x * x, axis=-1, keepdims=True) + RMS_EPS)
    xr = x * r
    t = dy * g
    dx = r * (t - xr * jnp.mean(t * xr, axis=-1, keepdims=True))
    return dx, dy * xr


def _rms_bwd(h, g, dhn, dh, name):
    S, D = h.shape
    ts = _tile(S, ROW_TILE)

    def body(h_ref, g_ref, dhn_ref, dh_ref, o_ref, o16_ref, o16t_ref, dg_ref):
        dx, dgt = _rms_bwd_block(h_ref[...], g_ref[...], dhn_ref[...])
        out = dh_ref[...] + dx
        o_ref[...] = out
        o16_ref[...] = out.astype(BF16)
        o16t_ref[...] = out.T.astype(BF16)
        part = jnp.sum(dgt.reshape(ts // SUBLANES, SUBLANES, D), axis=0)

        @pl.when(pl.program_id(0) == 0)
        def _():
            dg_ref[...] = part

        @pl.when(pl.program_id(0) > 0)
        def _():
            dg_ref[...] += part

    row = pl.BlockSpec((ts, D), lambda i: (i, 0))
    return pl.pallas_call(
        body, name=name, grid=(S // ts,),
        in_specs=[row, pl.BlockSpec((1, D), lambda i: (0, 0)), row, row],
        out_specs=[row, row, pl.BlockSpec((D, ts), lambda i: (0, i)),
                   pl.BlockSpec((SUBLANES, D), lambda i: (0, 0))],
        out_shape=[jax.ShapeDtypeStruct((S, D), F32), jax.ShapeDtypeStruct((S, D), BF16),
                   jax.ShapeDtypeStruct((D, S), BF16), jax.ShapeDtypeStruct((SUBLANES, D), F32)],
        compiler_params=_params("arbitrary"),
    )(h, g, dhn, dh)


def _loss_head(h, g, target, name):
    S, D = h.shape
    ts = _tile(S, ROW_TILE)

    def body(h_ref, g_ref, t_ref, o_ref, o16_ref, o16t_ref, loss_ref, dg_ref):
        x = h_ref[...]
        gg = g_ref[...]
        r = lax.rsqrt(jnp.mean(x * x, axis=-1, keepdims=True) + RMS_EPS)
        err = x * r * gg - t_ref[...]
        row_loss = 0.5 * jnp.mean(err * err, axis=-1, keepdims=True)
        dx, dgt = _rms_bwd_block(x, gg, err * (1.0 / D))
        o_ref[...] = dx
        o16_ref[...] = dx.astype(BF16)
        o16t_ref[...] = dx.T.astype(BF16)
        part = jnp.sum(dgt.reshape(ts // SUBLANES, SUBLANES, D), axis=0)
        lpart = jnp.sum(jnp.broadcast_to(row_loss, (ts, LANES)).reshape(ts // SUBLANES, SUBLANES, LANES), axis=0)

        @pl.when(pl.program_id(0) == 0)
        def _():
            dg_ref[...] = part
            loss_ref[...] = lpart

        @pl.when(pl.program_id(0) > 0)
        def _():
            dg_ref[...] += part
            loss_ref[...] += lpart

    row = pl.BlockSpec((ts, D), lambda i: (i, 0))
    return pl.pallas_call(
        body, name=name, grid=(S // ts,),
        in_specs=[row, pl.BlockSpec((1, D), lambda i: (0, 0)), row],
        out_specs=[row, row, pl.BlockSpec((D, ts), lambda i: (0, i)),
                   pl.BlockSpec((SUBLANES, LANES), lambda i: (0, 0)), pl.BlockSpec((SUBLANES, D), lambda i: (0, 0))],
        out_shape=[jax.ShapeDtypeStruct((S, D), F32), jax.ShapeDtypeStruct((S, D), BF16),
                   jax.ShapeDtypeStruct((D, S), BF16), jax.ShapeDtypeStruct((SUBLANES, LANES), F32),
                   jax.ShapeDtypeStruct((SUBLANES, D), F32)],
        compiler_params=_params("arbitrary"),
    )(h, g, target)


def _split3(x):
    hi = x.astype(BF16)
    r1 = x - hi.astype(F32)
    mid = r1.astype(BF16)
    lo = (r1 - mid.astype(F32)).astype(BF16)
    return hi, mid, lo


def _tri_sum(tri, x):
    hi, mid, lo = _split3(x)
    dot = functools.partial(jnp.dot, preferred_element_type=F32)
    return dot(tri, hi) + dot(tri, mid) + dot(tri, lo)


def _fgate_fwd(f, b_f, name):
    S = f.shape[0]
    tb = _tile(S, FGATE_ROWS)

    def body(f_ref, b_ref, c_ref, carry_ref):
        @pl.when(pl.program_id(0) == 0)
        def _():
            carry_ref[...] = jnp.zeros_like(carry_ref)

        x = f_ref[...] + b_ref[...]
        lf = jnp.minimum(x, 0.0) - jnp.log1p(jnp.exp(-jnp.abs(x)))
        r = lax.broadcasted_iota(jnp.int32, (tb, tb), 0)
        c = lax.broadcasted_iota(jnp.int32, (tb, tb), 1)
        tri = (c <= r).astype(BF16)
        c_ref[...] = _tri_sum(tri, lf) + carry_ref[0:1, :]
        carry_ref[...] += _tri_sum(jnp.ones((SUBLANES, tb), BF16), lf)

    return pl.pallas_call(
        body, name=name, grid=(S // tb,),
        in_specs=[pl.BlockSpec((tb, LANES), lambda i: (i, 0)), pl.BlockSpec((1, LANES), lambda i: (0, 0))],
        out_specs=pl.BlockSpec((tb, LANES), lambda i: (i, 0)),
        out_shape=jax.ShapeDtypeStruct((S, LANES), F32),
        scratch_shapes=[pltpu.VMEM((SUBLANES, LANES), F32)],
        compiler_params=_params("arbitrary"),
    )(f, b_f)


def _fgate_bwd(dc, f, b_f, name):
    S = f.shape[0]
    tb = _tile(S, FGATE_ROWS)
    nb = S // tb

    def body(dc_ref, f_ref, b_ref, df_ref, db_ref, carry_ref):
        @pl.when(pl.program_id(0) == 0)
        def _():
            carry_ref[...] = jnp.zeros_like(carry_ref)
            db_ref[...] = jnp.zeros_like(db_ref)

        d = dc_ref[...]
        r = lax.broadcasted_iota(jnp.int32, (tb, tb), 0)
        c = lax.broadcasted_iota(jnp.int32, (tb, tb), 1)
        tri = (c >= r).astype(BF16)
        dlf = _tri_sum(tri, d) + carry_ref[0:1, :]
        carry_ref[...] += _tri_sum(jnp.ones((SUBLANES, tb), BF16), d)
        df = dlf * _sigmoid(-(f_ref[...] + b_ref[...]))
        df_ref[...] = df
        db_ref[...] += jnp.sum(df.reshape(tb // SUBLANES, SUBLANES, LANES), axis=0)

    rev = pl.BlockSpec((tb, LANES), lambda i: (nb - 1 - i, 0))
    return pl.pallas_call(
        body, name=name, grid=(nb,),
        in_specs=[rev, rev, pl.BlockSpec((1, LANES), lambda i: (0, 0))],
        out_specs=[rev, pl.BlockSpec((SUBLANES, LANES), lambda i: (0, 0))],
        out_shape=[jax.ShapeDtypeStruct((S, LANES), F32), jax.ShapeDtypeStruct((SUBLANES, LANES), F32)],
        scratch_shapes=[pltpu.VMEM((SUBLANES, LANES), F32)],
        compiler_params=_params("arbitrary"),
    )(dc, f, b_f)


LOG2E = 1.4426950408889634
NT_DIMS = (((1,), (1,)), ((), ()))
TN_DIMS = (((0,), (0,)), ((), ()))


def _causal_mask(t):
    r = lax.broadcasted_iota(jnp.int32, (t, t), 0)
    c = lax.broadcasted_iota(jnp.int32, (t, t), 1)
    return r >= c


def _key_tiles(qi, tile, group):
    def several(i, carry):
        for u in range(group):
            tile(group * i + u, False)
        return carry
    lax.fori_loop(0, qi // group, several, 0)

    def single(j, carry):
        tile(j, False)
        return carry
    lax.fori_loop((qi // group) * group, qi, single, 0)
    tile(qi, True)


def _flash_fwd(qkv, gate, cneg, heads, name):
    S, W3 = qkv.shape
    W = W3 // 3
    dh = W // heads
    assert dh == LANES
    tq = _tile(S, FLASH_TILE)
    nq = S // tq
    c1 = dh ** -0.5 * LOG2E

    def body(q_ref, k_ref, v_ref, b_ref, g_ref, y_ref, o_ref, lse_ref, m_scr, l_scr, acc_scr):
        qi = pl.program_id(1)
        m_scr[...] = jnp.full_like(m_scr, NEG_BIG)
        l_scr[...] = jnp.zeros_like(l_scr)
        acc_scr[...] = jnp.zeros_like(acc_scr)
        q = q_ref[...]

        def tile(j, diagonal):
            rows = pl.ds(pl.multiple_of(j * tq, tq), tq)
            t = lax.dot_general(q, k_ref[rows, :], NT_DIMS, preferred_element_type=F32) * c1 + b_ref[0, j] * LOG2E
            if diagonal:
                t = jnp.where(_causal_mask(tq), t, NEG_BIG)
            m_prev = m_scr[...]
            m_next = jnp.maximum(m_prev, jnp.max(t, axis=1, keepdims=True))
            alpha = jnp.exp2(m_prev - m_next)
            p = jnp.exp2(t - m_next[:, :1])
            l_scr[...] = alpha * l_scr[...] + jnp.sum(p, axis=1, keepdims=True)
            acc_scr[...] = alpha * acc_scr[...] + jnp.dot(p.astype(BF16), v_ref[rows, :],
                                                          preferred_element_type=F32)
            m_scr[...] = m_next

        _key_tiles(qi, tile, 4)
        l = l_scr[...]
        o = acc_scr[...] / l
        g = g_ref[...]
        o_ref[...] = o
        lse_ref[...] = m_scr[...] + jnp.log2(l)
        y_ref[...] = (o * (g * _sigmoid(g))).astype(BF16)

    qblk = pl.BlockSpec((tq, dh), lambda h, i: (i, h))
    return pl.pallas_call(
        body, name=name, grid=(heads, nq),
        in_specs=[qblk,
                  pl.BlockSpec((S, dh), lambda h, i: (0, heads + h)),
                  pl.BlockSpec((S, dh), lambda h, i: (0, 2 * heads + h)),
                  pl.BlockSpec((1, nq, 1, tq), lambda h, i: (h, 0, 0, 0)),
                  qblk],
        out_specs=[qblk, qblk, qblk],
        out_shape=[jax.ShapeDtypeStruct((S, W), BF16), jax.ShapeDtypeStruct((S, W), F32),
                   jax.ShapeDtypeStruct((S, W), F32)],
        scratch_shapes=[pltpu.VMEM((tq, LANES), F32), pltpu.VMEM((tq, LANES), F32), pltpu.VMEM((tq, dh), F32)],
        compiler_params=_params("parallel", "arbitrary"),
    )(qkv, qkv, qkv, cneg, gate)


def _flash_bwd(qkv, do, delta, lse, cneg, heads, name):
    S, W3 = qkv.shape
    W = W3 // 3
    dh = W // heads
    tq = _tile(S, FLASH_TILE)
    nq = S // tq
    scale = dh ** -0.5
    c1 = scale * LOG2E

    def body(q_ref, k_ref, v_ref, b_ref, do_ref, delta_ref, lse_ref,
             dq_ref, dk_ref, dv_ref, db_ref, rs_ref, dq_scr, dk_scr, dv_scr):
        qi = pl.program_id(1)

        @pl.when(qi == 0)
        def _():
            dk_scr[...] = jnp.zeros_like(dk_scr)
            dv_scr[...] = jnp.zeros_like(dv_scr)
            db_ref[...] = jnp.zeros_like(db_ref)

        dq_scr[...] = jnp.zeros_like(dq_scr)
        rs_ref[...] = jnp.zeros_like(rs_ref)
        q, d_o = q_ref[...], do_ref[...]
        lse = lse_ref[...][:, :1]
        delta = delta_ref[...][:, :1]

        def tile(j, diagonal):
            rows = pl.ds(pl.multiple_of(j * tq, tq), tq)
            k, v = k_ref[rows, :], v_ref[rows, :]
            t = lax.dot_general(q, k, NT_DIMS, preferred_element_type=F32) * c1 + b_ref[0, j] * LOG2E
            p = jnp.exp2(t - lse)
            if diagonal:
                p = jnp.where(_causal_mask(tq), p, 0.0)
            dp = lax.dot_general(d_o, v, NT_DIMS, preferred_element_type=F32)
            ds = p * (dp - delta)
            dv_scr[rows, :] += lax.dot_general(p.astype(BF16), d_o, TN_DIMS, preferred_element_type=F32)
            db_ref[0, j] += jnp.sum(ds.reshape(tq // SUBLANES, SUBLANES, tq), axis=0)
            dsb = (ds * scale).astype(BF16)
            dk_scr[rows, :] += lax.dot_general(dsb, q, TN_DIMS, preferred_element_type=F32)
            dq_scr[...] += jnp.dot(dsb, k, preferred_element_type=F32)
            rs_ref[...] += jnp.sum(ds, axis=1, keepdims=True)

        _key_tiles(qi, tile, 4)
        dq_ref[...] = dq_scr[...].astype(BF16)

        @pl.when(qi == nq - 1)
        def _():
            dk_ref[...] = dk_scr[...].astype(BF16)
            dv_ref[...] = dv_scr[...].astype(BF16)

    qblk = pl.BlockSpec((tq, dh), lambda h, i: (i, h))
    head = pl.BlockSpec((S, dh), lambda h, i: (0, h))
    return pl.pallas_call(
        body, name=name, grid=(heads, nq),
        in_specs=[qblk,
                  pl.BlockSpec((S, dh), lambda h, i: (0, heads + h)),
                  pl.BlockSpec((S, dh), lambda h, i: (0, 2 * heads + h)),
                  pl.BlockSpec((1, nq, 1, tq), lambda h, i: (h, 0, 0, 0)),
                  qblk, qblk, qblk],
        out_specs=[qblk, head, head,
                   pl.BlockSpec((1, nq, SUBLANES, tq), lambda h, i: (h, 0, 0, 0)),
                   qblk],
        out_shape=[jax.ShapeDtypeStruct((S, W), BF16), jax.ShapeDtypeStruct((S, W), BF16),
                   jax.ShapeDtypeStruct((S, W), BF16), jax.ShapeDtypeStruct((heads, nq, SUBLANES, tq), F32),
                   jax.ShapeDtypeStruct((S, W), F32)],
        scratch_shapes=[pltpu.VMEM((tq, dh), F32), pltpu.VMEM((S, dh), F32), pltpu.VMEM((S, dh), F32)],
        compiler_params=_params("parallel", "arbitrary"),
    )(qkv, qkv, qkv, cneg, do, delta, lse)


def _fox_gate_bwd(dy, o, gate, name):
    S, W = dy.shape
    ts = _tile(S, ROW_TILE)

    def body(dy_ref, o_ref, g_ref, do_ref, dg_ref, delta_ref):
        d, g, o_val = dy_ref[...], g_ref[...], o_ref[...]
        sg = _sigmoid(g)
        d_o = (d * (g * sg)).astype(BF16)
        do_ref[...] = d_o
        dg_ref[...] = (d * o_val * _dsilu(g, sg)).astype(BF16)
        prod = d_o.astype(F32) * o_val
        for h in range(W // LANES):
            cols = slice(h * LANES, (h + 1) * LANES)
            delta_ref[:, cols] = jnp.broadcast_to(jnp.sum(prod[:, cols], axis=1, keepdims=True), (ts, LANES))

    row = pl.BlockSpec((ts, W), lambda i: (i, 0))
    return pl.pallas_call(
        body, name=name, grid=(S // ts,),
        in_specs=[row, row, row], out_specs=[row, row, row],
        out_shape=[jax.ShapeDtypeStruct((S, W), BF16), jax.ShapeDtypeStruct((S, W), BF16),
                   jax.ShapeDtypeStruct((S, W), F32)],
        compiler_params=_params("parallel"),
    )(dy, o, gate)


def _shifted_copies(buf_ref, sh_ref, rows):
    for j in range(1, SUBLANES):
        sh_ref[j, 0:rows, :] = buf_ref[j:j + rows, :]


def _tap(buf_ref, sh_ref, r0, off, cols):
    j, base = off % SUBLANES, off - off % SUBLANES
    if j == 0:
        return buf_ref[pl.ds(r0 + base, SUBLANES), cols]
    return sh_ref[j, pl.ds(r0 + base, SUBLANES), cols]


def _tap_weights(dw_ref, cols):
    return [jnp.broadcast_to(dw_ref[k:k + 1, cols], (SUBLANES, LANES)) for k in range(dw_ref.shape[0])]


def _conv_fwd(proj, dw, dw_b, ln_g, ln_b, name):
    S, C3 = proj.shape
    C = C3 // 3
    K = dw.shape[0]
    assert K - 1 <= CONV_HALO - 2
    ts = _tile(S, CONV_ROWS)
    hb = ts // CONV_HALO
    nrows = ts + CONV_HALO
    lead = CONV_HALO - (K - 1)

    def body(a_ref, b_ref, ah_ref, bh_ref, g_ref, dw_ref, dwb_ref, lg_ref, lb_ref, y_ref, u1_ref, buf, sh):
        first = pl.program_id(0) == 0
        buf[0:CONV_HALO, :] = jnp.where(first, 0.0, ah_ref[...] * _sigmoid(bh_ref[...]))

        def glu(r0):
            rows = pl.ds(r0, CONV_CHUNK)
            buf[pl.ds(r0 + CONV_HALO, CONV_CHUNK), :] = a_ref[rows, :] * _sigmoid(b_ref[rows, :])
        _rows(ts, CONV_CHUNK, glu)
        _shifted_copies(buf, sh, nrows - SUBLANES)

        for s in range(C // LANES):
            cols = slice(s * LANES, (s + 1) * LANES)
            w = _tap_weights(dw_ref, cols)
            bias = jnp.broadcast_to(dwb_ref[:, cols], (SUBLANES, LANES))

            def taps(r0, w=w, bias=bias, cols=cols):
                for u in range(CONV_CHUNK // SUBLANES):
                    r = r0 + u * SUBLANES
                    acc = bias
                    for k in range(K):
                        acc = acc + w[k] * _tap(buf, sh, r, lead + k, cols)
                    u1_ref[pl.ds(r, SUBLANES), cols] = acc
            _rows(ts, CONV_CHUNK, taps)

        def norm(r0):
            rows = pl.ds(r0, CONV_CHUNK)
            u1 = u1_ref[rows, :]
            mu = jnp.mean(u1, axis=-1, keepdims=True)
            xc = u1 - mu
            rstd = lax.rsqrt(jnp.mean(xc * xc, axis=-1, keepdims=True) + LN_EPS)
            z = xc * rstd * lg_ref[...] + lb_ref[...]
            g = g_ref[rows, :]
            y_ref[rows, :] = ((z * _sigmoid(z)) * (g * _sigmoid(g))).astype(BF16)
        _rows(ts, CONV_CHUNK, norm)

    row = lambda col: pl.BlockSpec((ts, C), lambda i: (i, col))
    halo = lambda col: pl.BlockSpec((CONV_HALO, C), lambda i: (jnp.maximum(i * hb - 1, 0), col))
    vec = pl.BlockSpec((1, C), lambda i: (0, 0))
    return pl.pallas_call(
        body, name=name, grid=(S // ts,),
        in_specs=[row(0), row(1), halo(0), halo(1), row(2), pl.BlockSpec((K, C), lambda i: (0, 0)), vec, vec, vec],
        out_specs=[row(0), row(0)],
        out_shape=[jax.ShapeDtypeStruct((S, C), BF16), jax.ShapeDtypeStruct((S, C), F32)],
        scratch_shapes=[pltpu.VMEM((nrows, C), F32), pltpu.VMEM((SUBLANES, nrows, C), F32)],
        compiler_params=_params("parallel"),
    )(proj, proj, proj, proj, proj, dw, dw_b, ln_g, ln_b)


def _conv_bwd_norm(dy, proj, u1, ln_g, ln_b, name):
    S, C = dy.shape
    ts = _tile(S, ROW_TILE)
    groups = CONV_CHUNK // SUBLANES

    def fold(x):
        return jnp.sum(x.reshape(groups, SUBLANES, C), axis=0)

    def body(dy_ref, g_ref, u1_ref, lg_ref, lb_ref, du1_ref, dg_ref, sums_ref):
        @pl.when(pl.program_id(0) == 0)
        def _():
            sums_ref[...] = jnp.zeros_like(sums_ref)

        def chunk(r0):
            rows = pl.ds(r0, CONV_CHUNK)
            d, g, u1 = dy_ref[rows, :], g_ref[rows, :], u1_ref[rows, :]
            mu = jnp.mean(u1, axis=-1, keepdims=True)
            xc = u1 - mu
            rstd = lax.rsqrt(jnp.mean(xc * xc, axis=-1, keepdims=True) + LN_EPS)
            xh = xc * rstd
            z = xh * lg_ref[...] + lb_ref[...]
            sz, sg = _sigmoid(z), _sigmoid(g)
            dgate = d * (z * sz) * _dsilu(g, sg)
            dz = d * (g * sg) * _dsilu(z, sz)
            dxh = dz * lg_ref[...]
            du1 = rstd * (dxh - jnp.mean(dxh, axis=-1, keepdims=True)
                          - xh * jnp.mean(dxh * xh, axis=-1, keepdims=True))
            du1_ref[rows, :] = du1
            dg_ref[rows, :] = dgate.astype(BF16)
            sums_ref[0] += fold(dz * xh)
            sums_ref[1] += fold(dz)
            sums_ref[2] += fold(du1)
            sums_ref[3] += fold(dgate)
        _rows(ts, CONV_CHUNK, chunk)

    row = pl.BlockSpec((ts, C), lambda i: (i, 0))
    vec = pl.BlockSpec((1, C), lambda i: (0, 0))
    return pl.pallas_call(
        body, name=name, grid=(S // ts,),
        in_specs=[row, pl.BlockSpec((ts, C), lambda i: (i, 2)), row, vec, vec],
        out_specs=[row, row, pl.BlockSpec((4, SUBLANES, C), lambda i: (0, 0, 0))],
        out_shape=[jax.ShapeDtypeStruct((S, C), F32), jax.ShapeDtypeStruct((S, C), BF16),
                   jax.ShapeDtypeStruct((4, SUBLANES, C), F32)],
        compiler_params=_params("arbitrary"),
    )(dy, proj, u1, ln_g, ln_b)


def _conv_bwd_taps(du1, proj, dw, name):
    S, C = du1.shape
    K = dw.shape[0]
    ts = _tile(S, CONV_ROWS)
    hb = ts // CONV_HALO
    nblk = S // ts
    last_halo = S // CONV_HALO - 1
    nrows = ts + CONV_HALO
    groups = CONV_CHUNK // SUBLANES

    def body(d_ref, dn_ref, a_ref, b_ref, dw_ref, da_ref, db_ref, ddw_ref, sums_ref, dbuf, dsh):
        i = pl.program_id(0)

        @pl.when(i == 0)
        def _():
            ddw_ref[...] = jnp.zeros_like(ddw_ref)
            sums_ref[...] = jnp.zeros_like(sums_ref)

        dbuf[ts:nrows, :] = jnp.where(i == nblk - 1, 0.0, dn_ref[...])

        def fill(r0):
            rows = pl.ds(r0, CONV_CHUNK)
            dbuf[rows, :] = d_ref[rows, :]
        _rows(ts, CONV_CHUNK, fill)
        _shifted_copies(dbuf, dsh, nrows - SUBLANES)

        zero = jnp.zeros((SUBLANES, LANES), F32)
        fold = lambda x: jnp.sum(x.reshape(groups, SUBLANES, LANES), axis=0)
        for s in range(C // LANES):
            cols = slice(s * LANES, (s + 1) * LANES)
            w = _tap_weights(dw_ref, cols)

            def step(j, carry, w=w, cols=cols):
                r0 = pl.multiple_of(j * CONV_CHUNK, CONV_CHUNK)
                rows = pl.ds(r0, CONV_CHUNK)
                a, sb = a_ref[rows, cols], _sigmoid(b_ref[rows, cols])
                u0 = a * sb
                accs = list(carry[2:])
                parts = []
                for u in range(groups):
                    u0_u = u0[u * SUBLANES:(u + 1) * SUBLANES]
                    acc = zero
                    for k in range(K):
                        x = _tap(dbuf, dsh, r0 + u * SUBLANES, K - 1 - k, cols)
                        acc = acc + w[k] * x
                        accs[k] = accs[k] + u0_u * x
                    parts.append(acc)
                da = jnp.concatenate(parts, axis=0) * sb
                db = da * a * (1.0 - sb)
                da_ref[rows, cols] = da.astype(BF16)
                db_ref[rows, cols] = db.astype(BF16)
                return (carry[0] + fold(da), carry[1] + fold(db), *accs)
            out = lax.fori_loop(0, ts // CONV_CHUNK, step, (zero,) * (K + 2))
            sums_ref[0, :, cols] += out[0]
            sums_ref[1, :, cols] += out[1]
            for k in range(K):
                ddw_ref[k, :, cols] += out[2 + k]

    row = lambda col: pl.BlockSpec((ts, C), lambda i: (i, col))
    nxt = pl.BlockSpec((CONV_HALO, C), lambda i: (jnp.minimum((i + 1) * hb, last_halo), 0))
    return pl.pallas_call(
        body, name=name, grid=(nblk,),
        in_specs=[row(0), nxt, row(0), row(1), pl.BlockSpec((K, C), lambda i: (0, 0))],
        out_specs=[row(0), row(0), pl.BlockSpec((K, SUBLANES, C), lambda i: (0, 0, 0)),
                   pl.BlockSpec((2, SUBLANES, C), lambda i: (0, 0, 0))],
        out_shape=[jax.ShapeDtypeStruct((S, C), BF16), jax.ShapeDtypeStruct((S, C), BF16),
                   jax.ShapeDtypeStruct((K, SUBLANES, C), F32), jax.ShapeDtypeStruct((2, SUBLANES, C), F32)],
        scratch_shapes=[pltpu.VMEM((nrows, C), F32), pltpu.VMEM((SUBLANES, nrows, C), F32)],
        compiler_params=_params("arbitrary"),
    )(du1, du1, proj, proj, dw)


def _adamw(parts, w, m, v, name):
    R, C = w.shape
    tr = _tile(R, 256)
    c1 = 1.0 - ADAM_B1 ** ADAM_STEP
    c2 = 1.0 - ADAM_B2 ** ADAM_STEP

    def body(p_ref, w_ref, m_ref, v_ref, g_ref, d_ref, nm_ref, nv_ref):
        g = p_ref[0].astype(F32)
        for d in range(1, N_DEV):
            g = g + p_ref[d].astype(F32)
        nm = ADAM_B1 * m_ref[...] + (1.0 - ADAM_B1) * g
        nv = ADAM_B2 * v_ref[...] + (1.0 - ADAM_B2) * (g * g)
        g_ref[...] = g
        nm_ref[...] = nm
        nv_ref[...] = nv
        d_ref[...] = -ADAM_LR * ((nm / c1) / (jnp.sqrt(nv / c2) + ADAM_EPS) + ADAM_WD * w_ref[...])

    row = pl.BlockSpec((tr, C), lambda i: (i, 0))
    out = jax.ShapeDtypeStruct((R, C), F32)
    return pl.pallas_call(
        body, name=name, grid=(R // tr,),
        in_specs=[pl.BlockSpec((N_DEV, tr, C), lambda i: (0, i, 0)), row, row, row],
        out_specs=[row, row, row, row], out_shape=[out, out, out, out],
        compiler_params=_params("parallel"),
    )(parts, w, m, v)


def _pad_lanes(a, width=LANES):
    return jnp.pad(a, ((0, 0), (0, width - a.shape[1])))


def _flat_rows(parts, width=LANES):
    flat = jnp.concatenate([p.reshape(-1) for p in parts])
    rows = -(-flat.shape[0] // width)
    rows = -(-rows // SUBLANES) * SUBLANES
    return jnp.pad(flat, (0, rows * width - flat.shape[0])).reshape(rows, width)


def _unflat(rows2d, shapes):
    flat = rows2d.reshape(-1)
    out, pos = [], 0
    for s in shapes:
        n = int(np.prod(s))
        out.append(flat[pos:pos + n].reshape(s))
        pos += n
    return out


def kernel(x, norm_g, fox_w_in, fox_b_f, fox_w_out, conv_w_in, conv_b_in, conv_dw, conv_dw_b, conv_ln_g, conv_ln_b, conv_w_out, final_norm_g, loss_target, m_norm_g, m_fox_w_in, m_fox_b_f, m_fox_w_out, m_conv_w_in, m_conv_b_in, m_conv_dw, m_conv_dw_b, m_conv_ln_g, m_conv_ln_b, m_conv_w_out, m_final_norm_g, v_norm_g, v_fox_w_in, v_fox_b_f, v_fox_w_out, v_conv_w_in, v_conv_b_in, v_conv_dw, v_conv_dw_b, v_conv_ln_g, v_conv_ln_b, v_conv_w_out, v_final_norm_g):
    h0 = x[0]
    target = loss_target[0]
    S, D = h0.shape
    depth = norm_g.shape[0]
    n_fox, _, fin_shard = fox_w_in.shape
    n_conv, _, cin_shard = conv_w_in.shape
    heads = fox_b_f.shape[1]
    W = fox_w_out.shape[1] * N_DEV
    C = conv_w_out.shape[1] * N_DEV
    assert fin_shard * N_DEV == 4 * W + heads and cin_shard * N_DEV == 3 * C and heads <= LANES
    is_fox = lambda i: i % 2 == 0

    shards = {}
    for i in range(depth):
        j = i // 2
        w_in, w_out = (fox_w_in, fox_w_out) if is_fox(i) else (conv_w_in, conv_w_out)
        shards[i] = (w_in[j].astype(BF16), w_out[j].astype(BF16))
    split = (5 * D) // 8

    def carried(*args, carry=None, **kw):
        r = _matmul(*args, carry=carry, **kw)
        return r if carry is not None else (r, None)

    small_shapes = [conv_b_in.shape, conv_dw.shape, conv_dw_b.shape, conv_ln_g.shape, conv_ln_b.shape]
    small = _flat_rows([conv_b_in, conv_dw, conv_dw_b, conv_ln_g, conv_ln_b])
    small_all = _exchange(small, "ag_small", True)
    b_in_s, dw_s, dwb_s, lng_s, lnb_s = zip(*[_unflat(small_all[d], small_shapes) for d in range(N_DEV)])
    cat = lambda parts, axis: jnp.concatenate(parts, axis=axis)
    conv_b_in_f = cat(b_in_s, 1)
    conv_dw_f = cat(dw_s, 2)
    conv_dw_b_f = cat(dwb_s, 1)
    conv_ln_g_f = cat(lng_s, 1)
    conv_ln_b_f = cat(lnb_s, 1)

    gathered_in = {0: _exchange(shards[0][0], "ag_w_in0", True)}
    gathered_out = {}

    h = h0
    saved = []
    for i in range(depth):
        j = i // 2
        has_next = i + 1 < depth
        gather = lambda x: (x, True) if has_next else None
        hn, hn_t = _rms_fwd(h, norm_g[i:i + 1], f"rms_fwd{i}")
        w = jnp.transpose(gathered_in.pop(i), (1, 0, 2)).reshape(D, -1)
        if is_fox(i):
            w_qkvg, w_f = w[:, :4 * W], _pad_lanes(w[:, 4 * W:])
            qkv, got_in = carried(hn, w_qkvg, "nn", BF16, f"fox_qkv{i}", n_out=3 * W,
                                  carry=gather(shards[i + 1][0] if has_next else None))
            gate, got = carried(hn, w_qkvg, "nn", F32, f"fox_gate{i}", b_col_off=3 * W, n_out=W,
                                carry=(shards[i][1], True))
            w_out = got.reshape(W, D)
            f = _matmul(hn, w_f, "nn", F32, f"fox_f{i}")
            b_f = _pad_lanes(fox_b_f[j:j + 1])
            c = _fgate_fwd(f, b_f, f"fgate_fwd{i}")
            tq = _tile(S, FLASH_TILE)
            cneg = (-c[:, :heads]).T.reshape(heads, S // tq, 1, tq)
            y, o, lse = _flash_fwd(qkv, gate, cneg, heads, f"flash_fwd{i}")
            h_next, got_out = carried(y, w_out, "nn", F32, f"out_proj{i}", add=h,
                                      carry=gather(shards[i + 1][1] if has_next else None))
            if has_next:
                gathered_in[i + 1], gathered_out[i + 1] = got_in, got_out
            saved.append(dict(h=h, hn_t=hn_t, qkv=qkv, gate=gate, f=f, b_f=b_f, cneg=cneg, y=y, o=o, lse=lse,
                              w_qkvg=w_qkvg, w_f=w_f, w_out=w_out))
        else:
            w_out = gathered_out.pop(i).reshape(C, D)
            nxt = shards[i + 1][0] if has_next else None
            proj, got_a = carried(hn, w, "nn", F32, f"conv_in{i}", bias=conv_b_in_f[j:j + 1],
                                  carry=gather(nxt[:split] if has_next else None))
            y, u1 = _conv_fwd(proj, conv_dw_f[j], conv_dw_b_f[j:j + 1], conv_ln_g_f[j:j + 1],
                              conv_ln_b_f[j:j + 1], f"conv_fwd{i}")
            h_next, got_b = carried(y, w_out, "nn", F32, f"out_proj{i}", add=h,
                                    carry=gather(nxt[split:] if has_next else None))
            if has_next:
                gathered_in[i + 1] = jnp.concatenate([got_a, got_b], axis=1)
            saved.append(dict(h=h, hn_t=hn_t, proj=proj, y=y, u1=u1, w_in=w, w_out=w_out))
        h = h_next

    dh, dh16, dh16_t, loss_part, dg_final = _loss_head(h, final_norm_g[None, :], target, "loss_head")

    def shard_cols(g, shard):
        return jnp.transpose(g.reshape(g.shape[0], N_DEV, shard), (1, 0, 2))

    d_norm_g = [None] * depth
    d_fox_b_f = [None] * n_fox
    d_conv_small = [None] * n_conv
    summed_in = [None] * depth
    summed_out = [None] * depth
    pend_in = pend_out = None
    early_in = None
    for i in reversed(range(depth)):
        j = i // 2
        sv = saved[i]
        scatter = lambda x: (x, False) if x is not None else None
        dy = _matmul(dh16, sv["w_out"], "nt", F32, f"d_out_proj{i}")
        dw_out_t, got = carried(dh16_t, sv["y"], "nn", F32, f"dw_out{i}", carry=scatter(pend_out))
        if pend_out is not None:
            summed_out[i + 1] = got
        own_out = dw_out_t.T.reshape(N_DEV, -1, D).astype(BF16)
        if is_fox(i):
            do, dgate, delta = _fox_gate_bwd(dy, sv["o"], sv["gate"], f"fox_gate_bwd{i}")
            dq, dk, dv, colsum, rowsum = _flash_bwd(sv["qkv"], do, delta, sv["lse"], sv["cneg"], heads,
                                                    f"flash_bwd{i}")
            rowsum = jnp.max(rowsum.reshape(S, heads, LANES), axis=2)
            dc = _pad_lanes(rowsum - jnp.sum(colsum, axis=2).reshape(heads, S).T)
            df, dbf = _fgate_bwd(dc, sv["f"], sv["b_f"], f"fgate_bwd{i}")
            dproj = jnp.concatenate([dq, dk, dv, dgate], axis=1)
            dhn = _matmul(df, sv["w_f"], "nt", F32, f"d_fox_f{i}")
            dhn, got = carried(dproj, sv["w_qkvg"], "nt", F32, f"d_fox_in{i}", add=dhn, carry=scatter(pend_in))
            if pend_in is not None:
                summed_in[i + 1] = got
            dw_f = _matmul(sv["hn_t"], df, "nn", F32, f"dw_fox_f{i}")[:, :heads]
            parts_of = lambda dw, rows: shard_cols(jnp.concatenate([dw, dw_f[rows]], axis=1), fin_shard).astype(BF16)
            if i > 0:
                dw_qkvg, summed_out[i] = carried(sv["hn_t"], dproj, "nn", F32, f"dw_fox_in{i}",
                                                 carry=(own_out, False))
                pend_in = parts_of(dw_qkvg, slice(None))
            else:
                top, bottom = slice(0, D // 2), slice(D // 2, D)
                dw_top, summed_out[i] = carried(sv["hn_t"][top], dproj, "nn", F32, f"dw_fox_in{i}",
                                                carry=(own_out, False))
                dw_bottom, early_in = carried(sv["hn_t"][bottom], dproj, "nn", F32, f"dw_fox_in{i}_rest",
                                              carry=(parts_of(dw_top, top), False))
                pend_in = parts_of(dw_bottom, bottom)
            pend_out = None
            d_fox_b_f[j] = jnp.sum(dbf, axis=0)[:heads]
        else:
            du1, dgate, nsums = _conv_bwd_norm(dy, sv["proj"], sv["u1"], conv_ln_g_f[j:j + 1],
                                               conv_ln_b_f[j:j + 1], f"conv_bwd_norm{i}")
            da, db, ddw, absums = _conv_bwd_taps(du1, sv["proj"], conv_dw_f[j], f"conv_bwd_taps{i}")
            dproj = jnp.concatenate([da, db, dgate], axis=1)
            half = D // 2
            dhn, got_a = carried(dproj, sv["w_in"], "nt", F32, f"d_conv_in{i}",
                                 carry=scatter(pend_in[:, :half] if pend_in is not None else None))
            dw_in, got_b = carried(sv["hn_t"], dproj, "nn", F32, f"dw_conv_in{i}",
                                   carry=scatter(pend_in[:, half:] if pend_in is not None else None))
            if pend_in is not None:
                summed_in[i + 1] = jnp.concatenate([got_a, got_b], axis=1)
            pend_in, pend_out = shard_cols(dw_in, cin_shard).astype(BF16), own_out
            nsum = jnp.sum(nsums, axis=1)
            absum = jnp.sum(absums, axis=1)
            d_conv_small[j] = dict(b_in=jnp.concatenate([absum[0], absum[1], nsum[3]]),
                                   dw=jnp.sum(ddw, axis=1), dw_b=nsum[2], ln_g=nsum[0], ln_b=nsum[1])
        dh, dh16, dh16_t, dg = _rms_bwd(sv["h"], norm_g[i:i + 1], dhn, dh, f"rms_bwd{i}")
        d_norm_g[i] = jnp.sum(dg, axis=0)
    summed_in[0] = _exchange(pend_in, "rs_w_in0", False)
    if early_in is not None:
        summed_in[0] = jnp.concatenate([early_in, summed_in[0]], axis=1)
    if pend_out is not None:
        summed_out[0] = _exchange(pend_out, "rs_w_out0", False)
    grad_x = dh[None]

    def small_for(d):
        sl = lambda a, n: a[..., d * n:(d + 1) * n]
        return _flat_rows([
            jnp.stack([sl(s["b_in"], 3 * C // N_DEV) for s in d_conv_small]),
            jnp.stack([sl(s["dw"], C // N_DEV) for s in d_conv_small]),
            jnp.stack([sl(s["dw_b"], C // N_DEV) for s in d_conv_small]),
            jnp.stack([sl(s["ln_g"], C // N_DEV) for s in d_conv_small]),
            jnp.stack([sl(s["ln_b"], C // N_DEV) for s in d_conv_small])])
    r_small = _exchange(jnp.stack([small_for(d) for d in range(N_DEV)]), "rs_small", False)

    rep_shapes = [norm_g.shape, fox_b_f.shape, final_norm_g.shape, (1,)]
    rep_part = _flat_rows([jnp.stack(d_norm_g), jnp.stack(d_fox_b_f), jnp.sum(dg_final, axis=0),
                           jnp.sum(loss_part[:, 0])[None]])
    r_rep = _exchange(rep_part, "ag_replicated", True)

    def update(parts, w, m, v, name):
        two_d = (-1, w.shape[-1])
        stacked = jnp.stack(parts, axis=1).reshape((N_DEV,) + w.reshape(two_d).shape)
        res = _adamw(stacked, w.reshape(two_d), m.reshape(two_d), v.reshape(two_d), name)
        return [r.reshape(w.shape) for r in res]

    fox_layers = [i for i in range(depth) if is_fox(i)]
    conv_layers = [i for i in range(depth) if not is_fox(i)]
    u_fin = update([summed_in[i] for i in fox_layers], fox_w_in, m_fox_w_in, v_fox_w_in, "adamw_fox_w_in")
    u_fout = update([summed_out[i] for i in fox_layers], fox_w_out, m_fox_w_out, v_fox_w_out, "adamw_fox_w_out")
    u_cin = update([summed_in[i] for i in conv_layers], conv_w_in, m_conv_w_in, v_conv_w_in, "adamw_conv_w_in")
    u_cout = update([summed_out[i] for i in conv_layers], conv_w_out, m_conv_w_out, v_conv_w_out,
                    "adamw_conv_w_out")
    u_small = _adamw(r_small, small,
                     _flat_rows([m_conv_b_in, m_conv_dw, m_conv_dw_b, m_conv_ln_g, m_conv_ln_b]),
                     _flat_rows([v_conv_b_in, v_conv_dw, v_conv_dw_b, v_conv_ln_g, v_conv_ln_b]), "adamw_small")
    zero1 = jnp.zeros((1,), F32)
    u_rep = _adamw(r_rep, _flat_rows([norm_g, fox_b_f, final_norm_g, zero1]),
                   _flat_rows([m_norm_g, m_fox_b_f, m_final_norm_g, zero1]),
                   _flat_rows([v_norm_g, v_fox_b_f, v_final_norm_g, zero1]), "adamw_replicated")

    outs = []
    loss = None
    for kind in range(4):
        b_in_k, dw_k, dwb_k, lng_k, lnb_k = _unflat(u_small[kind], small_shapes)
        ng_k, bf_k, fg_k, loss_k = _unflat(u_rep[kind], rep_shapes)
        if kind == 0:
            loss = loss_k[0]
        outs += [ng_k, u_fin[kind], bf_k, u_fout[kind], u_cin[kind], b_in_k, dw_k, dwb_k, lng_k, lnb_k,
                 u_cout[kind], fg_k]
    return (loss, grad_x, *outs)
```

```python
import functools

import numpy as np
import jax
import jax.numpy as jnp
from jax import lax
from jax.experimental import pallas as pl
from jax.experimental.pallas import tpu as pltpu

F32 = jnp.float32
BF16 = jnp.bfloat16
MESH_ID = pl.DeviceIdType.MESH

N_DEV = 8
RMS_EPS = 1e-6
LN_EPS = 1e-5
ADAM_LR = 0.001
ADAM_B1 = 0.9
ADAM_B2 = 0.999
ADAM_EPS = 1e-08
ADAM_WD = 0.01
ADAM_STEP = 10

LANES = 128
SUBLANES = 8
VMEM_LIMIT = 56 * 1024 * 1024
NEG_BIG = -1e30
CONV_HALO = 32
FLASH_FWD_TILE = 1024
FLASH_BWD_TILE = 512
CONV_ROWS = 128
CONV_CHUNK = 32
FGATE_ROWS = 256
ROW_TILE = 256


def _params(*sem):
    return pltpu.CompilerParams(dimension_semantics=sem if sem else None, vmem_limit_bytes=VMEM_LIMIT)


def _tile(n, pref):
    if n <= pref:
        return n
    t = pref
    while n % t:
        t //= 2
    return t


def _sigmoid(x):
    return 1.0 / (1.0 + jnp.exp(-x))


def _dsilu(x, s):
    return s * (1.0 + x * (1.0 - s))


def _rows(n, chunk, fn):
    def step(i, carry):
        fn(pl.multiple_of(i * chunk, chunk))
        return carry
    lax.fori_loop(0, n // chunk, step, 0)


def _peer(k):
    x, y, c = lax.axis_index("x"), lax.axis_index("y"), lax.axis_index("c")
    px = 1 - x if (k >> 2) & 1 else x
    py = 1 - y if (k >> 1) & 1 else y
    pc = 1 - c if k & 1 else c
    return (px, py, pc), 4 * px + 2 * py + pc


def _exchange_copies(x_ref, o_ref, send_sems, recv_sems, local_sem, gather):
    _, me = _peer(0)
    mine = x_ref if gather else x_ref.at[me]
    local = pltpu.make_async_copy(mine, o_ref.at[me], local_sem)
    sends, arrivals = [], []
    for k in range(1, N_DEV):
        peer, pidx = _peer(k)
        sems = dict(send_sem=send_sems.at[k - 1], recv_sem=recv_sems.at[k - 1], device_id=peer,
                    device_id_type=MESH_ID)
        sends.append(pltpu.make_async_remote_copy(src_ref=x_ref if gather else x_ref.at[pidx],
                                                  dst_ref=o_ref.at[me], **sems))
        arrivals.append(pltpu.make_async_remote_copy(src_ref=mine, dst_ref=o_ref.at[pidx], **sems))
    return local, sends, arrivals


def _exchange_start(*refs, gather):
    local, sends, _ = _exchange_copies(*refs, gather)
    local.start()
    for cp in sends:
        cp.start()


def _exchange_wait(*refs, gather):
    local, sends, arrivals = _exchange_copies(*refs, gather)
    for cp in arrivals:
        cp.wait_recv()
    for cp in sends:
        cp.wait_send()
    local.wait()


EXCHANGE_SCRATCH = [pltpu.SemaphoreType.DMA((N_DEV - 1,)), pltpu.SemaphoreType.DMA((N_DEV - 1,)),
                    pltpu.SemaphoreType.DMA]


def _exchange_shape(x, gather):
    return jax.ShapeDtypeStruct((N_DEV,) + x.shape if gather else x.shape, x.dtype)


def _exchange(x, name, gather):
    def body(*refs):
        _exchange_start(*refs, gather=gather)
        _exchange_wait(*refs, gather=gather)

    return pl.pallas_call(
        body, name=name,
        out_shape=_exchange_shape(x, gather),
        in_specs=[pl.BlockSpec(memory_space=pl.ANY)],
        out_specs=pl.BlockSpec(memory_space=pl.ANY),
        scratch_shapes=list(EXCHANGE_SCRATCH),
    )(x)


def _matmul(a, b, mode, out_dtype, name, bias=None, add=None, b_col_off=0, n_out=None, carry=None):
    M, K = a.shape
    N = n_out if n_out is not None else (b.shape[0] if mode == "nt" else b.shape[1])
    tm = _tile(M, 512)
    tn = _tile(N, 1024)
    k_cap = 4096 if a.dtype.itemsize == 2 and b.dtype.itemsize == 2 else 2048
    tk = next(K // d for d in range(1, K + 1) if K % d == 0 and K // d <= k_cap and (K // d) % LANES == 0)
    nm, nn, nk = M // tm, N // tn, K // tk
    assert b_col_off % tn == 0
    joff = b_col_off // tn
    dims = {"nn": (((1,), (0,)), ((), ())), "nt": (((1,), (1,)), ((), ()))}[mode]
    n_in = 2 + (bias is not None) + (add is not None)

    def body(*refs):
        a_ref, b_ref = refs[0], refs[1]
        bias_ref = refs[2] if bias is not None else None
        add_ref = refs[n_in - 1] if add is not None else None
        pos = n_in
        x_ref = o_ref = x_out_ref = None
        if carry is not None:
            x_ref, o_ref, x_out_ref = refs[pos], refs[pos + 1], refs[pos + 2]
            pos += 3
        else:
            o_ref = refs[pos]
            pos += 1
        acc_ref = None
        if nk > 1:
            acc_ref = refs[pos]
            pos += 1
        exchange_refs = (x_ref, x_out_ref) + tuple(refs[pos:])
        i, j, kk = pl.program_id(0), pl.program_id(1), pl.program_id(2)

        if carry is not None:
            @pl.when((i == 0) & (j == 0) & (kk == 0))
            def _():
                _exchange_start(*exchange_refs, gather=carry[1])

        part = lax.dot_general(a_ref[...].astype(BF16), b_ref[...].astype(BF16), dims,
                               preferred_element_type=F32)

        def finish(r):
            if bias_ref is not None:
                r = r + bias_ref[...]
            if add_ref is not None:
                r = r + add_ref[...]
            o_ref[...] = r.astype(o_ref.dtype)

        if nk == 1:
            finish(part)
        else:
            @pl.when(kk == 0)
            def _():
                acc_ref[...] = part

            @pl.when(kk > 0)
            def _():
                acc_ref[...] += part

            @pl.when(kk == nk - 1)
            def _():
                finish(acc_ref[...])

        if carry is not None:
            @pl.when((i == nm - 1) & (j == nn - 1) & (kk == nk - 1))
            def _():
                _exchange_wait(*exchange_refs, gather=carry[1])

    a_spec = pl.BlockSpec((tm, tk), lambda i, j, k: (i, k))
    if mode == "nt":
        b_spec = pl.BlockSpec((tn, tk), lambda i, j, k: (j, k))
    else:
        b_spec = pl.BlockSpec((tk, tn), lambda i, j, k: (k, j + joff))
    in_specs = [a_spec, b_spec]
    args = [a, b]
    if bias is not None:
        in_specs.append(pl.BlockSpec((1, tn), lambda i, j, k: (0, j)))
        args.append(bias)
    if add is not None:
        in_specs.append(pl.BlockSpec((tm, tn), lambda i, j, k: (i, j)))
        args.append(add)
    out_specs = [pl.BlockSpec((tm, tn), lambda i, j, k: (i, j))]
    out_shape = [jax.ShapeDtypeStruct((M, N), out_dtype)]
    scratch = [pltpu.VMEM((tm, tn), F32)] if nk > 1 else []
    if carry is not None:
        in_specs.append(pl.BlockSpec(memory_space=pl.ANY))
        args.append(carry[0])
        out_specs.append(pl.BlockSpec(memory_space=pl.ANY))
        out_shape.append(_exchange_shape(*carry))
        scratch += EXCHANGE_SCRATCH
    res = pl.pallas_call(
        body, name=name,
        grid=(nm, nn, nk),
        in_specs=in_specs, out_specs=out_specs, out_shape=out_shape, scratch_shapes=scratch,
        compiler_params=_params(*(("arbitrary",) * 3 if carry is not None else ("parallel", "parallel", "arbitrary"))),
    )(*args)
    return res if carry is not None else res[0]


def _rms_fwd(h, g, name):
    S, D = h.shape
    ts = _tile(S, 512)

    def body(h_ref, g_ref, o_ref, ot_ref):
        x = h_ref[...]
        r = lax.rsqrt(jnp.mean(x * x, axis=-1, keepdims=True) + RMS_EPS)
        y = x * r * g_ref[...]
        o_ref[...] = y.astype(BF16)
        ot_ref[...] = y.T.astype(BF16)

    return pl.pallas_call(
        body, name=name, grid=(S // ts,),
        in_specs=[pl.BlockSpec((ts, D), lambda i: (i, 0)), pl.BlockSpec((1, D), lambda i: (0, 0))],
        out_specs=[pl.BlockSpec((ts, D), lambda i: (i, 0)), pl.BlockSpec((D, ts), lambda i: (0, i))],
        out_shape=[jax.ShapeDtypeStruct((S, D), BF16), jax.ShapeDtypeStruct((D, S), BF16)],
        compiler_params=_params("parallel"),
    )(h, g)


def _rms_bwd_block(x, g, dy):
    r = lax.rsqrt(jnp.mean(x * x, axis=-1, keepdims=True) + RMS_EPS)
    xr = x * r
    t = dy * g
    dx = r * (t - xr * jnp.mean(t * xr, axis=-1, keepdims=True))
    return dx, dy * xr


def _rms_bwd(h, g, dhn, dh, name):
    S, D = h.shape
    ts = _tile(S, ROW_TILE)

    def body(h_ref, g_ref, dhn_ref, dh_ref, o_ref, o16_ref, o16t_ref, dg_ref):
        dx, dgt = _rms_bwd_block(h_ref[...], g_ref[...], dhn_ref[...])
        out = dh_ref[...] + dx
        o_ref[...] = out
        o16_ref[...] = out.astype(BF16)
        o16t_ref[...] = out.T.astype(BF16)
        part = jnp.sum(dgt.reshape(ts // SUBLANES, SUBLANES, D), axis=0)

        @pl.when(pl.program_id(0) == 0)
        def _():
            dg_ref[...] = part

        @pl.when(pl.program_id(0) > 0)
        def _():
            dg_ref[...] += part

    row = pl.BlockSpec((ts, D), lambda i: (i, 0))
    return pl.pallas_call(
        body, name=name, grid=(S // ts,),
        in_specs=[row, pl.BlockSpec((1, D), lambda i: (0, 0)), row, row],
        out_specs=[row, row, pl.BlockSpec((D, ts), lambda i: (0, i)),
                   pl.BlockSpec((SUBLANES, D), lambda i: (0, 0))],
        out_shape=[jax.ShapeDtypeStruct((S, D), F32), jax.ShapeDtypeStruct((S, D), BF16),
                   jax.ShapeDtypeStruct((D, S), BF16), jax.ShapeDtypeStruct((SUBLANES, D), F32)],
        compiler_params=_params("arbitrary"),
    )(h, g, dhn, dh)


def _loss_head(h, g, target, name):
    S, D = h.shape
    ts = _tile(S, ROW_TILE)

    def body(h_ref, g_ref, t_ref, o_ref, o16_ref, o16t_ref, loss_ref, dg_ref):
        x = h_ref[...]
        gg = g_ref[...]
        r = lax.rsqrt(jnp.mean(x * x, axis=-1, keepdims=True) + RMS_EPS)
        err = x * r * gg - t_ref[...]
        row_loss = 0.5 * jnp.mean(err * err, axis=-1, keepdims=True)
        dx, dgt = _rms_bwd_block(x, gg, err * (1.0 / D))
        o_ref[...] = dx
        o16_ref[...] = dx.astype(BF16)
        o16t_ref[...] = dx.T.astype(BF16)
        part = jnp.sum(dgt.reshape(ts // SUBLANES, SUBLANES, D), axis=0)
        lpart = jnp.sum(jnp.broadcast_to(row_loss, (ts, LANES)).reshape(ts // SUBLANES, SUBLANES, LANES), axis=0)

        @pl.when(pl.program_id(0) == 0)
        def _():
            dg_ref[...] = part
            loss_ref[...] = lpart

        @pl.when(pl.program_id(0) > 0)
        def _():
            dg_ref[...] += part
            loss_ref[...] += lpart

    row = pl.BlockSpec((ts, D), lambda i: (i, 0))
    return pl.pallas_call(
        body, name=name, grid=(S // ts,),
        in_specs=[row, pl.BlockSpec((1, D), lambda i: (0, 0)), row],
        out_specs=[row, row, pl.BlockSpec((D, ts), lambda i: (0, i)),
                   pl.BlockSpec((SUBLANES, LANES), lambda i: (0, 0)), pl.BlockSpec((SUBLANES, D), lambda i: (0, 0))],
        out_shape=[jax.ShapeDtypeStruct((S, D), F32), jax.ShapeDtypeStruct((S, D), BF16),
                   jax.ShapeDtypeStruct((D, S), BF16), jax.ShapeDtypeStruct((SUBLANES, LANES), F32),
                   jax.ShapeDtypeStruct((SUBLANES, D), F32)],
        compiler_params=_params("arbitrary"),
    )(h, g, target)


def _split3(x):
    hi = x.astype(BF16)
    r1 = x - hi.astype(F32)
    mid = r1.astype(BF16)
    lo = (r1 - mid.astype(F32)).astype(BF16)
    return hi, mid, lo


def _tri_sum(tri, x):
    hi, mid, lo = _split3(x)
    dot = functools.partial(jnp.dot, preferred_element_type=F32)
    return dot(tri, hi) + dot(tri, mid) + dot(tri, lo)


def _fgate_fwd(f, b_f, name):
    S = f.shape[0]
    tb = _tile(S, FGATE_ROWS)

    def body(f_ref, b_ref, c_ref, carry_ref):
        @pl.when(pl.program_id(0) == 0)
        def _():
            carry_ref[...] = jnp.zeros_like(carry_ref)

        x = f_ref[...] + b_ref[...]
        lf = jnp.minimum(x, 0.0) - jnp.log1p(jnp.exp(-jnp.abs(x)))
        r = lax.broadcasted_iota(jnp.int32, (tb, tb), 0)
        c = lax.broadcasted_iota(jnp.int32, (tb, tb), 1)
        tri = (c <= r).astype(BF16)
        c_ref[...] = _tri_sum(tri, lf) + carry_ref[0:1, :]
        carry_ref[...] += _tri_sum(jnp.ones((SUBLANES, tb), BF16), lf)

    return pl.pallas_call(
        body, name=name, grid=(S // tb,),
        in_specs=[pl.BlockSpec((tb, LANES), lambda i: (i, 0)), pl.BlockSpec((1, LANES), lambda i: (0, 0))],
        out_specs=pl.BlockSpec((tb, LANES), lambda i: (i, 0)),
        out_shape=jax.ShapeDtypeStruct((S, LANES), F32),
        scratch_shapes=[pltpu.VMEM((SUBLANES, LANES), F32)],
        compiler_params=_params("arbitrary"),
    )(f, b_f)


def _fgate_bwd(dc, f, b_f, name):
    S = f.shape[0]
    tb = _tile(S, FGATE_ROWS)
    nb = S // tb

    def body(dc_ref, f_ref, b_ref, df_ref, db_ref, carry_ref):
        @pl.when(pl.program_id(0) == 0)
        def _():
            carry_ref[...] = jnp.zeros_like(carry_ref)
            db_ref[...] = jnp.zeros_like(db_ref)

        d = dc_ref[...]
        r = lax.broadcasted_iota(jnp.int32, (tb, tb), 0)
        c = lax.broadcasted_iota(jnp.int32, (tb, tb), 1)
        tri = (c >= r).astype(BF16)
        dlf = _tri_sum(tri, d) + carry_ref[0:1, :]
        carry_ref[...] += _tri_sum(jnp.ones((SUBLANES, tb), BF16), d)
        df = dlf * _sigmoid(-(f_ref[...] + b_ref[...]))
        df_ref[...] = df
        db_ref[...] += jnp.sum(df.reshape(tb // SUBLANES, SUBLANES, LANES), axis=0)

    rev = pl.BlockSpec((tb, LANES), lambda i: (nb - 1 - i, 0))
    return pl.pallas_call(
        body, name=name, grid=(nb,),
        in_specs=[rev, rev, pl.BlockSpec((1, LANES), lambda i: (0, 0))],
        out_specs=[rev, pl.BlockSpec((SUBLANES, LANES), lambda i: (0, 0))],
        out_shape=[jax.ShapeDtypeStruct((S, LANES), F32), jax.ShapeDtypeStruct((SUBLANES, LANES), F32)],
        scratch_shapes=[pltpu.VMEM((SUBLANES, LANES), F32)],
        compiler_params=_params("arbitrary"),
    )(dc, f, b_f)


LOG2E = 1.4426950408889634
NT_DIMS = (((1,), (1,)), ((), ()))
TN_DIMS = (((0,), (0,)), ((), ()))


def _causal_mask(t):
    r = lax.broadcasted_iota(jnp.int32, (t, t), 0)
    c = lax.broadcasted_iota(jnp.int32, (t, t), 1)
    return r >= c


def _key_tiles(qi, tile, group):
    def several(i, carry):
        for u in range(group):
            tile(group * i + u, False)
        return carry
    lax.fori_loop(0, qi // group, several, 0)

    def single(j, carry):
        tile(j, False)
        return carry
    lax.fori_loop((qi // group) * group, qi, single, 0)
    tile(qi, True)


def _flash_fwd(qkv, gate, cneg, heads, name):
    S, W3 = qkv.shape
    W = W3 // 3
    dh = W // heads
    assert dh == LANES
    tq = _tile(S, FLASH_FWD_TILE)
    nq = S // tq
    c1 = dh ** -0.5 * LOG2E

    def body(q_ref, k_ref, v_ref, b_ref, g_ref, y_ref, o_ref, lse_ref, m_scr, l_scr, acc_scr):
        qi = pl.program_id(1)
        m_scr[...] = jnp.full_like(m_scr, NEG_BIG)
        l_scr[...] = jnp.zeros_like(l_scr)
        acc_scr[...] = jnp.zeros_like(acc_scr)
        q = q_ref[...]

        def tile(j, diagonal):
            rows = pl.ds(pl.multiple_of(j * tq, tq), tq)
            t = lax.dot_general(q, k_ref[rows, :], NT_DIMS, preferred_element_type=F32) * c1 + b_ref[0, j] * LOG2E
            if diagonal:
                t = jnp.where(_causal_mask(tq), t, NEG_BIG)
            m_prev = m_scr[...]
            m_next = jnp.maximum(m_prev, jnp.max(t, axis=1, keepdims=True))
            alpha = jnp.exp2(m_prev - m_next)
            p = jnp.exp2(t - m_next[:, :1])
            l_scr[...] = alpha * l_scr[...] + jnp.sum(p, axis=1, keepdims=True)
            acc_scr[...] = alpha * acc_scr[...] + jnp.dot(p.astype(BF16), v_ref[rows, :],
                                                          preferred_element_type=F32)
            m_scr[...] = m_next

        _key_tiles(qi, tile, 4)
        l = l_scr[...]
        o = acc_scr[...] / l
        g = g_ref[...]
        o_ref[...] = o
        lse_ref[...] = m_scr[...] + jnp.log2(l)
        y_ref[...] = (o * (g * _sigmoid(g))).astype(BF16)

    qblk = pl.BlockSpec((tq, dh), lambda h, i: (i, h))
    return pl.pallas_call(
        body, name=name, grid=(heads, nq),
        in_specs=[qblk,
                  pl.BlockSpec((S, dh), lambda h, i: (0, heads + h)),
                  pl.BlockSpec((S, dh), lambda h, i: (0, 2 * heads + h)),
                  pl.BlockSpec((1, nq, 1, tq), lambda h, i: (h, 0, 0, 0)),
                  qblk],
        out_specs=[qblk, qblk, qblk],
        out_shape=[jax.ShapeDtypeStruct((S, W), BF16), jax.ShapeDtypeStruct((S, W), F32),
                   jax.ShapeDtypeStruct((S, W), F32)],
        scratch_shapes=[pltpu.VMEM((tq, LANES), F32), pltpu.VMEM((tq, LANES), F32), pltpu.VMEM((tq, dh), F32)],
        compiler_params=_params("parallel", "arbitrary"),
    )(qkv, qkv, qkv, cneg, gate)


def _flash_bwd(qkv, do, delta, lse, cneg, heads, name):
    S, W3 = qkv.shape
    W = W3 // 3
    dh = W // heads
    tq = _tile(S, FLASH_BWD_TILE)
    nq = S // tq
    scale = dh ** -0.5
    c1 = scale * LOG2E

    def body(q_ref, k_ref, v_ref, b_ref, do_ref, delta_ref, lse_ref,
             dq_ref, dk_ref, dv_ref, db_ref, rs_ref, dq_scr, dk_scr, dv_scr, rs_scr):
        h, qi = pl.program_id(0), pl.program_id(1)

        @pl.when((h == 0) & (qi == 0))
        def _():
            rs_ref[...] = jnp.zeros_like(rs_ref)

        @pl.when(qi == 0)
        def _():
            dk_scr[...] = jnp.zeros_like(dk_scr)
            dv_scr[...] = jnp.zeros_like(dv_scr)
            db_ref[...] = jnp.zeros_like(db_ref)

        dq_scr[...] = jnp.zeros_like(dq_scr)
        rs_scr[...] = jnp.zeros_like(rs_scr)
        q, d_o = q_ref[...], do_ref[...]
        lse = lse_ref[...][:, :1]
        delta = delta_ref[...][:, :1]

        def tile(j, diagonal):
            rows = pl.ds(pl.multiple_of(j * tq, tq), tq)
            k, v = k_ref[rows, :], v_ref[rows, :]
            t = lax.dot_general(q, k, NT_DIMS, preferred_element_type=F32) * c1 + b_ref[0, j] * LOG2E
            p = jnp.exp2(t - lse)
            if diagonal:
                p = jnp.where(_causal_mask(tq), p, 0.0)
            dp = lax.dot_general(d_o, v, NT_DIMS, preferred_element_type=F32)
            ds = p * (dp - delta)
            dv_scr[rows, :] += lax.dot_general(p.astype(BF16), d_o, TN_DIMS, preferred_element_type=F32)
            db_ref[0, j] += jnp.sum(ds.reshape(tq // SUBLANES, SUBLANES, tq), axis=0)
            dsb = (ds * scale).astype(BF16)
            dk_scr[rows, :] += lax.dot_general(dsb, q, TN_DIMS, preferred_element_type=F32)
            dq_scr[...] += jnp.dot(dsb, k, preferred_element_type=F32)
            rs_scr[...] += jnp.sum(ds, axis=1, keepdims=True)

        _key_tiles(qi, tile, 4)
        dq_ref[...] = dq_scr[...].astype(BF16)
        lane = lax.broadcasted_iota(jnp.int32, (tq, LANES), 1)
        rs_ref[pl.ds(pl.multiple_of(qi * tq, tq), tq), :] += jnp.where(lane == h, rs_scr[...], 0.0)

        @pl.when(qi == nq - 1)
        def _():
            dk_ref[...] = dk_scr[...].astype(BF16)
            dv_ref[...] = dv_scr[...].astype(BF16)

    qblk = pl.BlockSpec((tq, dh), lambda h, i: (i, h))
    head = pl.BlockSpec((S, dh), lambda h, i: (0, h))
    return pl.pallas_call(
        body, name=name, grid=(heads, nq),
        in_specs=[qblk,
                  pl.BlockSpec((S, dh), lambda h, i: (0, heads + h)),
                  pl.BlockSpec((S, dh), lambda h, i: (0, 2 * heads + h)),
                  pl.BlockSpec((1, nq, 1, tq), lambda h, i: (h, 0, 0, 0)),
                  qblk, qblk, qblk],
        out_specs=[qblk, head, head,
                   pl.BlockSpec((1, nq, SUBLANES, tq), lambda h, i: (h, 0, 0, 0)),
                   pl.BlockSpec((S, LANES), lambda h, i: (0, 0))],
        out_shape=[jax.ShapeDtypeStruct((S, W), BF16), jax.ShapeDtypeStruct((S, W), BF16),
                   jax.ShapeDtypeStruct((S, W), BF16), jax.ShapeDtypeStruct((heads, nq, SUBLANES, tq), F32),
                   jax.ShapeDtypeStruct((S, LANES), F32)],
        scratch_shapes=[pltpu.VMEM((tq, dh), F32), pltpu.VMEM((S, dh), F32), pltpu.VMEM((S, dh), F32),
                        pltpu.VMEM((tq, LANES), F32)],
        compiler_params=_params("arbitrary", "arbitrary"),
    )(qkv, qkv, qkv, cneg, do, delta, lse)


def _fox_gate_bwd(dy, o, gate, name):
    S, W = dy.shape
    ts = _tile(S, ROW_TILE)

    def body(dy_ref, o_ref, g_ref, do_ref, dg_ref, delta_ref):
        d, g, o_val = dy_ref[...], g_ref[...], o_ref[...]
        sg = _sigmoid(g)
        d_o = (d * (g * sg)).astype(BF16)
        do_ref[...] = d_o
        dg_ref[...] = (d * o_val * _dsilu(g, sg)).astype(BF16)
        prod = d_o.astype(F32) * o_val
        for h in range(W // LANES):
            cols = slice(h * LANES, (h + 1) * LANES)
            delta_ref[:, cols] = jnp.broadcast_to(jnp.sum(prod[:, cols], axis=1, keepdims=True), (ts, LANES))

    row = pl.BlockSpec((ts, W), lambda i: (i, 0))
    return pl.pallas_call(
        body, name=name, grid=(S // ts,),
        in_specs=[row, row, row], out_specs=[row, row, row],
        out_shape=[jax.ShapeDtypeStruct((S, W), BF16), jax.ShapeDtypeStruct((S, W), BF16),
                   jax.ShapeDtypeStruct((S, W), F32)],
        compiler_params=_params("parallel"),
    )(dy, o, gate)


def _shifted_copies(buf_ref, sh_ref, rows):
    for j in range(1, SUBLANES):
        sh_ref[j, 0:rows, :] = buf_ref[j:j + rows, :]


def _tap(buf_ref, sh_ref, r0, off, cols):
    j, base = off % SUBLANES, off - off % SUBLANES
    if j == 0:
        return buf_ref[pl.ds(r0 + base, SUBLANES), cols]
    return sh_ref[j, pl.ds(r0 + base, SUBLANES), cols]


def _tap_weights(dw_ref, cols):
    return [jnp.broadcast_to(dw_ref[k:k + 1, cols], (SUBLANES, LANES)) for k in range(dw_ref.shape[0])]


def _conv_fwd(proj, dw, dw_b, ln_g, ln_b, name):
    S, C3 = proj.shape
    C = C3 // 3
    K = dw.shape[0]
    assert K - 1 <= CONV_HALO - 2
    ts = _tile(S, CONV_ROWS)
    hb = ts // CONV_HALO
    nrows = ts + CONV_HALO
    lead = CONV_HALO - (K - 1)

    def body(a_ref, b_ref, ah_ref, bh_ref, g_ref, dw_ref, dwb_ref, lg_ref, lb_ref, y_ref, u1_ref, buf, sh):
        first = pl.program_id(0) == 0
        buf[0:CONV_HALO, :] = jnp.where(first, 0.0, ah_ref[...] * _sigmoid(bh_ref[...]))

        def glu(r0):
            rows = pl.ds(r0, CONV_CHUNK)
            buf[pl.ds(r0 + CONV_HALO, CONV_CHUNK), :] = a_ref[rows, :] * _sigmoid(b_ref[rows, :])
        _rows(ts, CONV_CHUNK, glu)
        _shifted_copies(buf, sh, nrows - SUBLANES)

        for s in range(C // LANES):
            cols = slice(s * LANES, (s + 1) * LANES)
            w = _tap_weights(dw_ref, cols)
            bias = jnp.broadcast_to(dwb_ref[:, cols], (SUBLANES, LANES))

            def taps(r0, w=w, bias=bias, cols=cols):
                for u in range(CONV_CHUNK // SUBLANES):
                    r = r0 + u * SUBLANES
                    acc = bias
                    for k in range(K):
                        acc = acc + w[k] * _tap(buf, sh, r, lead + k, cols)
                    u1_ref[pl.ds(r, SUBLANES), cols] = acc
            _rows(ts, CONV_CHUNK, taps)

        def norm(r0):
            rows = pl.ds(r0, CONV_CHUNK)
            u1 = u1_ref[rows, :]
            mu = jnp.mean(u1, axis=-1, keepdims=True)
            xc = u1 - mu
            rstd = lax.rsqrt(jnp.mean(xc * xc, axis=-1, keepdims=True) + LN_EPS)
            z = xc * rstd * lg_ref[...] + lb_ref[...]
            g = g_ref[rows, :]
            y_ref[rows, :] = ((z * _sigmoid(z)) * (g * _sigmoid(g))).astype(BF16)
        _rows(ts, CONV_CHUNK, norm)

    row = lambda col: pl.BlockSpec((ts, C), lambda i: (i, col))
    halo = lambda col: pl.BlockSpec((CONV_HALO, C), lambda i: (jnp.maximum(i * hb - 1, 0), col))
    vec = pl.BlockSpec((1, C), lambda i: (0, 0))
    return pl.pallas_call(
        body, name=name, grid=(S // ts,),
        in_specs=[row(0), row(1), halo(0), halo(1), row(2), pl.BlockSpec((K, C), lambda i: (0, 0)), vec, vec, vec],
        out_specs=[row(0), row(0)],
        out_shape=[jax.ShapeDtypeStruct((S, C), BF16), jax.ShapeDtypeStruct((S, C), F32)],
        scratch_shapes=[pltpu.VMEM((nrows, C), F32), pltpu.VMEM((SUBLANES, nrows, C), F32)],
        compiler_params=_params("parallel"),
    )(proj, proj, proj, proj, proj, dw, dw_b, ln_g, ln_b)


def _conv_bwd_norm(dy, proj, u1, ln_g, ln_b, name):
    S, C = dy.shape
    ts = _tile(S, ROW_TILE)
    groups = CONV_CHUNK // SUBLANES

    def fold(x):
        return jnp.sum(x.reshape(groups, SUBLANES, C), axis=0)

    def body(dy_ref, g_ref, u1_ref, lg_ref, lb_ref, du1_ref, dg_ref, sums_ref):
        @pl.when(pl.program_id(0) == 0)
        def _():
            sums_ref[...] = jnp.zeros_like(sums_ref)

        def chunk(r0):
            rows = pl.ds(r0, CONV_CHUNK)
            d, g, u1 = dy_ref[rows, :], g_ref[rows, :], u1_ref[rows, :]
            mu = jnp.mean(u1, axis=-1, keepdims=True)
            xc = u1 - mu
            rstd = lax.rsqrt(jnp.mean(xc * xc, axis=-1, keepdims=True) + LN_EPS)
            xh = xc * rstd
            z = xh * lg_ref[...] + lb_ref[...]
            sz, sg = _sigmoid(z), _sigmoid(g)
            dgate = d * (z * sz) * _dsilu(g, sg)
            dz = d * (g * sg) * _dsilu(z, sz)
            dxh = dz * lg_ref[...]
            du1 = rstd * (dxh - jnp.mean(dxh, axis=-1, keepdims=True)
                          - xh * jnp.mean(dxh * xh, axis=-1, keepdims=True))
            du1_ref[rows, :] = du1
            dg_ref[rows, :] = dgate.astype(BF16)
            sums_ref[0] += fold(dz * xh)
            sums_ref[1] += fold(dz)
            sums_ref[2] += fold(du1)
            sums_ref[3] += fold(dgate)
        _rows(ts, CONV_CHUNK, chunk)

    row = pl.BlockSpec((ts, C), lambda i: (i, 0))
    vec = pl.BlockSpec((1, C), lambda i: (0, 0))
    return pl.pallas_call(
        body, name=name, grid=(S // ts,),
        in_specs=[row, pl.BlockSpec((ts, C), lambda i: (i, 2)), row, vec, vec],
        out_specs=[row, row, pl.BlockSpec((4, SUBLANES, C), lambda i: (0, 0, 0))],
        out_shape=[jax.ShapeDtypeStruct((S, C), F32), jax.ShapeDtypeStruct((S, C), BF16),
                   jax.ShapeDtypeStruct((4, SUBLANES, C), F32)],
        compiler_params=_params("arbitrary"),
    )(dy, proj, u1, ln_g, ln_b)


def _conv_bwd_taps(du1, proj, dw, name):
    S, C = du1.shape
    K = dw.shape[0]
    ts = _tile(S, CONV_ROWS)
    hb = ts // CONV_HALO
    nblk = S // ts
    last_halo = S // CONV_HALO - 1
    nrows = ts + CONV_HALO
    groups = CONV_CHUNK // SUBLANES

    def body(d_ref, dn_ref, a_ref, b_ref, dw_ref, da_ref, db_ref, ddw_ref, sums_ref, dbuf, dsh):
        i = pl.program_id(0)

        @pl.when(i == 0)
        def _():
            ddw_ref[...] = jnp.zeros_like(ddw_ref)
            sums_ref[...] = jnp.zeros_like(sums_ref)

        dbuf[ts:nrows, :] = jnp.where(i == nblk - 1, 0.0, dn_ref[...])

        def fill(r0):
            rows = pl.ds(r0, CONV_CHUNK)
            dbuf[rows, :] = d_ref[rows, :]
        _rows(ts, CONV_CHUNK, fill)
        _shifted_copies(dbuf, dsh, nrows - SUBLANES)

        zero = jnp.zeros((SUBLANES, LANES), F32)
        fold = lambda x: jnp.sum(x.reshape(groups, SUBLANES, LANES), axis=0)
        for s in range(C // LANES):
            cols = slice(s * LANES, (s + 1) * LANES)
            w = _tap_weights(dw_ref, cols)

            def step(j, carry, w=w, cols=cols):
                r0 = pl.multiple_of(j * CONV_CHUNK, CONV_CHUNK)
                rows = pl.ds(r0, CONV_CHUNK)
                a, sb = a_ref[rows, cols], _sigmoid(b_ref[rows, cols])
                u0 = a * sb
                accs = list(carry[2:])
                parts = []
                for u in range(groups):
                    u0_u = u0[u * SUBLANES:(u + 1) * SUBLANES]
                    acc = zero
                    for k in range(K):
                        x = _tap(dbuf, dsh, r0 + u * SUBLANES, K - 1 - k, cols)
                        acc = acc + w[k] * x
                        accs[k] = accs[k] + u0_u * x
                    parts.append(acc)
                da = jnp.concatenate(parts, axis=0) * sb
                db = da * a * (1.0 - sb)
                da_ref[rows, cols] = da.astype(BF16)
                db_ref[rows, cols] = db.astype(BF16)
                return (carry[0] + fold(da), carry[1] + fold(db), *accs)
            out = lax.fori_loop(0, ts // CONV_CHUNK, step, (zero,) * (K + 2))
            sums_ref[0, :, cols] += out[0]
            sums_ref[1, :, cols] += out[1]
            for k in range(K):
                ddw_ref[k, :, cols] += out[2 + k]

    row = lambda col: pl.BlockSpec((ts, C), lambda i: (i, col))
    nxt = pl.BlockSpec((CONV_HALO, C), lambda i: (jnp.minimum((i + 1) * hb, last_halo), 0))
    return pl.pallas_call(
        body, name=name, grid=(nblk,),
        in_specs=[row(0), nxt, row(0), row(1), pl.BlockSpec((K, C), lambda i: (0, 0))],
        out_specs=[row(0), row(0), pl.BlockSpec((K, SUBLANES, C), lambda i: (0, 0, 0)),
                   pl.BlockSpec((2, SUBLANES, C), lambda i: (0, 0, 0))],
        out_shape=[jax.ShapeDtypeStruct((S, C), BF16), jax.ShapeDtypeStruct((S, C), BF16),
                   jax.ShapeDtypeStruct((K, SUBLANES, C), F32), jax.ShapeDtypeStruct((2, SUBLANES, C), F32)],
        scratch_shapes=[pltpu.VMEM((nrows, C), F32), pltpu.VMEM((SUBLANES, nrows, C), F32)],
        compiler_params=_params("arbitrary"),
    )(du1, du1, proj, proj, dw)


def _adamw(parts, w, m, v, name):
    R, C = w.shape
    tr = _tile(R, 256)
    c1 = 1.0 - ADAM_B1 ** ADAM_STEP
    c2 = 1.0 - ADAM_B2 ** ADAM_STEP

    def body(p_ref, w_ref, m_ref, v_ref, g_ref, d_ref, nm_ref, nv_ref):
        g = p_ref[0].astype(F32)
        for d in range(1, N_DEV):
            g = g + p_ref[d].astype(F32)
        nm = ADAM_B1 * m_ref[...] + (1.0 - ADAM_B1) * g
        nv = ADAM_B2 * v_ref[...] + (1.0 - ADAM_B2) * (g * g)
        g_ref[...] = g
        nm_ref[...] = nm
        nv_ref[...] = nv
        d_ref[...] = -ADAM_LR * ((nm / c1) / (jnp.sqrt(nv / c2) + ADAM_EPS) + ADAM_WD * w_ref[...])

    row = pl.BlockSpec((tr, C), lambda i: (i, 0))
    out = jax.ShapeDtypeStruct((R, C), F32)
    return pl.pallas_call(
        body, name=name, grid=(R // tr,),
        in_specs=[pl.BlockSpec((N_DEV, tr, C), lambda i: (0, i, 0)), row, row, row],
        out_specs=[row, row, row, row], out_shape=[out, out, out, out],
        compiler_params=_params("parallel"),
    )(parts, w, m, v)


def _pad_lanes(a, width=LANES):
    return jnp.pad(a, ((0, 0), (0, width - a.shape[1])))


def _flat_rows(parts, width=LANES):
    flat = jnp.concatenate([p.reshape(-1) for p in parts])
    rows = -(-flat.shape[0] // width)
    rows = -(-rows // SUBLANES) * SUBLANES
    return jnp.pad(flat, (0, rows * width - flat.shape[0])).reshape(rows, width)


def _unflat(rows2d, shapes):
    flat = rows2d.reshape(-1)
    out, pos = [], 0
    for s in shapes:
        n = int(np.prod(s))
        out.append(flat[pos:pos + n].reshape(s))
        pos += n
    return out


def kernel(x, norm_g, fox_w_in, fox_b_f, fox_w_out, conv_w_in, conv_b_in, conv_dw, conv_dw_b, conv_ln_g, conv_ln_b, conv_w_out, final_norm_g, loss_target, m_norm_g, m_fox_w_in, m_fox_b_f, m_fox_w_out, m_conv_w_in, m_conv_b_in, m_conv_dw, m_conv_dw_b, m_conv_ln_g, m_conv_ln_b, m_conv_w_out, m_final_norm_g, v_norm_g, v_fox_w_in, v_fox_b_f, v_fox_w_out, v_conv_w_in, v_conv_b_in, v_conv_dw, v_conv_dw_b, v_conv_ln_g, v_conv_ln_b, v_conv_w_out, v_final_norm_g):
    h0 = x[0]
    target = loss_target[0]
    S, D = h0.shape
    depth = norm_g.shape[0]
    n_fox, _, fin_shard = fox_w_in.shape
    n_conv, _, cin_shard = conv_w_in.shape
    heads = fox_b_f.shape[1]
    W = fox_w_out.shape[1] * N_DEV
    C = conv_w_out.shape[1] * N_DEV
    assert fin_shard * N_DEV == 4 * W + heads and cin_shard * N_DEV == 3 * C and heads <= LANES
    is_fox = lambda i: i % 2 == 0

    shards = {}
    for i in range(depth):
        j = i // 2
        w_in, w_out = (fox_w_in, fox_w_out) if is_fox(i) else (conv_w_in, conv_w_out)
        shards[i] = (w_in[j].astype(BF16), w_out[j].astype(BF16))
    split = (5 * D) // 8

    def carried(*args, carry=None, **kw):
        r = _matmul(*args, carry=carry, **kw)
        return r if carry is not None else (r, None)

    small_shapes = [conv_b_in.shape, conv_dw.shape, conv_dw_b.shape, conv_ln_g.shape, conv_ln_b.shape]
    small = _flat_rows([conv_b_in, conv_dw, conv_dw_b, conv_ln_g, conv_ln_b])
    small_all = _exchange(small, "ag_small", True)
    b_in_s, dw_s, dwb_s, lng_s, lnb_s = zip(*[_unflat(small_all[d], small_shapes) for d in range(N_DEV)])
    cat = lambda parts, axis: jnp.concatenate(parts, axis=axis)
    conv_b_in_f = cat(b_in_s, 1)
    conv_dw_f = cat(dw_s, 2)
    conv_dw_b_f = cat(dwb_s, 1)
    conv_ln_g_f = cat(lng_s, 1)
    conv_ln_b_f = cat(lnb_s, 1)

    gathered_in = {0: _exchange(shards[0][0], "ag_w_in0", True)}
    gathered_out = {}

    h = h0
    saved = []
    for i in range(depth):
        j = i // 2
        has_next = i + 1 < depth
        gather = lambda x: (x, True) if has_next else None
        hn, hn_t = _rms_fwd(h, norm_g[i:i + 1], f"rms_fwd{i}")
        w = jnp.transpose(gathered_in.pop(i), (1, 0, 2)).reshape(D, -1)
        if is_fox(i):
            w_qkvg, w_f = w[:, :4 * W], _pad_lanes(w[:, 4 * W:])
            qkv, got_in = carried(hn, w_qkvg, "nn", BF16, f"fox_qkv{i}", n_out=3 * W,
                                  carry=gather(shards[i + 1][0] if has_next else None))
            gate, got = carried(hn, w_qkvg, "nn", F32, f"fox_gate{i}", b_col_off=3 * W, n_out=W,
                                carry=(shards[i][1], True))
            w_out = got.reshape(W, D)
            f = _matmul(hn, w_f, "nn", F32, f"fox_f{i}")
            b_f = _pad_lanes(fox_b_f[j:j + 1])
            c = _fgate_fwd(f, b_f, f"fgate_fwd{i}")
            key_bias = lambda t: (-c[:, :heads]).T.reshape(heads, S // t, 1, t)
            y, o, lse = _flash_fwd(qkv, gate, key_bias(_tile(S, FLASH_FWD_TILE)), heads, f"flash_fwd{i}")
            cneg = key_bias(_tile(S, FLASH_BWD_TILE))
            h_next, got_out = carried(y, w_out, "nn", F32, f"out_proj{i}", add=h,
                                      carry=gather(shards[i + 1][1] if has_next else None))
            if has_next:
                gathered_in[i + 1], gathered_out[i + 1] = got_in, got_out
            saved.append(dict(h=h, hn_t=hn_t, qkv=qkv, gate=gate, f=f, b_f=b_f, cneg=cneg, y=y, o=o, lse=lse,
                              w_qkvg=w_qkvg, w_f=w_f, w_out=w_out))
        else:
            w_out = gathered_out.pop(i).reshape(C, D)
            nxt = shards[i + 1][0] if has_next else None
            proj, got_a = carried(hn, w, "nn", F32, f"conv_in{i}", bias=conv_b_in_f[j:j + 1],
                                  carry=gather(nxt[:split] if has_next else None))
            y, u1 = _conv_fwd(proj, conv_dw_f[j], conv_dw_b_f[j:j + 1], conv_ln_g_f[j:j + 1],
                              conv_ln_b_f[j:j + 1], f"conv_fwd{i}")
            h_next, got_b = carried(y, w_out, "nn", F32, f"out_proj{i}", add=h,
                                    carry=gather(nxt[split:] if has_next else None))
            if has_next:
                gathered_in[i + 1] = jnp.concatenate([got_a, got_b], axis=1)
            saved.append(dict(h=h, hn_t=hn_t, proj=proj, y=y, u1=u1, w_in=w, w_out=w_out))
        h = h_next

    dh, dh16, dh16_t, loss_part, dg_final = _loss_head(h, final_norm_g[None, :], target, "loss_head")

    def shard_cols(g, shard):
        return jnp.transpose(g.reshape(g.shape[0], N_DEV, shard), (1, 0, 2))

    d_norm_g = [None] * depth
    d_fox_b_f = [None] * n_fox
    d_conv_small = [None] * n_conv
    summed_in = [None] * depth
    summed_out = [None] * depth
    pend_in = pend_out = None
    early_in = None
    for i in reversed(range(depth)):
        j = i // 2
        sv = saved[i]
        scatter = lambda x: (x, False) if x is not None else None
        dy = _matmul(dh16, sv["w_out"], "nt", F32, f"d_out_proj{i}")
        dw_out_t, got = carried(dh16_t, sv["y"], "nn", F32, f"dw_out{i}", carry=scatter(pend_out))
        if pend_out is not None:
            summed_out[i + 1] = got
        own_out = dw_out_t.T.reshape(N_DEV, -1, D).astype(BF16)
        if is_fox(i):
            do, dgate, delta = _fox_gate_bwd(dy, sv["o"], sv["gate"], f"fox_gate_bwd{i}")
            dq, dk, dv, colsum, rowsum = _flash_bwd(sv["qkv"], do, delta, sv["lse"], sv["cneg"], heads,
                                                    f"flash_bwd{i}")
            dc = _pad_lanes(rowsum[:, :heads] - jnp.sum(colsum, axis=2).reshape(heads, S).T)
            df, dbf = _fgate_bwd(dc, sv["f"], sv["b_f"], f"fgate_bwd{i}")
            dproj = jnp.concatenate([dq, dk, dv, dgate], axis=1)
            dhn = _matmul(df, sv["w_f"], "nt", F32, f"d_fox_f{i}")
            dhn, got = carried(dproj, sv["w_qkvg"], "nt", F32, f"d_fox_in{i}", add=dhn, carry=scatter(pend_in))
            if pend_in is not None:
                summed_in[i + 1] = got
            dw_f = _matmul(sv["hn_t"], df, "nn", F32, f"dw_fox_f{i}")[:, :heads]
            parts_of = lambda dw, rows: shard_cols(jnp.concatenate([dw, dw_f[rows]], axis=1), fin_shard).astype(BF16)
            if i > 0:
                dw_qkvg, summed_out[i] = carried(sv["hn_t"], dproj, "nn", F32, f"dw_fox_in{i}",
                                                 carry=(own_out, False))
                pend_in = parts_of(dw_qkvg, slice(None))
            else:
                top, bottom = slice(0, D // 2), slice(D // 2, D)
                dw_top, summed_out[i] = carried(sv["hn_t"][top], dproj, "nn", F32, f"dw_fox_in{i}",
                                                carry=(own_out, False))
                dw_bottom, early_in = carried(sv["hn_t"][bottom], dproj, "nn", F32, f"dw_fox_in{i}_rest",
                                              carry=(parts_of(dw_top, top), False))
                pend_in = parts_of(dw_bottom, bottom)
            pend_out = None
            d_fox_b_f[j] = jnp.sum(dbf, axis=0)[:heads]
        else:
            du1, dgate, nsums = _conv_bwd_norm(dy, sv["proj"], sv["u1"], conv_ln_g_f[j:j + 1],
                                               conv_ln_b_f[j:j + 1], f"conv_bwd_norm{i}")
            da, db, ddw, absums = _conv_bwd_taps(du1, sv["proj"], conv_dw_f[j], f"conv_bwd_taps{i}")
            dproj = jnp.concatenate([da, db, dgate], axis=1)
            half = D // 2
            dhn, got_a = carried(dproj, sv["w_in"], "nt", F32, f"d_conv_in{i}",
                                 carry=scatter(pend_in[:, :half] if pend_in is not None else None))
            dw_in, got_b = carried(sv["hn_t"], dproj, "nn", F32, f"dw_conv_in{i}",
                                   carry=scatter(pend_in[:, half:] if pend_in is not None else None))
            if pend_in is not None:
                summed_in[i + 1] = jnp.concatenate([got_a, got_b], axis=1)
            pend_in, pend_out = shard_cols(dw_in, cin_shard).astype(BF16), own_out
            nsum = jnp.sum(nsums, axis=1)
            absum = jnp.sum(absums, axis=1)
            d_conv_small[j] = dict(b_in=jnp.concatenate([absum[0], absum[1], nsum[3]]),
                                   dw=jnp.sum(ddw, axis=1), dw_b=nsum[2], ln_g=nsum[0], ln_b=nsum[1])
        dh, dh16, dh16_t, dg = _rms_bwd(sv["h"], norm_g[i:i + 1], dhn, dh, f"rms_bwd{i}")
        d_norm_g[i] = jnp.sum(dg, axis=0)
    summed_in[0] = _exchange(pend_in, "rs_w_in0", False)
    if early_in is not None:
        summed_in[0] = jnp.concatenate([early_in, summed_in[0]], axis=1)
    if pend_out is not None:
        summed_out[0] = _exchange(pend_out, "rs_w_out0", False)
    grad_x = dh[None]

    def small_for(d):
        sl = lambda a, n: a[..., d * n:(d + 1) * n]
        return _flat_rows([
            jnp.stack([sl(s["b_in"], 3 * C // N_DEV) for s in d_conv_small]),
            jnp.stack([sl(s["dw"], C // N_DEV) for s in d_conv_small]),
            jnp.stack([sl(s["dw_b"], C // N_DEV) for s in d_conv_small]),
            jnp.stack([sl(s["ln_g"], C // N_DEV) for s in d_conv_small]),
            jnp.stack([sl(s["ln_b"], C // N_DEV) for s in d_conv_small])])
    r_small = _exchange(jnp.stack([small_for(d) for d in range(N_DEV)]), "rs_small", False)

    rep_shapes = [norm_g.shape, fox_b_f.shape, final_norm_g.shape, (1,)]
    rep_part = _flat_rows([jnp.stack(d_norm_g), jnp.stack(d_fox_b_f), jnp.sum(dg_final, axis=0),
                           jnp.sum(loss_part[:, 0])[None]])
    r_rep = _exchange(rep_part, "ag_replicated", True)

    def update(parts, w, m, v, name):
        two_d = (-1, w.shape[-1])
        stacked = jnp.stack(parts, axis=1).reshape((N_DEV,) + w.reshape(two_d).shape)
        res = _adamw(stacked, w.reshape(two_d), m.reshape(two_d), v.reshape(two_d), name)
        return [r.reshape(w.shape) for r in res]

    fox_layers = [i for i in range(depth) if is_fox(i)]
    conv_layers = [i for i in range(depth) if not is_fox(i)]
    u_fin = update([summed_in[i] for i in fox_layers], fox_w_in, m_fox_w_in, v_fox_w_in, "adamw_fox_w_in")
    u_fout = update([summed_out[i] for i in fox_layers], fox_w_out, m_fox_w_out, v_fox_w_out, "adamw_fox_w_out")
    u_cin = update([summed_in[i] for i in conv_layers], conv_w_in, m_conv_w_in, v_conv_w_in, "adamw_conv_w_in")
    u_cout = update([summed_out[i] for i in conv_layers], conv_w_out, m_conv_w_out, v_conv_w_out,
                    "adamw_conv_w_out")
    u_small = _adamw(r_small, small,
                     _flat_rows([m_conv_b_in, m_conv_dw, m_conv_dw_b, m_conv_ln_g, m_conv_ln_b]),
                     _flat_rows([v_conv_b_in, v_conv_dw, v_conv_dw_b, v_conv_ln_g, v_conv_ln_b]), "adamw_small")
    zero1 = jnp.zeros((1,), F32)
    u_rep = _adamw(r_rep, _flat_rows([norm_g, fox_b_f, final_norm_g, zero1]),
                   _flat_rows([m_norm_g, m_fox_b_f, m_final_norm_g, zero1]),
                   _flat_rows([v_norm_g, v_fox_b_f, v_final_norm_g, zero1]), "adamw_replicated")

    outs = []
    loss = None
    for kind in range(4):
        b_in_k, dw_k, dwb_k, lng_k, lnb_k = _unflat(u_small[kind], small_shapes)
        ng_k, bf_k, fg_k, loss_k = _unflat(u_rep[kind], rep_shapes)
        if kind == 0:
            loss = loss_k[0]
        outs += [ng_k, u_fin[kind], bf_k, u_fout[kind], u_cin[kind], b_in_k, dw_k, dwb_k, lng_k, lnb_k,
                 u_cout[kind], fg_k]
    return (loss, grad_x, *outs)
```

```python
import functools

import numpy as np
import jax
import jax.numpy as jnp
from jax import lax
from jax.experimental import pallas as pl
from jax.experimental.pallas import tpu as pltpu

F32 = jnp.float32
BF16 = jnp.bfloat16
MESH_ID = pl.DeviceIdType.MESH

N_DEV = 8
RMS_EPS = 1e-6
LN_EPS = 1e-5
ADAM_LR = 0.001
ADAM_B1 = 0.9
ADAM_B2 = 0.999
ADAM_EPS = 1e-08
ADAM_WD = 0.01
ADAM_STEP = 10

LANES = 128
SUBLANES = 8
VMEM_LIMIT = 56 * 1024 * 1024
NEG_BIG = -1e30
CONV_HALO = 32
FLASH_FWD_TILE = 1024
FLASH_BWD_TILE = 512
CONV_ROWS = 128
CONV_CHUNK = 32
FGATE_ROWS = 256
ROW_TILE = 256


def _params(*sem):
    return pltpu.CompilerParams(dimension_semantics=sem if sem else None, vmem_limit_bytes=VMEM_LIMIT)


def _tile(n, pref):
    if n <= pref:
        return n
    t = pref
    while n % t:
        t //= 2
    return t


def _sigmoid(x):
    return 1.0 / (1.0 + jnp.exp(-x))


def _dsilu(x, s):
    return s * (1.0 + x * (1.0 - s))


def _rows(n, chunk, fn):
    def step(i, carry):
        fn(pl.multiple_of(i * chunk, chunk))
        return carry
    lax.fori_loop(0, n // chunk, step, 0)


def _peer(k):
    x, y, c = lax.axis_index("x"), lax.axis_index("y"), lax.axis_index("c")
    px = 1 - x if (k >> 2) & 1 else x
    py = 1 - y if (k >> 1) & 1 else y
    pc = 1 - c if k & 1 else c
    return (px, py, pc), 4 * px + 2 * py + pc


def _exchange_copies(x_ref, o_ref, send_sems, recv_sems, local_sem, kind):
    _, me = _peer(0)

    def window(ref, idx, n):
        assert n % LANES == 0
        return ref.at[:, pl.ds(pl.multiple_of(idx * n, LANES), n)]

    def src(idx):
        if kind == "scatter":
            return x_ref.at[idx]
        return window(x_ref, idx, o_ref.shape[2]) if kind == "scatter_cols" else x_ref

    def dst(idx):
        return window(o_ref, idx, x_ref.shape[1]) if kind == "gather_cols" else o_ref.at[idx]

    local = pltpu.make_async_copy(src(me), dst(me), local_sem)
    sends, arrivals = [], []
    for k in range(1, N_DEV):
        peer, pidx = _peer(k)
        sems = dict(send_sem=send_sems.at[k - 1], recv_sem=recv_sems.at[k - 1], device_id=peer,
                    device_id_type=MESH_ID)
        sends.append(pltpu.make_async_remote_copy(src_ref=src(pidx), dst_ref=dst(me), **sems))
        arrivals.append(pltpu.make_async_remote_copy(src_ref=src(me), dst_ref=dst(pidx), **sems))
    return local, sends, arrivals


def _exchange_start(*refs, kind):
    local, sends, _ = _exchange_copies(*refs, kind)
    local.start()
    for cp in sends:
        cp.start()


def _exchange_wait(*refs, kind):
    local, sends, arrivals = _exchange_copies(*refs, kind)
    for cp in arrivals:
        cp.wait_recv()
    for cp in sends:
        cp.wait_send()
    local.wait()


EXCHANGE_SCRATCH = [pltpu.SemaphoreType.DMA((N_DEV - 1,)), pltpu.SemaphoreType.DMA((N_DEV - 1,)),
                    pltpu.SemaphoreType.DMA]


def _exchange_shape(x, kind):
    shape = {"gather": lambda: (N_DEV,) + x.shape, "scatter": lambda: x.shape,
             "gather_cols": lambda: (x.shape[0], N_DEV * x.shape[1]),
             "scatter_cols": lambda: (N_DEV, x.shape[0], x.shape[1] // N_DEV)}[kind]()
    return jax.ShapeDtypeStruct(shape, x.dtype)


def _exchange(x, name, kind):
    def body(*refs):
        _exchange_start(*refs, kind=kind)
        _exchange_wait(*refs, kind=kind)

    return pl.pallas_call(
        body, name=name,
        out_shape=_exchange_shape(x, kind),
        in_specs=[pl.BlockSpec(memory_space=pl.ANY)],
        out_specs=pl.BlockSpec(memory_space=pl.ANY),
        scratch_shapes=list(EXCHANGE_SCRATCH),
    )(x)


def _matmul(a, b, mode, out_dtype, name, bias=None, add=None, b_col_off=0, n_out=None, carry=None):
    M, K = a.shape
    N = n_out if n_out is not None else (b.shape[0] if mode == "nt" else b.shape[1])
    tm = _tile(M, 512)
    tn = _tile(N, 1024)
    k_cap = 4096 if a.dtype.itemsize == 2 and b.dtype.itemsize == 2 else 2048
    tk = next(K // d for d in range(1, K + 1) if K % d == 0 and K // d <= k_cap and (K // d) % LANES == 0)
    nm, nn, nk = M // tm, N // tn, K // tk
    assert b_col_off % tn == 0
    joff = b_col_off // tn
    dims = {"nn": (((1,), (0,)), ((), ())), "nt": (((1,), (1,)), ((), ()))}[mode]
    n_in = 2 + (bias is not None) + (add is not None)

    def body(*refs):
        a_ref, b_ref = refs[0], refs[1]
        bias_ref = refs[2] if bias is not None else None
        add_ref = refs[n_in - 1] if add is not None else None
        pos = n_in
        x_ref = o_ref = x_out_ref = None
        if carry is not None:
            x_ref, o_ref, x_out_ref = refs[pos], refs[pos + 1], refs[pos + 2]
            pos += 3
        else:
            o_ref = refs[pos]
            pos += 1
        acc_ref = None
        if nk > 1:
            acc_ref = refs[pos]
            pos += 1
        exchange_refs = (x_ref, x_out_ref) + tuple(refs[pos:])
        i, j, kk = pl.program_id(0), pl.program_id(1), pl.program_id(2)

        if carry is not None:
            @pl.when((i == 0) & (j == 0) & (kk == 0))
            def _():
                _exchange_start(*exchange_refs, kind=carry[1])

        part = lax.dot_general(a_ref[...].astype(BF16), b_ref[...].astype(BF16), dims,
                               preferred_element_type=F32)

        def finish(r):
            if bias_ref is not None:
                r = r + bias_ref[...]
            if add_ref is not None:
                r = r + add_ref[...]
            o_ref[...] = r.astype(o_ref.dtype)

        if nk == 1:
            finish(part)
        else:
            @pl.when(kk == 0)
            def _():
                acc_ref[...] = part

            @pl.when(kk > 0)
            def _():
                acc_ref[...] += part

            @pl.when(kk == nk - 1)
            def _():
                finish(acc_ref[...])

        if carry is not None:
            @pl.when((i == nm - 1) & (j == nn - 1) & (kk == nk - 1))
            def _():
                _exchange_wait(*exchange_refs, kind=carry[1])

    a_spec = pl.BlockSpec((tm, tk), lambda i, j, k: (i, k))
    if mode == "nt":
        b_spec = pl.BlockSpec((tn, tk), lambda i, j, k: (j, k))
    else:
        b_spec = pl.BlockSpec((tk, tn), lambda i, j, k: (k, j + joff))
    in_specs = [a_spec, b_spec]
    args = [a, b]
    if bias is not None:
        in_specs.append(pl.BlockSpec((1, tn), lambda i, j, k: (0, j)))
        args.append(bias)
    if add is not None:
        in_specs.append(pl.BlockSpec((tm, tn), lambda i, j, k: (i, j)))
        args.append(add)
    out_specs = [pl.BlockSpec((tm, tn), lambda i, j, k: (i, j))]
    out_shape = [jax.ShapeDtypeStruct((M, N), out_dtype)]
    scratch = [pltpu.VMEM((tm, tn), F32)] if nk > 1 else []
    if carry is not None:
        in_specs.append(pl.BlockSpec(memory_space=pl.ANY))
        args.append(carry[0])
        out_specs.append(pl.BlockSpec(memory_space=pl.ANY))
        out_shape.append(_exchange_shape(*carry))
        scratch += EXCHANGE_SCRATCH
    res = pl.pallas_call(
        body, name=name,
        grid=(nm, nn, nk),
        in_specs=in_specs, out_specs=out_specs, out_shape=out_shape, scratch_shapes=scratch,
        compiler_params=_params(*(("arbitrary",) * 3 if carry is not None else ("parallel", "parallel", "arbitrary"))),
    )(*args)
    return res if carry is not None else res[0]


def _rms_fwd(h, g, name):
    S, D = h.shape
    ts = _tile(S, 512)

    def body(h_ref, g_ref, o_ref, ot_ref):
        x = h_ref[...]
        r = lax.rsqrt(jnp.mean(x * x, axis=-1, keepdims=True) + RMS_EPS)
        y = x * r * g_ref[...]
        o_ref[...] = y.astype(BF16)
        ot_ref[...] = y.T.astype(BF16)

    return pl.pallas_call(
        body, name=name, grid=(S // ts,),
        in_specs=[pl.BlockSpec((ts, D), lambda i: (i, 0)), pl.BlockSpec((1, D), lambda i: (0, 0))],
        out_specs=[pl.BlockSpec((ts, D), lambda i: (i, 0)), pl.BlockSpec((D, ts), lambda i: (0, i))],
        out_shape=[jax.ShapeDtypeStruct((S, D), BF16), jax.ShapeDtypeStruct((D, S), BF16)],
        compiler_params=_params("parallel"),
    )(h, g)


def _rms_bwd_block(x, g, dy):
    r = lax.rsqrt(jnp.mean(x * x, axis=-1, keepdims=True) + RMS_EPS)
    xr = x * r
    t = dy * g
    dx = r * (t - xr * jnp.mean(t * xr, axis=-1, keepdims=True))
    return dx, dy * xr


def _rms_bwd(h, g, dhn, dh, name):
    S, D = h.shape
    ts = _tile(S, ROW_TILE)

    def body(h_ref, g_ref, dhn_ref, dh_ref, o_ref, o16_ref, o16t_ref, dg_ref):
        dx, dgt = _rms_bwd_block(h_ref[...], g_ref[...], dhn_ref[...])
        out = dh_ref[...] + dx
        o_ref[...] = out
        o16_ref[...] = out.astype(BF16)
        o16t_ref[...] = out.T.astype(BF16)
        part = jnp.sum(dgt.reshape(ts // SUBLANES, SUBLANES, D), axis=0)

        @pl.when(pl.program_id(0) == 0)
        def _():
            dg_ref[...] = part

        @pl.when(pl.program_id(0) > 0)
        def _():
            dg_ref[...] += part

    row = pl.BlockSpec((ts, D), lambda i: (i, 0))
    return pl.pallas_call(
        body, name=name, grid=(S // ts,),
        in_specs=[row, pl.BlockSpec((1, D), lambda i: (0, 0)), row, row],
        out_specs=[row, row, pl.BlockSpec((D, ts), lambda i: (0, i)),
                   pl.BlockSpec((SUBLANES, D), lambda i: (0, 0))],
        out_shape=[jax.ShapeDtypeStruct((S, D), F32), jax.ShapeDtypeStruct((S, D), BF16),
                   jax.ShapeDtypeStruct((D, S), BF16), jax.ShapeDtypeStruct((SUBLANES, D), F32)],
        compiler_params=_params("arbitrary"),
    )(h, g, dhn, dh)


def _loss_head(h, g, target, name):
    S, D = h.shape
    ts = _tile(S, ROW_TILE)

    def body(h_ref, g_ref, t_ref, o_ref, o16_ref, o16t_ref, loss_ref, dg_ref):
        x = h_ref[...]
        gg = g_ref[...]
        r = lax.rsqrt(jnp.mean(x * x, axis=-1, keepdims=True) + RMS_EPS)
        err = x * r * gg - t_ref[...]
        row_loss = 0.5 * jnp.mean(err * err, axis=-1, keepdims=True)
        dx, dgt = _rms_bwd_block(x, gg, err * (1.0 / D))
        o_ref[...] = dx
        o16_ref[...] = dx.astype(BF16)
        o16t_ref[...] = dx.T.astype(BF16)
        part = jnp.sum(dgt.reshape(ts // SUBLANES, SUBLANES, D), axis=0)
        lpart = jnp.sum(jnp.broadcast_to(row_loss, (ts, LANES)).reshape(ts // SUBLANES, SUBLANES, LANES), axis=0)

        @pl.when(pl.program_id(0) == 0)
        def _():
            dg_ref[...] = part
            loss_ref[...] = lpart

        @pl.when(pl.program_id(0) > 0)
        def _():
            dg_ref[...] += part
            loss_ref[...] += lpart

    row = pl.BlockSpec((ts, D), lambda i: (i, 0))
    return pl.pallas_call(
        body, name=name, grid=(S // ts,),
        in_specs=[row, pl.BlockSpec((1, D), lambda i: (0, 0)), row],
        out_specs=[row, row, pl.BlockSpec((D, ts), lambda i: (0, i)),
                   pl.BlockSpec((SUBLANES, LANES), lambda i: (0, 0)), pl.BlockSpec((SUBLANES, D), lambda i: (0, 0))],
        out_shape=[jax.ShapeDtypeStruct((S, D), F32), jax.ShapeDtypeStruct((S, D), BF16),
                   jax.ShapeDtypeStruct((D, S), BF16), jax.ShapeDtypeStruct((SUBLANES, LANES), F32),
                   jax.ShapeDtypeStruct((SUBLANES, D), F32)],
        compiler_params=_params("arbitrary"),
    )(h, g, target)


def _split3(x):
    hi = x.astype(BF16)
    r1 = x - hi.astype(F32)
    mid = r1.astype(BF16)
    lo = (r1 - mid.astype(F32)).astype(BF16)
    return hi, mid, lo


def _tri_sum(tri, x):
    hi, mid, lo = _split3(x)
    dot = functools.partial(jnp.dot, preferred_element_type=F32)
    return dot(tri, hi) + dot(tri, mid) + dot(tri, lo)


def _fgate_fwd(f, b_f, name):
    S = f.shape[0]
    tb = _tile(S, FGATE_ROWS)

    def body(f_ref, b_ref, c_ref, carry_ref):
        @pl.when(pl.program_id(0) == 0)
        def _():
            carry_ref[...] = jnp.zeros_like(carry_ref)

        x = f_ref[...] + b_ref[...]
        lf = jnp.minimum(x, 0.0) - jnp.log1p(jnp.exp(-jnp.abs(x)))
        r = lax.broadcasted_iota(jnp.int32, (tb, tb), 0)
        c = lax.broadcasted_iota(jnp.int32, (tb, tb), 1)
        tri = (c <= r).astype(BF16)
        c_ref[...] = _tri_sum(tri, lf) + carry_ref[0:1, :]
        carry_ref[...] += _tri_sum(jnp.ones((SUBLANES, tb), BF16), lf)

    return pl.pallas_call(
        body, name=name, grid=(S // tb,),
        in_specs=[pl.BlockSpec((tb, LANES), lambda i: (i, 0)), pl.BlockSpec((1, LANES), lambda i: (0, 0))],
        out_specs=pl.BlockSpec((tb, LANES), lambda i: (i, 0)),
        out_shape=jax.ShapeDtypeStruct((S, LANES), F32),
        scratch_shapes=[pltpu.VMEM((SUBLANES, LANES), F32)],
        compiler_params=_params("arbitrary"),
    )(f, b_f)


def _fgate_bwd(dc, f, b_f, name):
    S = f.shape[0]
    tb = _tile(S, FGATE_ROWS)
    nb = S // tb

    def body(dc_ref, f_ref, b_ref, df_ref, db_ref, carry_ref):
        @pl.when(pl.program_id(0) == 0)
        def _():
            carry_ref[...] = jnp.zeros_like(carry_ref)
            db_ref[...] = jnp.zeros_like(db_ref)

        d = dc_ref[...]
        r = lax.broadcasted_iota(jnp.int32, (tb, tb), 0)
        c = lax.broadcasted_iota(jnp.int32, (tb, tb), 1)
        tri = (c >= r).astype(BF16)
        dlf = _tri_sum(tri, d) + carry_ref[0:1, :]
        carry_ref[...] += _tri_sum(jnp.ones((SUBLANES, tb), BF16), d)
        df = dlf * _sigmoid(-(f_ref[...] + b_ref[...]))
        df_ref[...] = df
        db_ref[...] += jnp.sum(df.reshape(tb // SUBLANES, SUBLANES, LANES), axis=0)

    rev = pl.BlockSpec((tb, LANES), lambda i: (nb - 1 - i, 0))
    return pl.pallas_call(
        body, name=name, grid=(nb,),
        in_specs=[rev, rev, pl.BlockSpec((1, LANES), lambda i: (0, 0))],
        out_specs=[rev, pl.BlockSpec((SUBLANES, LANES), lambda i: (0, 0))],
        out_shape=[jax.ShapeDtypeStruct((S, LANES), F32), jax.ShapeDtypeStruct((SUBLANES, LANES), F32)],
        scratch_shapes=[pltpu.VMEM((SUBLANES, LANES), F32)],
        compiler_params=_params("arbitrary"),
    )(dc, f, b_f)


LOG2E = 1.4426950408889634
NT_DIMS = (((1,), (1,)), ((), ()))
TN_DIMS = (((0,), (0,)), ((), ()))


def _causal_mask(t):
    r = lax.broadcasted_iota(jnp.int32, (t, t), 0)
    c = lax.broadcasted_iota(jnp.int32, (t, t), 1)
    return r >= c


def _key_tiles(qi, tile, group):
    def several(i, carry):
        for u in range(group):
            tile(group * i + u, False)
        return carry
    lax.fori_loop(0, qi // group, several, 0)

    def single(j, carry):
        tile(j, False)
        return carry
    lax.fori_loop((qi // group) * group, qi, single, 0)
    tile(qi, True)


def _flash_fwd(qkv, gate, cneg, heads, name):
    S, W3 = qkv.shape
    W = W3 // 3
    dh = W // heads
    assert dh == LANES
    tq = _tile(S, FLASH_FWD_TILE)
    nq = S // tq
    c1 = dh ** -0.5 * LOG2E

    def body(q_ref, k_ref, v_ref, b_ref, g_ref, y_ref, o_ref, lse_ref, m_scr, l_scr, acc_scr):
        qi = pl.program_id(1)
        m_scr[...] = jnp.full_like(m_scr, NEG_BIG)
        l_scr[...] = jnp.zeros_like(l_scr)
        acc_scr[...] = jnp.zeros_like(acc_scr)
        q = q_ref[...]

        def tile(j, diagonal):
            rows = pl.ds(pl.multiple_of(j * tq, tq), tq)
            t = lax.dot_general(q, k_ref[rows, :], NT_DIMS, preferred_element_type=F32) * c1 + b_ref[0, j] * LOG2E
            if diagonal:
                t = jnp.where(_causal_mask(tq), t, NEG_BIG)
            m_prev = m_scr[...]
            m_next = jnp.maximum(m_prev, jnp.max(t, axis=1, keepdims=True))
            alpha = jnp.exp2(m_prev - m_next)
            p = jnp.exp2(t - m_next[:, :1])
            l_scr[...] = alpha * l_scr[...] + jnp.sum(p, axis=1, keepdims=True)
            acc_scr[...] = alpha * acc_scr[...] + jnp.dot(p.astype(BF16), v_ref[rows, :],
                                                          preferred_element_type=F32)
            m_scr[...] = m_next

        _key_tiles(qi, tile, 4)
        l = l_scr[...]
        o = acc_scr[...] / l
        g = g_ref[...]
        o_ref[...] = o
        lse_ref[...] = m_scr[...] + jnp.log2(l)
        y_ref[...] = (o * (g * _sigmoid(g))).astype(BF16)

    qblk = pl.BlockSpec((tq, dh), lambda h, i: (i, h))
    return pl.pallas_call(
        body, name=name, grid=(heads, nq),
        in_specs=[qblk,
                  pl.BlockSpec((S, dh), lambda h, i: (0, heads + h)),
                  pl.BlockSpec((S, dh), lambda h, i: (0, 2 * heads + h)),
                  pl.BlockSpec((1, nq, 1, tq), lambda h, i: (h, 0, 0, 0)),
                  qblk],
        out_specs=[qblk, qblk, qblk],
        out_shape=[jax.ShapeDtypeStruct((S, W), BF16), jax.ShapeDtypeStruct((S, W), F32),
                   jax.ShapeDtypeStruct((S, W), F32)],
        scratch_shapes=[pltpu.VMEM((tq, LANES), F32), pltpu.VMEM((tq, LANES), F32), pltpu.VMEM((tq, dh), F32)],
        compiler_params=_params("parallel", "arbitrary"),
    )(qkv, qkv, qkv, cneg, gate)


def _flash_bwd(qkv, do, delta, lse, cneg, heads, name):
    S, W3 = qkv.shape
    W = W3 // 3
    dh = W // heads
    tq = _tile(S, FLASH_BWD_TILE)
    nq = S // tq
    scale = dh ** -0.5
    c1 = scale * LOG2E

    def body(q_ref, k_ref, v_ref, b_ref, do_ref, delta_ref, lse_ref,
             dq_ref, dk_ref, dv_ref, db_ref, rs_ref, dq_scr, dk_scr, dv_scr, rs_scr):
        h, qi = pl.program_id(0), pl.program_id(1)

        @pl.when((h == 0) & (qi == 0))
        def _():
            rs_ref[...] = jnp.zeros_like(rs_ref)

        @pl.when(qi == 0)
        def _():
            dk_scr[...] = jnp.zeros_like(dk_scr)
            dv_scr[...] = jnp.zeros_like(dv_scr)
            db_ref[...] = jnp.zeros_like(db_ref)

        dq_scr[...] = jnp.zeros_like(dq_scr)
        rs_scr[...] = jnp.zeros_like(rs_scr)
        q, d_o = q_ref[...], do_ref[...]
        lse = lse_ref[...][:, :1]
        delta = delta_ref[...][:, :1]

        def tile(j, diagonal):
            rows = pl.ds(pl.multiple_of(j * tq, tq), tq)
            k, v = k_ref[rows, :], v_ref[rows, :]
            t = lax.dot_general(q, k, NT_DIMS, preferred_element_type=F32) * c1 + b_ref[0, j] * LOG2E
            p = jnp.exp2(t - lse)
            if diagonal:
                p = jnp.where(_causal_mask(tq), p, 0.0)
            dp = lax.dot_general(d_o, v, NT_DIMS, preferred_element_type=F32)
            ds = p * (dp - delta)
            dv_scr[rows, :] += lax.dot_general(p.astype(BF16), d_o, TN_DIMS, preferred_element_type=F32)
            db_ref[0, j] += jnp.sum(ds.reshape(tq // SUBLANES, SUBLANES, tq), axis=0)
            dsb = (ds * scale).astype(BF16)
            dk_scr[rows, :] += lax.dot_general(dsb, q, TN_DIMS, preferred_element_type=F32)
            dq_scr[...] += jnp.dot(dsb, k, preferred_element_type=F32)
            rs_scr[...] += jnp.sum(ds, axis=1, keepdims=True)

        _key_tiles(qi, tile, 4)
        dq_ref[...] = dq_scr[...].astype(BF16)
        lane = lax.broadcasted_iota(jnp.int32, (tq, LANES), 1)
        rs_ref[pl.ds(pl.multiple_of(qi * tq, tq), tq), :] += jnp.where(lane == h, rs_scr[...], 0.0)

        @pl.when(qi == nq - 1)
        def _():
            dk_ref[...] = dk_scr[...].astype(BF16)
            dv_ref[...] = dv_scr[...].astype(BF16)

    qblk = pl.BlockSpec((tq, dh), lambda h, i: (i, h))
    head = pl.BlockSpec((S, dh), lambda h, i: (0, h))
    return pl.pallas_call(
        body, name=name, grid=(heads, nq),
        in_specs=[qblk,
                  pl.BlockSpec((S, dh), lambda h, i: (0, heads + h)),
                  pl.BlockSpec((S, dh), lambda h, i: (0, 2 * heads + h)),
                  pl.BlockSpec((1, nq, 1, tq), lambda h, i: (h, 0, 0, 0)),
                  qblk, qblk, qblk],
        out_specs=[qblk, head, head,
                   pl.BlockSpec((1, nq, SUBLANES, tq), lambda h, i: (h, 0, 0, 0)),
                   pl.BlockSpec((S, LANES), lambda h, i: (0, 0))],
        out_shape=[jax.ShapeDtypeStruct((S, W), BF16), jax.ShapeDtypeStruct((S, W), BF16),
                   jax.ShapeDtypeStruct((S, W), BF16), jax.ShapeDtypeStruct((heads, nq, SUBLANES, tq), F32),
                   jax.ShapeDtypeStruct((S, LANES), F32)],
        scratch_shapes=[pltpu.VMEM((tq, dh), F32), pltpu.VMEM((S, dh), F32), pltpu.VMEM((S, dh), F32),
                        pltpu.VMEM((tq, LANES), F32)],
        compiler_params=_params("arbitrary", "arbitrary"),
    )(qkv, qkv, qkv, cneg, do, delta, lse)


def _fox_gate_bwd(dy, o, gate, name):
    S, W = dy.shape
    ts = _tile(S, ROW_TILE)

    def body(dy_ref, o_ref, g_ref, do_ref, dg_ref, delta_ref):
        d, g, o_val = dy_ref[...], g_ref[...], o_ref[...]
        sg = _sigmoid(g)
        d_o = (d * (g * sg)).astype(BF16)
        do_ref[...] = d_o
        dg_ref[...] = (d * o_val * _dsilu(g, sg)).astype(BF16)
        prod = d_o.astype(F32) * o_val
        for h in range(W // LANES):
            cols = slice(h * LANES, (h + 1) * LANES)
            delta_ref[:, cols] = jnp.broadcast_to(jnp.sum(prod[:, cols], axis=1, keepdims=True), (ts, LANES))

    row = pl.BlockSpec((ts, W), lambda i: (i, 0))
    return pl.pallas_call(
        body, name=name, grid=(S // ts,),
        in_specs=[row, row, row], out_specs=[row, row, row],
        out_shape=[jax.ShapeDtypeStruct((S, W), BF16), jax.ShapeDtypeStruct((S, W), BF16),
                   jax.ShapeDtypeStruct((S, W), F32)],
        compiler_params=_params("parallel"),
    )(dy, o, gate)


def _shifted_copies(buf_ref, sh_ref, rows):
    for j in range(1, SUBLANES):
        sh_ref[j, 0:rows, :] = buf_ref[j:j + rows, :]


def _tap(buf_ref, sh_ref, r0, off, cols):
    j, base = off % SUBLANES, off - off % SUBLANES
    if j == 0:
        return buf_ref[pl.ds(r0 + base, SUBLANES), cols]
    return sh_ref[j, pl.ds(r0 + base, SUBLANES), cols]


def _tap_weights(dw_ref, cols):
    return [jnp.broadcast_to(dw_ref[k:k + 1, cols], (SUBLANES, LANES)) for k in range(dw_ref.shape[0])]


def _conv_fwd(proj, dw, dw_b, ln_g, ln_b, name):
    S, C3 = proj.shape
    C = C3 // 3
    K = dw.shape[0]
    assert K - 1 <= CONV_HALO - 2
    ts = _tile(S, CONV_ROWS)
    hb = ts // CONV_HALO
    nrows = ts + CONV_HALO
    lead = CONV_HALO - (K - 1)

    def body(a_ref, b_ref, ah_ref, bh_ref, g_ref, dw_ref, dwb_ref, lg_ref, lb_ref, y_ref, u1_ref, buf, sh):
        first = pl.program_id(0) == 0
        buf[0:CONV_HALO, :] = jnp.where(first, 0.0, ah_ref[...] * _sigmoid(bh_ref[...]))

        def glu(r0):
            rows = pl.ds(r0, CONV_CHUNK)
            buf[pl.ds(r0 + CONV_HALO, CONV_CHUNK), :] = a_ref[rows, :] * _sigmoid(b_ref[rows, :])
        _rows(ts, CONV_CHUNK, glu)
        _shifted_copies(buf, sh, nrows - SUBLANES)

        for s in range(C // LANES):
            cols = slice(s * LANES, (s + 1) * LANES)
            w = _tap_weights(dw_ref, cols)
            bias = jnp.broadcast_to(dwb_ref[:, cols], (SUBLANES, LANES))

            def taps(r0, w=w, bias=bias, cols=cols):
                for u in range(CONV_CHUNK // SUBLANES):
                    r = r0 + u * SUBLANES
                    acc = bias
                    for k in range(K):
                        acc = acc + w[k] * _tap(buf, sh, r, lead + k, cols)
                    u1_ref[pl.ds(r, SUBLANES), cols] = acc
            _rows(ts, CONV_CHUNK, taps)

        def norm(r0):
            rows = pl.ds(r0, CONV_CHUNK)
            u1 = u1_ref[rows, :]
            mu = jnp.mean(u1, axis=-1, keepdims=True)
            xc = u1 - mu
            rstd = lax.rsqrt(jnp.mean(xc * xc, axis=-1, keepdims=True) + LN_EPS)
            z = xc * rstd * lg_ref[...] + lb_ref[...]
            g = g_ref[rows, :]
            y_ref[rows, :] = ((z * _sigmoid(z)) * (g * _sigmoid(g))).astype(BF16)
        _rows(ts, CONV_CHUNK, norm)

    row = lambda col: pl.BlockSpec((ts, C), lambda i: (i, col))
    halo = lambda col: pl.BlockSpec((CONV_HALO, C), lambda i: (jnp.maximum(i * hb - 1, 0), col))
    vec = pl.BlockSpec((1, C), lambda i: (0, 0))
    return pl.pallas_call(
        body, name=name, grid=(S // ts,),
        in_specs=[row(0), row(1), halo(0), halo(1), row(2), pl.BlockSpec((K, C), lambda i: (0, 0)), vec, vec, vec],
        out_specs=[row(0), row(0)],
        out_shape=[jax.ShapeDtypeStruct((S, C), BF16), jax.ShapeDtypeStruct((S, C), F32)],
        scratch_shapes=[pltpu.VMEM((nrows, C), F32), pltpu.VMEM((SUBLANES, nrows, C), F32)],
        compiler_params=_params("parallel"),
    )(proj, proj, proj, proj, proj, dw, dw_b, ln_g, ln_b)


def _conv_bwd_norm(dy, proj, u1, ln_g, ln_b, name):
    S, C = dy.shape
    ts = _tile(S, ROW_TILE)
    groups = CONV_CHUNK // SUBLANES

    def fold(x):
        return jnp.sum(x.reshape(groups, SUBLANES, C), axis=0)

    def body(dy_ref, g_ref, u1_ref, lg_ref, lb_ref, du1_ref, dg_ref, sums_ref):
        @pl.when(pl.program_id(0) == 0)
        def _():
            sums_ref[...] = jnp.zeros_like(sums_ref)

        def chunk(r0):
            rows = pl.ds(r0, CONV_CHUNK)
            d, g, u1 = dy_ref[rows, :], g_ref[rows, :], u1_ref[rows, :]
            mu = jnp.mean(u1, axis=-1, keepdims=True)
            xc = u1 - mu
            rstd = lax.rsqrt(jnp.mean(xc * xc, axis=-1, keepdims=True) + LN_EPS)
            xh = xc * rstd
            z = xh * lg_ref[...] + lb_ref[...]
            sz, sg = _sigmoid(z), _sigmoid(g)
            dgate = d * (z * sz) * _dsilu(g, sg)
            dz = d * (g * sg) * _dsilu(z, sz)
            dxh = dz * lg_ref[...]
            du1 = rstd * (dxh - jnp.mean(dxh, axis=-1, keepdims=True)
                          - xh * jnp.mean(dxh * xh, axis=-1, keepdims=True))
            du1_ref[rows, :] = du1
            dg_ref[rows, :] = dgate.astype(BF16)
            sums_ref[0] += fold(dz * xh)
            sums_ref[1] += fold(dz)
            sums_ref[2] += fold(du1)
            sums_ref[3] += fold(dgate)
        _rows(ts, CONV_CHUNK, chunk)

    row = pl.BlockSpec((ts, C), lambda i: (i, 0))
    vec = pl.BlockSpec((1, C), lambda i: (0, 0))
    return pl.pallas_call(
        body, name=name, grid=(S // ts,),
        in_specs=[row, pl.BlockSpec((ts, C), lambda i: (i, 2)), row, vec, vec],
        out_specs=[row, row, pl.BlockSpec((4, SUBLANES, C), lambda i: (0, 0, 0))],
        out_shape=[jax.ShapeDtypeStruct((S, C), F32), jax.ShapeDtypeStruct((S, C), BF16),
                   jax.ShapeDtypeStruct((4, SUBLANES, C), F32)],
        compiler_params=_params("arbitrary"),
    )(dy, proj, u1, ln_g, ln_b)


def _conv_bwd_taps(du1, proj, dw, name):
    S, C = du1.shape
    K = dw.shape[0]
    ts = _tile(S, CONV_ROWS)
    hb = ts // CONV_HALO
    nblk = S // ts
    last_halo = S // CONV_HALO - 1
    nrows = ts + CONV_HALO
    groups = CONV_CHUNK // SUBLANES

    def body(d_ref, dn_ref, a_ref, b_ref, dw_ref, da_ref, db_ref, ddw_ref, sums_ref, dbuf, dsh):
        i = pl.program_id(0)

        @pl.when(i == 0)
        def _():
            ddw_ref[...] = jnp.zeros_like(ddw_ref)
            sums_ref[...] = jnp.zeros_like(sums_ref)

        dbuf[ts:nrows, :] = jnp.where(i == nblk - 1, 0.0, dn_ref[...])

        def fill(r0):
            rows = pl.ds(r0, CONV_CHUNK)
            dbuf[rows, :] = d_ref[rows, :]
        _rows(ts, CONV_CHUNK, fill)
        _shifted_copies(dbuf, dsh, nrows - SUBLANES)

        zero = jnp.zeros((SUBLANES, LANES), F32)
        fold = lambda x: jnp.sum(x.reshape(groups, SUBLANES, LANES), axis=0)
        for s in range(C // LANES):
            cols = slice(s * LANES, (s + 1) * LANES)
            w = _tap_weights(dw_ref, cols)

            def step(j, carry, w=w, cols=cols):
                r0 = pl.multiple_of(j * CONV_CHUNK, CONV_CHUNK)
                rows = pl.ds(r0, CONV_CHUNK)
                a, sb = a_ref[rows, cols], _sigmoid(b_ref[rows, cols])
                u0 = a * sb
                accs = list(carry[2:])
                parts = []
                for u in range(groups):
                    u0_u = u0[u * SUBLANES:(u + 1) * SUBLANES]
                    acc = zero
                    for k in range(K):
                        x = _tap(dbuf, dsh, r0 + u * SUBLANES, K - 1 - k, cols)
                        acc = acc + w[k] * x
                        accs[k] = accs[k] + u0_u * x
                    parts.append(acc)
                da = jnp.concatenate(parts, axis=0) * sb
                db = da * a * (1.0 - sb)
                da_ref[rows, cols] = da.astype(BF16)
                db_ref[rows, cols] = db.astype(BF16)
                return (carry[0] + fold(da), carry[1] + fold(db), *accs)
            out = lax.fori_loop(0, ts // CONV_CHUNK, step, (zero,) * (K + 2))
            sums_ref[0, :, cols] += out[0]
            sums_ref[1, :, cols] += out[1]
            for k in range(K):
                ddw_ref[k, :, cols] += out[2 + k]

    row = lambda col: pl.BlockSpec((ts, C), lambda i: (i, col))
    nxt = pl.BlockSpec((CONV_HALO, C), lambda i: (jnp.minimum((i + 1) * hb, last_halo), 0))
    return pl.pallas_call(
        body, name=name, grid=(nblk,),
        in_specs=[row(0), nxt, row(0), row(1), pl.BlockSpec((K, C), lambda i: (0, 0))],
        out_specs=[row(0), row(0), pl.BlockSpec((K, SUBLANES, C), lambda i: (0, 0, 0)),
                   pl.BlockSpec((2, SUBLANES, C), lambda i: (0, 0, 0))],
        out_shape=[jax.ShapeDtypeStruct((S, C), BF16), jax.ShapeDtypeStruct((S, C), BF16),
                   jax.ShapeDtypeStruct((K, SUBLANES, C), F32), jax.ShapeDtypeStruct((2, SUBLANES, C), F32)],
        scratch_shapes=[pltpu.VMEM((nrows, C), F32), pltpu.VMEM((SUBLANES, nrows, C), F32)],
        compiler_params=_params("arbitrary"),
    )(du1, du1, proj, proj, dw)


def _adamw(parts, w, m, v, name):
    R, C = w.shape
    tr = _tile(R, 256)
    c1 = 1.0 - ADAM_B1 ** ADAM_STEP
    c2 = 1.0 - ADAM_B2 ** ADAM_STEP

    def body(p_ref, w_ref, m_ref, v_ref, g_ref, d_ref, nm_ref, nv_ref):
        g = p_ref[0].astype(F32)
        for d in range(1, N_DEV):
            g = g + p_ref[d].astype(F32)
        nm = ADAM_B1 * m_ref[...] + (1.0 - ADAM_B1) * g
        nv = ADAM_B2 * v_ref[...] + (1.0 - ADAM_B2) * (g * g)
        g_ref[...] = g
        nm_ref[...] = nm
        nv_ref[...] = nv
        d_ref[...] = -ADAM_LR * ((nm / c1) / (jnp.sqrt(nv / c2) + ADAM_EPS) + ADAM_WD * w_ref[...])

    row = pl.BlockSpec((tr, C), lambda i: (i, 0))
    out = jax.ShapeDtypeStruct((R, C), F32)
    return pl.pallas_call(
        body, name=name, grid=(R // tr,),
        in_specs=[pl.BlockSpec((N_DEV, tr, C), lambda i: (0, i, 0)), row, row, row],
        out_specs=[row, row, row, row], out_shape=[out, out, out, out],
        compiler_params=_params("parallel"),
    )(parts, w, m, v)


def _pad_lanes(a, width=LANES):
    return jnp.pad(a, ((0, 0), (0, width - a.shape[1])))


def _flat_rows(parts, width=LANES):
    flat = jnp.concatenate([p.reshape(-1) for p in parts])
    rows = -(-flat.shape[0] // width)
    rows = -(-rows // SUBLANES) * SUBLANES
    return jnp.pad(flat, (0, rows * width - flat.shape[0])).reshape(rows, width)


def _unflat(rows2d, shapes):
    flat = rows2d.reshape(-1)
    out, pos = [], 0
    for s in shapes:
        n = int(np.prod(s))
        out.append(flat[pos:pos + n].reshape(s))
        pos += n
    return out


def kernel(x, norm_g, fox_w_in, fox_b_f, fox_w_out, conv_w_in, conv_b_in, conv_dw, conv_dw_b, conv_ln_g, conv_ln_b, conv_w_out, final_norm_g, loss_target, m_norm_g, m_fox_w_in, m_fox_b_f, m_fox_w_out, m_conv_w_in, m_conv_b_in, m_conv_dw, m_conv_dw_b, m_conv_ln_g, m_conv_ln_b, m_conv_w_out, m_final_norm_g, v_norm_g, v_fox_w_in, v_fox_b_f, v_fox_w_out, v_conv_w_in, v_conv_b_in, v_conv_dw, v_conv_dw_b, v_conv_ln_g, v_conv_ln_b, v_conv_w_out, v_final_norm_g):
    h0 = x[0]
    target = loss_target[0]
    S, D = h0.shape
    depth = norm_g.shape[0]
    n_fox, _, fin_shard = fox_w_in.shape
    n_conv, _, cin_shard = conv_w_in.shape
    heads = fox_b_f.shape[1]
    W = fox_w_out.shape[1] * N_DEV
    C = conv_w_out.shape[1] * N_DEV
    assert fin_shard * N_DEV == 4 * W + heads and cin_shard * N_DEV == 3 * C and heads <= LANES
    is_fox = lambda i: i % 2 == 0

    shards = {}
    for i in range(depth):
        j = i // 2
        w_in, w_out = (fox_w_in, fox_w_out) if is_fox(i) else (conv_w_in, conv_w_out)
        shards[i] = (w_in[j].astype(BF16), w_out[j].astype(BF16))
    split = (5 * D) // 8

    def carried(*args, carry=None, **kw):
        r = _matmul(*args, carry=carry, **kw)
        return r if carry is not None else (r, None)

    small_shapes = [conv_b_in.shape, conv_dw.shape, conv_dw_b.shape, conv_ln_g.shape, conv_ln_b.shape]
    small = _flat_rows([conv_b_in, conv_dw, conv_dw_b, conv_ln_g, conv_ln_b])
    small_all = _exchange(small, "ag_small", "gather")
    b_in_s, dw_s, dwb_s, lng_s, lnb_s = zip(*[_unflat(small_all[d], small_shapes) for d in range(N_DEV)])
    cat = lambda parts, axis: jnp.concatenate(parts, axis=axis)
    conv_b_in_f = cat(b_in_s, 1)
    conv_dw_f = cat(dw_s, 2)
    conv_dw_b_f = cat(dwb_s, 1)
    conv_ln_g_f = cat(lng_s, 1)
    conv_ln_b_f = cat(lnb_s, 1)

    gathered_in = {0: _exchange(shards[0][0], "ag_w_in0", "gather")}
    gathered_out = {}

    h = h0
    saved = []
    for i in range(depth):
        j = i // 2
        has_next = i + 1 < depth
        gather = lambda x, kind="gather": (x, kind) if has_next else None
        hn, hn_t = _rms_fwd(h, norm_g[i:i + 1], f"rms_fwd{i}")
        w = gathered_in.pop(i)
        if is_fox(i):
            w = jnp.transpose(w, (1, 0, 2)).reshape(D, -1)
            w_qkvg, w_f = w[:, :4 * W], _pad_lanes(w[:, 4 * W:])
            qkv, got_in = carried(hn, w_qkvg, "nn", BF16, f"fox_qkv{i}", n_out=3 * W,
                                  carry=gather(shards[i + 1][0] if has_next else None, "gather_cols"))
            gate, got = carried(hn, w_qkvg, "nn", F32, f"fox_gate{i}", b_col_off=3 * W, n_out=W,
                                carry=(shards[i][1], "gather"))
            w_out = got.reshape(W, D)
            f = _matmul(hn, w_f, "nn", F32, f"fox_f{i}")
            b_f = _pad_lanes(fox_b_f[j:j + 1])
            c = _fgate_fwd(f, b_f, f"fgate_fwd{i}")
            key_bias = lambda t: (-c[:, :heads]).T.reshape(heads, S // t, 1, t)
            y, o, lse = _flash_fwd(qkv, gate, key_bias(_tile(S, FLASH_FWD_TILE)), heads, f"flash_fwd{i}")
            cneg = key_bias(_tile(S, FLASH_BWD_TILE))
            h_next, got_out = carried(y, w_out, "nn", F32, f"out_proj{i}", add=h,
                                      carry=gather(shards[i + 1][1] if has_next else None))
            if has_next:
                gathered_in[i + 1], gathered_out[i + 1] = got_in, got_out
            saved.append(dict(h=h, hn_t=hn_t, qkv=qkv, gate=gate, f=f, b_f=b_f, cneg=cneg, y=y, o=o, lse=lse,
                              w_qkvg=w_qkvg, w_f=w_f, w_out=w_out))
        else:
            w_out = gathered_out.pop(i).reshape(C, D)
            nxt = shards[i + 1][0] if has_next else None
            proj, got_a = carried(hn, w, "nn", F32, f"conv_in{i}", bias=conv_b_in_f[j:j + 1],
                                  carry=gather(nxt[:split] if has_next else None))
            y, u1 = _conv_fwd(proj, conv_dw_f[j], conv_dw_b_f[j:j + 1], conv_ln_g_f[j:j + 1],
                              conv_ln_b_f[j:j + 1], f"conv_fwd{i}")
            h_next, got_b = carried(y, w_out, "nn", F32, f"out_proj{i}", add=h,
                                    carry=gather(nxt[split:] if has_next else None))
            if has_next:
                gathered_in[i + 1] = jnp.concatenate([got_a, got_b], axis=1)
            saved.append(dict(h=h, hn_t=hn_t, proj=proj, y=y, u1=u1, w_in=w, w_out=w_out))
        h = h_next

    dh, dh16, dh16_t, loss_part, dg_final = _loss_head(h, final_norm_g[None, :], target, "loss_head")

    def shard_cols(g, shard):
        return jnp.transpose(g.reshape(g.shape[0], N_DEV, shard), (1, 0, 2))

    d_norm_g = [None] * depth
    d_fox_b_f = [None] * n_fox
    d_conv_small = [None] * n_conv
    summed_in = [None] * depth
    summed_out = [None] * depth
    pend_in = pend_out = None
    early_in = None
    for i in reversed(range(depth)):
        j = i // 2
        sv = saved[i]
        scatter = lambda x, kind="scatter": (x, kind) if x is not None else None
        dy = _matmul(dh16, sv["w_out"], "nt", F32, f"d_out_proj{i}")
        dw_out_t, got = carried(dh16_t, sv["y"], "nn", F32, f"dw_out{i}", carry=scatter(pend_out))
        if pend_out is not None:
            summed_out[i + 1] = got
        own_out = dw_out_t.T.reshape(N_DEV, -1, D).astype(BF16)
        if is_fox(i):
            do, dgate, delta = _fox_gate_bwd(dy, sv["o"], sv["gate"], f"fox_gate_bwd{i}")
            dq, dk, dv, colsum, rowsum = _flash_bwd(sv["qkv"], do, delta, sv["lse"], sv["cneg"], heads,
                                                    f"flash_bwd{i}")
            dc = _pad_lanes(rowsum[:, :heads] - jnp.sum(colsum, axis=2).reshape(heads, S).T)
            df, dbf = _fgate_bwd(dc, sv["f"], sv["b_f"], f"fgate_bwd{i}")
            dproj = jnp.concatenate([dq, dk, dv, dgate], axis=1)
            dhn = _matmul(df, sv["w_f"], "nt", F32, f"d_fox_f{i}")
            dhn, got = carried(dproj, sv["w_qkvg"], "nt", F32, f"d_fox_in{i}", add=dhn,
                               carry=scatter(pend_in, "scatter_cols"))
            if pend_in is not None:
                summed_in[i + 1] = got
            dw_f = _matmul(sv["hn_t"], df, "nn", F32, f"dw_fox_f{i}")[:, :heads]
            parts_of = lambda dw, rows: shard_cols(jnp.concatenate([dw, dw_f[rows]], axis=1), fin_shard).astype(BF16)
            if i > 0:
                dw_qkvg, summed_out[i] = carried(sv["hn_t"], dproj, "nn", F32, f"dw_fox_in{i}",
                                                 carry=(own_out, "scatter"))
                pend_in = parts_of(dw_qkvg, slice(None))
            else:
                top, bottom = slice(0, D // 2), slice(D // 2, D)
                dw_top, summed_out[i] = carried(sv["hn_t"][top], dproj, "nn", F32, f"dw_fox_in{i}",
                                                carry=(own_out, "scatter"))
                dw_bottom, early_in = carried(sv["hn_t"][bottom], dproj, "nn", F32, f"dw_fox_in{i}_rest",
                                              carry=(parts_of(dw_top, top), "scatter"))
                pend_in = parts_of(dw_bottom, bottom)
            pend_out = None
            d_fox_b_f[j] = jnp.sum(dbf, axis=0)[:heads]
        else:
            du1, dgate, nsums = _conv_bwd_norm(dy, sv["proj"], sv["u1"], conv_ln_g_f[j:j + 1],
                                               conv_ln_b_f[j:j + 1], f"conv_bwd_norm{i}")
            da, db, ddw, absums = _conv_bwd_taps(du1, sv["proj"], conv_dw_f[j], f"conv_bwd_taps{i}")
            dproj = jnp.concatenate([da, db, dgate], axis=1)
            half = D // 2
            dhn, got_a = carried(dproj, sv["w_in"], "nt", F32, f"d_conv_in{i}",
                                 carry=scatter(pend_in[:, :half] if pend_in is not None else None))
            dw_in, got_b = carried(sv["hn_t"], dproj, "nn", BF16, f"dw_conv_in{i}",
                                   carry=scatter(pend_in[:, half:] if pend_in is not None else None))
            if pend_in is not None:
                summed_in[i + 1] = jnp.concatenate([got_a, got_b], axis=1)
            pend_in, pend_out = dw_in, own_out
            nsum = jnp.sum(nsums, axis=1)
            absum = jnp.sum(absums, axis=1)
            d_conv_small[j] = dict(b_in=jnp.concatenate([absum[0], absum[1], nsum[3]]),
                                   dw=jnp.sum(ddw, axis=1), dw_b=nsum[2], ln_g=nsum[0], ln_b=nsum[1])
        dh, dh16, dh16_t, dg = _rms_bwd(sv["h"], norm_g[i:i + 1], dhn, dh, f"rms_bwd{i}")
        d_norm_g[i] = jnp.sum(dg, axis=0)
    summed_in[0] = _exchange(pend_in, "rs_w_in0", "scatter")
    if early_in is not None:
        summed_in[0] = jnp.concatenate([early_in, summed_in[0]], axis=1)
    if pend_out is not None:
        summed_out[0] = _exchange(pend_out, "rs_w_out0", "scatter")
    grad_x = dh[None]

    def small_for(d):
        sl = lambda a, n: a[..., d * n:(d + 1) * n]
        return _flat_rows([
            jnp.stack([sl(s["b_in"], 3 * C // N_DEV) for s in d_conv_small]),
            jnp.stack([sl(s["dw"], C // N_DEV) for s in d_conv_small]),
            jnp.stack([sl(s["dw_b"], C // N_DEV) for s in d_conv_small]),
            jnp.stack([sl(s["ln_g"], C // N_DEV) for s in d_conv_small]),
            jnp.stack([sl(s["ln_b"], C // N_DEV) for s in d_conv_small])])
    r_small = _exchange(jnp.stack([small_for(d) for d in range(N_DEV)]), "rs_small", "scatter")

    rep_shapes = [norm_g.shape, fox_b_f.shape, final_norm_g.shape, (1,)]
    rep_part = _flat_rows([jnp.stack(d_norm_g), jnp.stack(d_fox_b_f), jnp.sum(dg_final, axis=0),
                           jnp.sum(loss_part[:, 0])[None]])
    r_rep = _exchange(rep_part, "ag_replicated", "gather")

    def update(parts, w, m, v, name):
        two_d = (-1, w.shape[-1])
        stacked = jnp.stack(parts, axis=1).reshape((N_DEV,) + w.reshape(two_d).shape)
        res = _adamw(stacked, w.reshape(two_d), m.reshape(two_d), v.reshape(two_d), name)
        return [r.reshape(w.shape) for r in res]

    fox_layers = [i for i in range(depth) if is_fox(i)]
    conv_layers = [i for i in range(depth) if not is_fox(i)]
    u_fin = update([summed_in[i] for i in fox_layers], fox_w_in, m_fox_w_in, v_fox_w_in, "adamw_fox_w_in")
    u_fout = update([summed_out[i] for i in fox_layers], fox_w_out, m_fox_w_out, v_fox_w_out, "adamw_fox_w_out")
    u_cin = update([summed_in[i] for i in conv_layers], conv_w_in, m_conv_w_in, v_conv_w_in, "adamw_conv_w_in")
    u_cout = update([summed_out[i] for i in conv_layers], conv_w_out, m_conv_w_out, v_conv_w_out,
                    "adamw_conv_w_out")
    u_small = _adamw(r_small, small,
                     _flat_rows([m_conv_b_in, m_conv_dw, m_conv_dw_b, m_conv_ln_g, m_conv_ln_b]),
                     _flat_rows([v_conv_b_in, v_conv_dw, v_conv_dw_b, v_conv_ln_g, v_conv_ln_b]), "adamw_small")
    zero1 = jnp.zeros((1,), F32)
    u_rep = _adamw(r_rep, _flat_rows([norm_g, fox_b_f, final_norm_g, zero1]),
                   _flat_rows([m_norm_g, m_fox_b_f, m_final_norm_g, zero1]),
                   _flat_rows([v_norm_g, v_fox_b_f, v_final_norm_g, zero1]), "adamw_replicated")

    outs = []
    loss = None
    for kind in range(4):
        b_in_k, dw_k, dwb_k, lng_k, lnb_k = _unflat(u_small[kind], small_shapes)
        ng_k, bf_k, fg_k, loss_k = _unflat(u_rep[kind], rep_shapes)
        if kind == 0:
            loss = loss_k[0]
        outs += [ng_k, u_fin[kind], bf_k, u_fout[kind], u_cin[kind], b_in_k, dw_k, dwb_k, lng_k, lnb_k,
                 u_cout[kind], fg_k]
    return (loss, grad_x, *outs)
```

```python
import functools

import numpy as np
import jax
import jax.numpy as jnp
from jax import lax
from jax.experimental import pallas as pl
from jax.experimental.pallas import tpu as pltpu

F32 = jnp.float32
BF16 = jnp.bfloat16
MESH_ID = pl.DeviceIdType.MESH

N_DEV = 8
RMS_EPS = 1e-6
LN_EPS = 1e-5
ADAM_LR = 0.001
ADAM_B1 = 0.9
ADAM_B2 = 0.999
ADAM_EPS = 1e-08
ADAM_WD = 0.01
ADAM_STEP = 10

LANES = 128
SUBLANES = 8
VMEM_LIMIT = 56 * 1024 * 1024
NEG_BIG = -1e30
CONV_HALO = 32
FLASH_FWD_TILE = 1024
FLASH_BWD_TILE = 512
CONV_ROWS = 128
CONV_CHUNK = 32
FGATE_ROWS = 256
ROW_TILE = 256


def _params(*sem):
    return pltpu.CompilerParams(dimension_semantics=sem if sem else None, vmem_limit_bytes=VMEM_LIMIT)


def _tile(n, pref):
    if n <= pref:
        return n
    t = pref
    while n % t:
        t //= 2
    return t


def _sigmoid(x):
    return 1.0 / (1.0 + jnp.exp(-x))


def _dsilu(x, s):
    return s * (1.0 + x * (1.0 - s))


def _rows(n, chunk, fn):
    def step(i, carry):
        fn(pl.multiple_of(i * chunk, chunk))
        return carry
    lax.fori_loop(0, n // chunk, step, 0)


def _peer(k):
    x, y, c = lax.axis_index("x"), lax.axis_index("y"), lax.axis_index("c")
    px = 1 - x if (k >> 2) & 1 else x
    py = 1 - y if (k >> 1) & 1 else y
    pc = 1 - c if k & 1 else c
    return (px, py, pc), 4 * px + 2 * py + pc


def _exchange_copies(x_ref, o_ref, send_sems, recv_sems, local_sem, kind):
    _, me = _peer(0)

    def window(ref, idx, n):
        assert n % LANES == 0
        return ref.at[:, pl.ds(pl.multiple_of(idx * n, LANES), n)]

    def src(idx):
        if kind == "scatter":
            return x_ref.at[idx]
        return window(x_ref, idx, o_ref.shape[2]) if kind == "scatter_cols" else x_ref

    def dst(idx):
        return window(o_ref, idx, x_ref.shape[1]) if kind == "gather_cols" else o_ref.at[idx]

    local = pltpu.make_async_copy(src(me), dst(me), local_sem)
    sends, arrivals = [], []
    for k in range(1, N_DEV):
        peer, pidx = _peer(k)
        sems = dict(send_sem=send_sems.at[k - 1], recv_sem=recv_sems.at[k - 1], device_id=peer,
                    device_id_type=MESH_ID)
        sends.append(pltpu.make_async_remote_copy(src_ref=src(pidx), dst_ref=dst(me), **sems))
        arrivals.append(pltpu.make_async_remote_copy(src_ref=src(me), dst_ref=dst(pidx), **sems))
    return local, sends, arrivals


def _exchange_start(*refs, kind):
    local, sends, _ = _exchange_copies(*refs, kind)
    local.start()
    for cp in sends:
        cp.start()


def _exchange_wait(*refs, kind):
    local, sends, arrivals = _exchange_copies(*refs, kind)
    for cp in arrivals:
        cp.wait_recv()
    for cp in sends:
        cp.wait_send()
    local.wait()


EXCHANGE_SCRATCH = [pltpu.SemaphoreType.DMA((N_DEV - 1,)), pltpu.SemaphoreType.DMA((N_DEV - 1,)),
                    pltpu.SemaphoreType.DMA]


def _exchange_shape(x, kind):
    shape = {"gather": lambda: (N_DEV,) + x.shape, "scatter": lambda: x.shape,
             "gather_cols": lambda: (x.shape[0], N_DEV * x.shape[1]),
             "scatter_cols": lambda: (N_DEV, x.shape[0], x.shape[1] // N_DEV)}[kind]()
    return jax.ShapeDtypeStruct(shape, x.dtype)


def _exchange(x, name, kind):
    def body(*refs):
        _exchange_start(*refs, kind=kind)
        _exchange_wait(*refs, kind=kind)

    return pl.pallas_call(
        body, name=name,
        out_shape=_exchange_shape(x, kind),
        in_specs=[pl.BlockSpec(memory_space=pl.ANY)],
        out_specs=pl.BlockSpec(memory_space=pl.ANY),
        scratch_shapes=list(EXCHANGE_SCRATCH),
    )(x)


def _matmul(a, b, mode, out_dtype, name, bias=None, add=None, b_col_off=0, n_out=None, carry=None):
    M, K = a.shape
    N = n_out if n_out is not None else (b.shape[0] if mode == "nt" else b.shape[1])
    tm = _tile(M, 512)
    tn = _tile(N, 1024)
    k_cap = 4096 if a.dtype.itemsize == 2 and b.dtype.itemsize == 2 else 2048
    tk = next(K // d for d in range(1, K + 1) if K % d == 0 and K // d <= k_cap and (K // d) % LANES == 0)
    nm, nn, nk = M // tm, N // tn, K // tk
    assert b_col_off % tn == 0
    joff = b_col_off // tn
    dims = {"nn": (((1,), (0,)), ((), ())), "nt": (((1,), (1,)), ((), ()))}[mode]
    n_in = 2 + (bias is not None) + (add is not None)

    def body(*refs):
        a_ref, b_ref = refs[0], refs[1]
        bias_ref = refs[2] if bias is not None else None
        add_ref = refs[n_in - 1] if add is not None else None
        pos = n_in
        x_ref = o_ref = x_out_ref = None
        if carry is not None:
            x_ref, o_ref, x_out_ref = refs[pos], refs[pos + 1], refs[pos + 2]
            pos += 3
        else:
            o_ref = refs[pos]
            pos += 1
        acc_ref = None
        if nk > 1:
            acc_ref = refs[pos]
            pos += 1
        exchange_refs = (x_ref, x_out_ref) + tuple(refs[pos:])
        i, j, kk = pl.program_id(0), pl.program_id(1), pl.program_id(2)

        if carry is not None:
            @pl.when((i == 0) & (j == 0) & (kk == 0))
            def _():
                _exchange_start(*exchange_refs, kind=carry[1])

        part = lax.dot_general(a_ref[...].astype(BF16), b_ref[...].astype(BF16), dims,
                               preferred_element_type=F32)

        def finish(r):
            if bias_ref is not None:
                r = r + bias_ref[...]
            if add_ref is not None:
                r = r + add_ref[...]
            o_ref[...] = r.astype(o_ref.dtype)

        if nk == 1:
            finish(part)
        else:
            @pl.when(kk == 0)
            def _():
                acc_ref[...] = part

            @pl.when(kk > 0)
            def _():
                acc_ref[...] += part

            @pl.when(kk == nk - 1)
            def _():
                finish(acc_ref[...])

        if carry is not None:
            @pl.when((i == nm - 1) & (j == nn - 1) & (kk == nk - 1))
            def _():
                _exchange_wait(*exchange_refs, kind=carry[1])

    a_spec = pl.BlockSpec((tm, tk), lambda i, j, k: (i, k))
    if mode == "nt":
        b_spec = pl.BlockSpec((tn, tk), lambda i, j, k: (j, k))
    else:
        b_spec = pl.BlockSpec((tk, tn), lambda i, j, k: (k, j + joff))
    in_specs = [a_spec, b_spec]
    args = [a, b]
    if bias is not None:
        in_specs.append(pl.BlockSpec((1, tn), lambda i, j, k: (0, j)))
        args.append(bias)
    if add is not None:
        in_specs.append(pl.BlockSpec((tm, tn), lambda i, j, k: (i, j)))
        args.append(add)
    out_specs = [pl.BlockSpec((tm, tn), lambda i, j, k: (i, j))]
    out_shape = [jax.ShapeDtypeStruct((M, N), out_dtype)]
    scratch = [pltpu.VMEM((tm, tn), F32)] if nk > 1 else []
    if carry is not None:
        in_specs.append(pl.BlockSpec(memory_space=pl.ANY))
        args.append(carry[0])
        out_specs.append(pl.BlockSpec(memory_space=pl.ANY))
        out_shape.append(_exchange_shape(*carry))
        scratch += EXCHANGE_SCRATCH
    res = pl.pallas_call(
        body, name=name,
        grid=(nm, nn, nk),
        in_specs=in_specs, out_specs=out_specs, out_shape=out_shape, scratch_shapes=scratch,
        compiler_params=_params(*(("arbitrary",) * 3 if carry is not None else ("parallel", "parallel", "arbitrary"))),
    )(*args)
    return res if carry is not None else res[0]


def _rms_fwd(h, g, name):
    S, D = h.shape
    ts = _tile(S, 512)

    def body(h_ref, g_ref, o_ref, ot_ref):
        x = h_ref[...]
        r = lax.rsqrt(jnp.mean(x * x, axis=-1, keepdims=True) + RMS_EPS)
        y = x * r * g_ref[...]
        o_ref[...] = y.astype(BF16)
        ot_ref[...] = y.T.astype(BF16)

    return pl.pallas_call(
        body, name=name, grid=(S // ts,),
        in_specs=[pl.BlockSpec((ts, D), lambda i: (i, 0)), pl.BlockSpec((1, D), lambda i: (0, 0))],
        out_specs=[pl.BlockSpec((ts, D), lambda i: (i, 0)), pl.BlockSpec((D, ts), lambda i: (0, i))],
        out_shape=[jax.ShapeDtypeStruct((S, D), BF16), jax.ShapeDtypeStruct((D, S), BF16)],
        compiler_params=_params("parallel"),
    )(h, g)


def _rms_bwd_block(x, g, dy):
    r = lax.rsqrt(jnp.mean(x * x, axis=-1, keepdims=True) + RMS_EPS)
    xr = x * r
    t = dy * g
    dx = r * (t - xr * jnp.mean(t * xr, axis=-1, keepdims=True))
    return dx, dy * xr


def _rms_bwd(h, g, dhn, dh, name):
    S, D = h.shape
    ts = _tile(S, ROW_TILE)

    def body(h_ref, g_ref, dhn_ref, dh_ref, o_ref, o16_ref, o16t_ref, dg_ref):
        dx, dgt = _rms_bwd_block(h_ref[...], g_ref[...], dhn_ref[...])
        out = dh_ref[...] + dx
        o_ref[...] = out
        o16_ref[...] = out.astype(BF16)
        o16t_ref[...] = out.T.astype(BF16)
        part = jnp.sum(dgt.reshape(ts // SUBLANES, SUBLANES, D), axis=0)

        @pl.when(pl.program_id(0) == 0)
        def _():
            dg_ref[...] = part

        @pl.when(pl.program_id(0) > 0)
        def _():
            dg_ref[...] += part

    row = pl.BlockSpec((ts, D), lambda i: (i, 0))
    return pl.pallas_call(
        body, name=name, grid=(S // ts,),
        in_specs=[row, pl.BlockSpec((1, D), lambda i: (0, 0)), row, row],
        out_specs=[row, row, pl.BlockSpec((D, ts), lambda i: (0, i)),
                   pl.BlockSpec((SUBLANES, D), lambda i: (0, 0))],
        out_shape=[jax.ShapeDtypeStruct((S, D), F32), jax.ShapeDtypeStruct((S, D), BF16),
                   jax.ShapeDtypeStruct((D, S), BF16), jax.ShapeDtypeStruct((SUBLANES, D), F32)],
        compiler_params=_params("arbitrary"),
    )(h, g, dhn, dh)


def _loss_head(h, g, target, name):
    S, D = h.shape
    ts = _tile(S, ROW_TILE)

    def body(h_ref, g_ref, t_ref, o_ref, o16_ref, o16t_ref, loss_ref, dg_ref):
        x = h_ref[...]
        gg = g_ref[...]
        r = lax.rsqrt(jnp.mean(x * x, axis=-1, keepdims=True) + RMS_EPS)
        err = x * r * gg - t_ref[...]
        row_loss = 0.5 * jnp.mean(err * err, axis=-1, keepdims=True)
        dx, dgt = _rms_bwd_block(x, gg, err * (1.0 / D))
        o_ref[...] = dx
        o16_ref[...] = dx.astype(BF16)
        o16t_ref[...] = dx.T.astype(BF16)
        part = jnp.sum(dgt.reshape(ts // SUBLANES, SUBLANES, D), axis=0)
        lpart = jnp.sum(jnp.broadcast_to(row_loss, (ts, LANES)).reshape(ts // SUBLANES, SUBLANES, LANES), axis=0)

        @pl.when(pl.program_id(0) == 0)
        def _():
            dg_ref[...] = part
            loss_ref[...] = lpart

        @pl.when(pl.program_id(0) > 0)
        def _():
            dg_ref[...] += part
            loss_ref[...] += lpart

    row = pl.BlockSpec((ts, D), lambda i: (i, 0))
    return pl.pallas_call(
        body, name=name, grid=(S // ts,),
        in_specs=[row, pl.BlockSpec((1, D), lambda i: (0, 0)), row],
        out_specs=[row, row, pl.BlockSpec((D, ts), lambda i: (0, i)),
                   pl.BlockSpec((SUBLANES, LANES), lambda i: (0, 0)), pl.BlockSpec((SUBLANES, D), lambda i: (0, 0))],
        out_shape=[jax.ShapeDtypeStruct((S, D), F32), jax.ShapeDtypeStruct((S, D), BF16),
                   jax.ShapeDtypeStruct((D, S), BF16), jax.ShapeDtypeStruct((SUBLANES, LANES), F32),
                   jax.ShapeDtypeStruct((SUBLANES, D), F32)],
        compiler_params=_params("arbitrary"),
    )(h, g, target)


def _split3(x):
    hi = x.astype(BF16)
    r1 = x - hi.astype(F32)
    mid = r1.astype(BF16)
    lo = (r1 - mid.astype(F32)).astype(BF16)
    return hi, mid, lo


def _tri_sum(tri, x):
    hi, mid, lo = _split3(x)
    dot = functools.partial(jnp.dot, preferred_element_type=F32)
    return dot(tri, hi) + dot(tri, mid) + dot(tri, lo)


def _fgate_fwd(f, b_f, name):
    S = f.shape[0]
    tb = _tile(S, FGATE_ROWS)

    def body(f_ref, b_ref, c_ref, carry_ref):
        @pl.when(pl.program_id(0) == 0)
        def _():
            carry_ref[...] = jnp.zeros_like(carry_ref)

        x = f_ref[...] + b_ref[...]
        lf = jnp.minimum(x, 0.0) - jnp.log1p(jnp.exp(-jnp.abs(x)))
        r = lax.broadcasted_iota(jnp.int32, (tb, tb), 0)
        c = lax.broadcasted_iota(jnp.int32, (tb, tb), 1)
        tri = (c <= r).astype(BF16)
        c_ref[...] = _tri_sum(tri, lf) + carry_ref[0:1, :]
        carry_ref[...] += _tri_sum(jnp.ones((SUBLANES, tb), BF16), lf)

    return pl.pallas_call(
        body, name=name, grid=(S // tb,),
        in_specs=[pl.BlockSpec((tb, LANES), lambda i: (i, 0)), pl.BlockSpec((1, LANES), lambda i: (0, 0))],
        out_specs=pl.BlockSpec((tb, LANES), lambda i: (i, 0)),
        out_shape=jax.ShapeDtypeStruct((S, LANES), F32),
        scratch_shapes=[pltpu.VMEM((SUBLANES, LANES), F32)],
        compiler_params=_params("arbitrary"),
    )(f, b_f)


def _fgate_bwd(dc, f, b_f, name):
    S = f.shape[0]
    tb = _tile(S, FGATE_ROWS)
    nb = S // tb

    def body(dc_ref, f_ref, b_ref, df_ref, db_ref, carry_ref):
        @pl.when(pl.program_id(0) == 0)
        def _():
            carry_ref[...] = jnp.zeros_like(carry_ref)
            db_ref[...] = jnp.zeros_like(db_ref)

        d = dc_ref[...]
        r = lax.broadcasted_iota(jnp.int32, (tb, tb), 0)
        c = lax.broadcasted_iota(jnp.int32, (tb, tb), 1)
        tri = (c >= r).astype(BF16)
        dlf = _tri_sum(tri, d) + carry_ref[0:1, :]
        carry_ref[...] += _tri_sum(jnp.ones((SUBLANES, tb), BF16), d)
        df = dlf * _sigmoid(-(f_ref[...] + b_ref[...]))
        df_ref[...] = df
        db_ref[...] += jnp.sum(df.reshape(tb // SUBLANES, SUBLANES, LANES), axis=0)

    rev = pl.BlockSpec((tb, LANES), lambda i: (nb - 1 - i, 0))
    return pl.pallas_call(
        body, name=name, grid=(nb,),
        in_specs=[rev, rev, pl.BlockSpec((1, LANES), lambda i: (0, 0))],
        out_specs=[rev, pl.BlockSpec((SUBLANES, LANES), lambda i: (0, 0))],
        out_shape=[jax.ShapeDtypeStruct((S, LANES), F32), jax.ShapeDtypeStruct((SUBLANES, LANES), F32)],
        scratch_shapes=[pltpu.VMEM((SUBLANES, LANES), F32)],
        compiler_params=_params("arbitrary"),
    )(dc, f, b_f)


LOG2E = 1.4426950408889634
NT_DIMS = (((1,), (1,)), ((), ()))
TN_DIMS = (((0,), (0,)), ((), ()))


def _causal_mask(t):
    r = lax.broadcasted_iota(jnp.int32, (t, t), 0)
    c = lax.broadcasted_iota(jnp.int32, (t, t), 1)
    return r >= c


def _key_tiles(qi, tile, group):
    def several(i, carry):
        for u in range(group):
            tile(group * i + u, False)
        return carry
    lax.fori_loop(0, qi // group, several, 0)

    def single(j, carry):
        tile(j, False)
        return carry
    lax.fori_loop((qi // group) * group, qi, single, 0)
    tile(qi, True)


def _flash_fwd(qkv, gate, cneg, heads, name):
    S, W3 = qkv.shape
    W = W3 // 3
    dh = W // heads
    assert dh == LANES
    tq = _tile(S, FLASH_FWD_TILE)
    nq = S // tq
    c1 = dh ** -0.5 * LOG2E

    def body(q_ref, k_ref, v_ref, b_ref, g_ref, y_ref, o_ref, lse_ref, m_scr, l_scr, acc_scr):
        qi = pl.program_id(1)
        m_scr[...] = jnp.full_like(m_scr, NEG_BIG)
        l_scr[...] = jnp.zeros_like(l_scr)
        acc_scr[...] = jnp.zeros_like(acc_scr)
        q = q_ref[...]

        def tile(j, diagonal):
            rows = pl.ds(pl.multiple_of(j * tq, tq), tq)
            t = lax.dot_general(q, k_ref[rows, :], NT_DIMS, preferred_element_type=F32) * c1 + b_ref[0, j] * LOG2E
            if diagonal:
                t = jnp.where(_causal_mask(tq), t, NEG_BIG)
            m_prev = m_scr[...]
            m_next = jnp.maximum(m_prev, jnp.max(t, axis=1, keepdims=True))
            alpha = jnp.exp2(m_prev - m_next)
            p = jnp.exp2(t - m_next[:, :1])
            l_scr[...] = alpha * l_scr[...] + jnp.sum(p, axis=1, keepdims=True)
            acc_scr[...] = alpha * acc_scr[...] + jnp.dot(p.astype(BF16), v_ref[rows, :],
                                                          preferred_element_type=F32)
            m_scr[...] = m_next

        _key_tiles(qi, tile, 4)
        l = l_scr[...]
        o = acc_scr[...] / l
        g = g_ref[...]
        o_ref[...] = o
        lse_ref[...] = m_scr[...] + jnp.log2(l)
        y_ref[...] = (o * (g * _sigmoid(g))).astype(BF16)

    qblk = pl.BlockSpec((tq, dh), lambda h, i: (i, h))
    return pl.pallas_call(
        body, name=name, grid=(heads, nq),
        in_specs=[qblk,
                  pl.BlockSpec((S, dh), lambda h, i: (0, heads + h)),
                  pl.BlockSpec((S, dh), lambda h, i: (0, 2 * heads + h)),
                  pl.BlockSpec((1, nq, 1, tq), lambda h, i: (h, 0, 0, 0)),
                  qblk],
        out_specs=[qblk, qblk, qblk],
        out_shape=[jax.ShapeDtypeStruct((S, W), BF16), jax.ShapeDtypeStruct((S, W), F32),
                   jax.ShapeDtypeStruct((S, W), F32)],
        scratch_shapes=[pltpu.VMEM((tq, LANES), F32), pltpu.VMEM((tq, LANES), F32), pltpu.VMEM((tq, dh), F32)],
        compiler_params=_params("parallel", "arbitrary"),
    )(qkv, qkv, qkv, cneg, gate)


def _flash_bwd(qkv, do, delta, lse, cneg, heads, name):
    S, W3 = qkv.shape
    W = W3 // 3
    dh = W // heads
    tq = _tile(S, FLASH_BWD_TILE)
    nq = S // tq
    scale = dh ** -0.5
    c1 = scale * LOG2E

    def body(q_ref, k_ref, v_ref, b_ref, do_ref, delta_ref, lse_ref,
             dq_ref, dk_ref, dv_ref, db_ref, rs_ref, dq_scr, dk_scr, dv_scr, rs_scr):
        h, qi = pl.program_id(0), pl.program_id(1)

        @pl.when((h == 0) & (qi == 0))
        def _():
            rs_ref[...] = jnp.zeros_like(rs_ref)

        @pl.when(qi == 0)
        def _():
            dk_scr[...] = jnp.zeros_like(dk_scr)
            dv_scr[...] = jnp.zeros_like(dv_scr)
            db_ref[...] = jnp.zeros_like(db_ref)

        dq_scr[...] = jnp.zeros_like(dq_scr)
        rs_scr[...] = jnp.zeros_like(rs_scr)
        q, d_o = q_ref[...], do_ref[...]
        lse = lse_ref[...][:, :1]
        delta = delta_ref[...][:, :1]

        def tile(j, diagonal):
            rows = pl.ds(pl.multiple_of(j * tq, tq), tq)
            k, v = k_ref[rows, :], v_ref[rows, :]
            t = lax.dot_general(q, k, NT_DIMS, preferred_element_type=F32) * c1 + b_ref[0, j] * LOG2E
            p = jnp.exp2(t - lse)
            if diagonal:
                p = jnp.where(_causal_mask(tq), p, 0.0)
            dp = lax.dot_general(d_o, v, NT_DIMS, preferred_element_type=F32)
            ds = p * (dp - delta)
            dv_scr[rows, :] += lax.dot_general(p.astype(BF16), d_o, TN_DIMS, preferred_element_type=F32)
            db_ref[0, j] += jnp.sum(ds.reshape(tq // SUBLANES, SUBLANES, tq), axis=0)
            dsb = (ds * scale).astype(BF16)
            dk_scr[rows, :] += lax.dot_general(dsb, q, TN_DIMS, preferred_element_type=F32)
            dq_scr[...] += jnp.dot(dsb, k, preferred_element_type=F32)
            rs_scr[...] += jnp.sum(ds, axis=1, keepdims=True)

        _key_tiles(qi, tile, 4)
        dq_ref[...] = dq_scr[...].astype(BF16)
        lane = lax.broadcasted_iota(jnp.int32, (tq, LANES), 1)
        rs_ref[pl.ds(pl.multiple_of(qi * tq, tq), tq), :] += jnp.where(lane == h, rs_scr[...], 0.0)

        @pl.when(qi == nq - 1)
        def _():
            dk_ref[...] = dk_scr[...].astype(BF16)
            dv_ref[...] = dv_scr[...].astype(BF16)

    qblk = pl.BlockSpec((tq, dh), lambda h, i: (i, h))
    head = pl.BlockSpec((S, dh), lambda h, i: (0, h))
    return pl.pallas_call(
        body, name=name, grid=(heads, nq),
        in_specs=[qblk,
                  pl.BlockSpec((S, dh), lambda h, i: (0, heads + h)),
                  pl.BlockSpec((S, dh), lambda h, i: (0, 2 * heads + h)),
                  pl.BlockSpec((1, nq, 1, tq), lambda h, i: (h, 0, 0, 0)),
                  qblk, qblk, qblk],
        out_specs=[qblk, head, head,
                   pl.BlockSpec((1, nq, SUBLANES, tq), lambda h, i: (h, 0, 0, 0)),
                   pl.BlockSpec((S, LANES), lambda h, i: (0, 0))],
        out_shape=[jax.ShapeDtypeStruct((S, W), BF16), jax.ShapeDtypeStruct((S, W), BF16),
                   jax.ShapeDtypeStruct((S, W), BF16), jax.ShapeDtypeStruct((heads, nq, SUBLANES, tq), F32),
                   jax.ShapeDtypeStruct((S, LANES), F32)],
        scratch_shapes=[pltpu.VMEM((tq, dh), F32), pltpu.VMEM((S, dh), F32), pltpu.VMEM((S, dh), F32),
                        pltpu.VMEM((tq, LANES), F32)],
        compiler_params=_params("arbitrary", "arbitrary"),
    )(qkv, qkv, qkv, cneg, do, delta, lse)


def _fox_gate_bwd(dy, o, gate, name):
    S, W = dy.shape
    ts = _tile(S, ROW_TILE)

    def body(dy_ref, o_ref, g_ref, do_ref, dg_ref, delta_ref):
        d, g, o_val = dy_ref[...], g_ref[...], o_ref[...]
        sg = _sigmoid(g)
        d_o = (d * (g * sg)).astype(BF16)
        do_ref[...] = d_o
        dg_ref[...] = (d * o_val * _dsilu(g, sg)).astype(BF16)
        prod = d_o.astype(F32) * o_val
        for h in range(W // LANES):
            cols = slice(h * LANES, (h + 1) * LANES)
            delta_ref[:, cols] = jnp.broadcast_to(jnp.sum(prod[:, cols], axis=1, keepdims=True), (ts, LANES))

    row = pl.BlockSpec((ts, W), lambda i: (i, 0))
    return pl.pallas_call(
        body, name=name, grid=(S // ts,),
        in_specs=[row, row, row], out_specs=[row, row, row],
        out_shape=[jax.ShapeDtypeStruct((S, W), BF16), jax.ShapeDtypeStruct((S, W), BF16),
                   jax.ShapeDtypeStruct((S, W), F32)],
        compiler_params=_params("parallel"),
    )(dy, o, gate)


def _shifted_copies(buf_ref, sh_ref, rows):
    for j in range(1, SUBLANES):
        sh_ref[j, 0:rows, :] = buf_ref[j:j + rows, :]


def _tap(buf_ref, sh_ref, r0, off, cols):
    j, base = off % SUBLANES, off - off % SUBLANES
    if j == 0:
        return buf_ref[pl.ds(r0 + base, SUBLANES), cols]
    return sh_ref[j, pl.ds(r0 + base, SUBLANES), cols]


def _tap_weights(dw_ref, cols):
    return [jnp.broadcast_to(dw_ref[k:k + 1, cols], (SUBLANES, LANES)) for k in range(dw_ref.shape[0])]


def _conv_fwd(proj, dw, dw_b, ln_g, ln_b, name):
    S, C3 = proj.shape
    C = C3 // 3
    K = dw.shape[0]
    assert K - 1 <= CONV_HALO - 2
    ts = _tile(S, CONV_ROWS)
    hb = ts // CONV_HALO
    nrows = ts + CONV_HALO
    lead = CONV_HALO - (K - 1)

    def body(a_ref, b_ref, ah_ref, bh_ref, g_ref, dw_ref, dwb_ref, lg_ref, lb_ref, y_ref, u1_ref, buf, sh):
        first = pl.program_id(0) == 0
        buf[0:CONV_HALO, :] = jnp.where(first, 0.0, ah_ref[...] * _sigmoid(bh_ref[...]))

        def glu(r0):
            rows = pl.ds(r0, CONV_CHUNK)
            buf[pl.ds(r0 + CONV_HALO, CONV_CHUNK), :] = a_ref[rows, :] * _sigmoid(b_ref[rows, :])
        _rows(ts, CONV_CHUNK, glu)
        _shifted_copies(buf, sh, nrows - SUBLANES)

        for s in range(C // LANES):
            cols = slice(s * LANES, (s + 1) * LANES)
            w = _tap_weights(dw_ref, cols)
            bias = jnp.broadcast_to(dwb_ref[:, cols], (SUBLANES, LANES))

            def taps(r0, w=w, bias=bias, cols=cols):
                for u in range(CONV_CHUNK // SUBLANES):
                    r = r0 + u * SUBLANES
                    acc = bias
                    for k in range(K):
                        acc = acc + w[k] * _tap(buf, sh, r, lead + k, cols)
                    u1_ref[pl.ds(r, SUBLANES), cols] = acc
            _rows(ts, CONV_CHUNK, taps)

        def norm(r0):
            rows = pl.ds(r0, CONV_CHUNK)
            u1 = u1_ref[rows, :]
            mu = jnp.mean(u1, axis=-1, keepdims=True)
            xc = u1 - mu
            rstd = lax.rsqrt(jnp.mean(xc * xc, axis=-1, keepdims=True) + LN_EPS)
            z = xc * rstd * lg_ref[...] + lb_ref[...]
            g = g_ref[rows, :]
            y_ref[rows, :] = ((z * _sigmoid(z)) * (g * _sigmoid(g))).astype(BF16)
        _rows(ts, CONV_CHUNK, norm)

    row = lambda col: pl.BlockSpec((ts, C), lambda i: (i, col))
    halo = lambda col: pl.BlockSpec((CONV_HALO, C), lambda i: (jnp.maximum(i * hb - 1, 0), col))
    vec = pl.BlockSpec((1, C), lambda i: (0, 0))
    return pl.pallas_call(
        body, name=name, grid=(S // ts,),
        in_specs=[row(0), row(1), halo(0), halo(1), row(2), pl.BlockSpec((K, C), lambda i: (0, 0)), vec, vec, vec],
        out_specs=[row(0), row(0)],
        out_shape=[jax.ShapeDtypeStruct((S, C), BF16), jax.ShapeDtypeStruct((S, C), F32)],
        scratch_shapes=[pltpu.VMEM((nrows, C), F32), pltpu.VMEM((SUBLANES, nrows, C), F32)],
        compiler_params=_params("parallel"),
    )(proj, proj, proj, proj, proj, dw, dw_b, ln_g, ln_b)


def _conv_bwd_norm(dy, proj, u1, ln_g, ln_b, name):
    S, C = dy.shape
    ts = _tile(S, ROW_TILE)
    groups = CONV_CHUNK // SUBLANES

    def fold(x):
        return jnp.sum(x.reshape(groups, SUBLANES, C), axis=0)

    def body(dy_ref, g_ref, u1_ref, lg_ref, lb_ref, du1_ref, dg_ref, sums_ref):
        @pl.when(pl.program_id(0) == 0)
        def _():
            sums_ref[...] = jnp.zeros_like(sums_ref)

        def chunk(r0):
            rows = pl.ds(r0, CONV_CHUNK)
            d, g, u1 = dy_ref[rows, :], g_ref[rows, :], u1_ref[rows, :]
            mu = jnp.mean(u1, axis=-1, keepdims=True)
            xc = u1 - mu
            rstd = lax.rsqrt(jnp.mean(xc * xc, axis=-1, keepdims=True) + LN_EPS)
            xh = xc * rstd
            z = xh * lg_ref[...] + lb_ref[...]
            sz, sg = _sigmoid(z), _sigmoid(g)
            dgate = d * (z * sz) * _dsilu(g, sg)
            dz = d * (g * sg) * _dsilu(z, sz)
            dxh = dz * lg_ref[...]
            du1 = rstd * (dxh - jnp.mean(dxh, axis=-1, keepdims=True)
                          - xh * jnp.mean(dxh * xh, axis=-1, keepdims=True))
            du1_ref[rows, :] = du1
            dg_ref[rows, :] = dgate.astype(BF16)
            sums_ref[0] += fold(dz * xh)
            sums_ref[1] += fold(dz)
            sums_ref[2] += fold(du1)
            sums_ref[3] += fold(dgate)
        _rows(ts, CONV_CHUNK, chunk)

    row = pl.BlockSpec((ts, C), lambda i: (i, 0))
    vec = pl.BlockSpec((1, C), lambda i: (0, 0))
    return pl.pallas_call(
        body, name=name, grid=(S // ts,),
        in_specs=[row, pl.BlockSpec((ts, C), lambda i: (i, 2)), row, vec, vec],
        out_specs=[row, row, pl.BlockSpec((4, SUBLANES, C), lambda i: (0, 0, 0))],
        out_shape=[jax.ShapeDtypeStruct((S, C), F32), jax.ShapeDtypeStruct((S, C), BF16),
                   jax.ShapeDtypeStruct((4, SUBLANES, C), F32)],
        compiler_params=_params("arbitrary"),
    )(dy, proj, u1, ln_g, ln_b)


def _conv_bwd_taps(du1, proj, dw, name):
    S, C = du1.shape
    K = dw.shape[0]
    ts = _tile(S, CONV_ROWS)
    hb = ts // CONV_HALO
    nblk = S // ts
    last_halo = S // CONV_HALO - 1
    nrows = ts + CONV_HALO
    groups = CONV_CHUNK // SUBLANES

    def body(d_ref, dn_ref, a_ref, b_ref, dw_ref, da_ref, db_ref, ddw_ref, sums_ref, dbuf, dsh):
        i = pl.program_id(0)

        @pl.when(i == 0)
        def _():
            ddw_ref[...] = jnp.zeros_like(ddw_ref)
            sums_ref[...] = jnp.zeros_like(sums_ref)

        dbuf[ts:nrows, :] = jnp.where(i == nblk - 1, 0.0, dn_ref[...])

        def fill(r0):
            rows = pl.ds(r0, CONV_CHUNK)
            dbuf[rows, :] = d_ref[rows, :]
        _rows(ts, CONV_CHUNK, fill)
        _shifted_copies(dbuf, dsh, nrows - SUBLANES)

        zero = jnp.zeros((SUBLANES, LANES), F32)
        fold = lambda x: jnp.sum(x.reshape(groups, SUBLANES, LANES), axis=0)
        for s in range(C // LANES):
            cols = slice(s * LANES, (s + 1) * LANES)
            w = _tap_weights(dw_ref, cols)

            def step(j, carry, w=w, cols=cols):
                r0 = pl.multiple_of(j * CONV_CHUNK, CONV_CHUNK)
                rows = pl.ds(r0, CONV_CHUNK)
                a, sb = a_ref[rows, cols], _sigmoid(b_ref[rows, cols])
                u0 = a * sb
                accs = list(carry[2:])
                parts = []
                for u in range(groups):
                    u0_u = u0[u * SUBLANES:(u + 1) * SUBLANES]
                    acc = zero
                    for k in range(K):
                        x = _tap(dbuf, dsh, r0 + u * SUBLANES, K - 1 - k, cols)
                        acc = acc + w[k] * x
                        accs[k] = accs[k] + u0_u * x
                    parts.append(acc)
                da = jnp.concatenate(parts, axis=0) * sb
                db = da * a * (1.0 - sb)
                da_ref[rows, cols] = da.astype(BF16)
                db_ref[rows, cols] = db.astype(BF16)
                return (carry[0] + fold(da), carry[1] + fold(db), *accs)
            out = lax.fori_loop(0, ts // CONV_CHUNK, step, (zero,) * (K + 2))
            sums_ref[0, :, cols] += out[0]
            sums_ref[1, :, cols] += out[1]
            for k in range(K):
                ddw_ref[k, :, cols] += out[2 + k]

    row = lambda col: pl.BlockSpec((ts, C), lambda i: (i, col))
    nxt = pl.BlockSpec((CONV_HALO, C), lambda i: (jnp.minimum((i + 1) * hb, last_halo), 0))
    return pl.pallas_call(
        body, name=name, grid=(nblk,),
        in_specs=[row(0), nxt, row(0), row(1), pl.BlockSpec((K, C), lambda i: (0, 0))],
        out_specs=[row(0), row(0), pl.BlockSpec((K, SUBLANES, C), lambda i: (0, 0, 0)),
                   pl.BlockSpec((2, SUBLANES, C), lambda i: (0, 0, 0))],
        out_shape=[jax.ShapeDtypeStruct((S, C), BF16), jax.ShapeDtypeStruct((S, C), BF16),
                   jax.ShapeDtypeStruct((K, SUBLANES, C), F32), jax.ShapeDtypeStruct((2, SUBLANES, C), F32)],
        scratch_shapes=[pltpu.VMEM((nrows, C), F32), pltpu.VMEM((SUBLANES, nrows, C), F32)],
        compiler_params=_params("arbitrary"),
    )(du1, du1, proj, proj, dw)


def _adamw(parts, w, m, v, name):
    R, C = w.shape
    tr = _tile(R, 256)
    c1 = 1.0 - ADAM_B1 ** ADAM_STEP
    c2 = 1.0 - ADAM_B2 ** ADAM_STEP

    def body(p_ref, w_ref, m_ref, v_ref, g_ref, d_ref, nm_ref, nv_ref):
        g = p_ref[0].astype(F32)
        for d in range(1, N_DEV):
            g = g + p_ref[d].astype(F32)
        nm = ADAM_B1 * m_ref[...] + (1.0 - ADAM_B1) * g
        nv = ADAM_B2 * v_ref[...] + (1.0 - ADAM_B2) * (g * g)
        g_ref[...] = g
        nm_ref[...] = nm
        nv_ref[...] = nv
        d_ref[...] = -ADAM_LR * ((nm / c1) / (jnp.sqrt(nv / c2) + ADAM_EPS) + ADAM_WD * w_ref[...])

    row = pl.BlockSpec((tr, C), lambda i: (i, 0))
    out = jax.ShapeDtypeStruct((R, C), F32)
    return pl.pallas_call(
        body, name=name, grid=(R // tr,),
        in_specs=[pl.BlockSpec((N_DEV, tr, C), lambda i: (0, i, 0)), row, row, row],
        out_specs=[row, row, row, row], out_shape=[out, out, out, out],
        compiler_params=_params("parallel"),
    )(parts, w, m, v)


def _pad_lanes(a, width=LANES):
    return jnp.pad(a, ((0, 0), (0, width - a.shape[1])))


def _flat_rows(parts, width=LANES):
    flat = jnp.concatenate([p.reshape(-1) for p in parts])
    rows = -(-flat.shape[0] // width)
    rows = -(-rows // SUBLANES) * SUBLANES
    return jnp.pad(flat, (0, rows * width - flat.shape[0])).reshape(rows, width)


def _unflat(rows2d, shapes):
    flat = rows2d.reshape(-1)
    out, pos = [], 0
    for s in shapes:
        n = int(np.prod(s))
        out.append(flat[pos:pos + n].reshape(s))
        pos += n
    return out


def _pack_shards(parts, axes):
    rows = []
    for a, ax in zip(parts, axes):
        sh = a.shape
        a = a.reshape(sh[:ax] + (N_DEV, sh[ax] // N_DEV) + sh[ax + 1:])
        rows.append(jnp.moveaxis(a, ax, 0).reshape(N_DEV, -1))
    return jnp.concatenate(rows, axis=1)


def _unpack_shards(packed, shard_shapes, axes):
    out, pos = [], 0
    for s, ax in zip(shard_shapes, axes):
        n = int(np.prod(s))
        a = jnp.moveaxis(packed[:, pos:pos + n].reshape((N_DEV,) + tuple(s)), 0, ax)
        out.append(a.reshape(tuple(s[:ax]) + (N_DEV * s[ax],) + tuple(s[ax + 1:])))
        pos += n
    return out


def kernel(x, norm_g, fox_w_in, fox_b_f, fox_w_out, conv_w_in, conv_b_in, conv_dw, conv_dw_b, conv_ln_g, conv_ln_b, conv_w_out, final_norm_g, loss_target, m_norm_g, m_fox_w_in, m_fox_b_f, m_fox_w_out, m_conv_w_in, m_conv_b_in, m_conv_dw, m_conv_dw_b, m_conv_ln_g, m_conv_ln_b, m_conv_w_out, m_final_norm_g, v_norm_g, v_fox_w_in, v_fox_b_f, v_fox_w_out, v_conv_w_in, v_conv_b_in, v_conv_dw, v_conv_dw_b, v_conv_ln_g, v_conv_ln_b, v_conv_w_out, v_final_norm_g):
    h0 = x[0]
    target = loss_target[0]
    S, D = h0.shape
    depth = norm_g.shape[0]
    n_fox, _, fin_shard = fox_w_in.shape
    n_conv, _, cin_shard = conv_w_in.shape
    heads = fox_b_f.shape[1]
    W = fox_w_out.shape[1] * N_DEV
    C = conv_w_out.shape[1] * N_DEV
    assert fin_shard * N_DEV == 4 * W + heads and cin_shard * N_DEV == 3 * C and heads <= LANES
    is_fox = lambda i: i % 2 == 0

    shards = {}
    for i in range(depth):
        j = i // 2
        w_in, w_out = (fox_w_in, fox_w_out) if is_fox(i) else (conv_w_in, conv_w_out)
        shards[i] = (w_in[j].astype(BF16), w_out[j].astype(BF16))
    split = (5 * D) // 8

    def carried(*args, carry=None, **kw):
        r = _matmul(*args, carry=carry, **kw)
        return r if carry is not None else (r, None)

    small_shapes = [conv_b_in.shape, conv_dw.shape, conv_dw_b.shape, conv_ln_g.shape, conv_ln_b.shape]
    small = _flat_rows([conv_b_in, conv_dw, conv_dw_b, conv_ln_g, conv_ln_b])
    small_all = _exchange(small, "ag_small", "gather")
    small_axes = [1, 2, 1, 1, 1]
    conv_b_in_f, conv_dw_f, conv_dw_b_f, conv_ln_g_f, conv_ln_b_f = _unpack_shards(
        small_all.reshape(N_DEV, -1), small_shapes, small_axes)

    gathered_in = {0: _exchange(shards[0][0], "ag_w_in0", "gather")}
    gathered_out = {}

    h = h0
    saved = []
    for i in range(depth):
        j = i // 2
        has_next = i + 1 < depth
        gather = lambda x, kind="gather": (x, kind) if has_next else None
        hn, hn_t = _rms_fwd(h, norm_g[i:i + 1], f"rms_fwd{i}")
        w = gathered_in.pop(i)
        if is_fox(i):
            w = jnp.transpose(w, (1, 0, 2)).reshape(D, -1)
            w_qkvg, w_f = w[:, :4 * W], _pad_lanes(w[:, 4 * W:])
            qkv, got_in = carried(hn, w_qkvg, "nn", BF16, f"fox_qkv{i}", n_out=3 * W,
                                  carry=gather(shards[i + 1][0] if has_next else None, "gather_cols"))
            gate, got = carried(hn, w_qkvg, "nn", F32, f"fox_gate{i}", b_col_off=3 * W, n_out=W,
                                carry=(shards[i][1], "gather"))
            w_out = got.reshape(W, D)
            f = _matmul(hn, w_f, "nn", F32, f"fox_f{i}")
            b_f = _pad_lanes(fox_b_f[j:j + 1])
            c = _fgate_fwd(f, b_f, f"fgate_fwd{i}")
            key_bias = lambda t: (-c[:, :heads]).T.reshape(heads, S // t, 1, t)
            y, o, lse = _flash_fwd(qkv, gate, key_bias(_tile(S, FLASH_FWD_TILE)), heads, f"flash_fwd{i}")
            cneg = key_bias(_tile(S, FLASH_BWD_TILE))
            h_next, got_out = carried(y, w_out, "nn", F32, f"out_proj{i}", add=h,
                                      carry=gather(shards[i + 1][1] if has_next else None))
            if has_next:
                gathered_in[i + 1], gathered_out[i + 1] = got_in, got_out
            saved.append(dict(h=h, hn_t=hn_t, qkv=qkv, gate=gate, f=f, b_f=b_f, cneg=cneg, y=y, o=o, lse=lse,
                              w_qkvg=w_qkvg, w_f=w_f, w_out=w_out))
        else:
            w_out = gathered_out.pop(i).reshape(C, D)
            nxt = shards[i + 1][0] if has_next else None
            proj, got_a = carried(hn, w, "nn", F32, f"conv_in{i}", bias=conv_b_in_f[j:j + 1],
                                  carry=gather(nxt[:split] if has_next else None))
            y, u1 = _conv_fwd(proj, conv_dw_f[j], conv_dw_b_f[j:j + 1], conv_ln_g_f[j:j + 1],
                              conv_ln_b_f[j:j + 1], f"conv_fwd{i}")
            h_next, got_b = carried(y, w_out, "nn", F32, f"out_proj{i}", add=h,
                                    carry=gather(nxt[split:] if has_next else None))
            if has_next:
                gathered_in[i + 1] = jnp.concatenate([got_a, got_b], axis=1)
            saved.append(dict(h=h, hn_t=hn_t, proj=proj, y=y, u1=u1, w_in=w, w_out=w_out))
        h = h_next

    dh, dh16, dh16_t, loss_part, dg_final = _loss_head(h, final_norm_g[None, :], target, "loss_head")

    def shard_cols(g, shard):
        return jnp.transpose(g.reshape(g.shape[0], N_DEV, shard), (1, 0, 2))

    d_norm_g = [None] * depth
    d_fox_b_f = [None] * n_fox
    d_conv_small = [None] * n_conv
    summed_in = [None] * depth
    summed_out = [None] * depth
    pend_in = pend_out = None
    early_in = None
    for i in reversed(range(depth)):
        j = i // 2
        sv = saved[i]
        scatter = lambda x, kind="scatter": (x, kind) if x is not None else None
        dy = _matmul(dh16, sv["w_out"], "nt", F32, f"d_out_proj{i}")
        dw_out_t, got = carried(dh16_t, sv["y"], "nn", F32, f"dw_out{i}", carry=scatter(pend_out))
        if pend_out is not None:
            summed_out[i + 1] = got
        own_out = dw_out_t.T.reshape(N_DEV, -1, D).astype(BF16)
        if is_fox(i):
            do, dgate, delta = _fox_gate_bwd(dy, sv["o"], sv["gate"], f"fox_gate_bwd{i}")
            dq, dk, dv, colsum, rowsum = _flash_bwd(sv["qkv"], do, delta, sv["lse"], sv["cneg"], heads,
                                                    f"flash_bwd{i}")
            dc = _pad_lanes(rowsum[:, :heads] - jnp.sum(colsum, axis=2).reshape(heads, S).T)
            df, dbf = _fgate_bwd(dc, sv["f"], sv["b_f"], f"fgate_bwd{i}")
            dproj = jnp.concatenate([dq, dk, dv, dgate], axis=1)
            dhn = _matmul(df, sv["w_f"], "nt", F32, f"d_fox_f{i}")
            dhn, got = carried(dproj, sv["w_qkvg"], "nt", F32, f"d_fox_in{i}", add=dhn,
                               carry=scatter(pend_in, "scatter_cols"))
            if pend_in is not None:
                summed_in[i + 1] = got
            dw_f = _matmul(sv["hn_t"], df, "nn", F32, f"dw_fox_f{i}")[:, :heads]
            parts_of = lambda dw, rows: shard_cols(jnp.concatenate([dw, dw_f[rows]], axis=1), fin_shard).astype(BF16)
            if i > 0:
                dw_qkvg, summed_out[i] = carried(sv["hn_t"], dproj, "nn", F32, f"dw_fox_in{i}",
                                                 carry=(own_out, "scatter"))
                pend_in = parts_of(dw_qkvg, slice(None))
            else:
                top, bottom = slice(0, D // 2), slice(D // 2, D)
                dw_top, summed_out[i] = carried(sv["hn_t"][top], dproj, "nn", F32, f"dw_fox_in{i}",
                                                carry=(own_out, "scatter"))
                dw_bottom, early_in = carried(sv["hn_t"][bottom], dproj, "nn", F32, f"dw_fox_in{i}_rest",
                                              carry=(parts_of(dw_top, top), "scatter"))
                pend_in = parts_of(dw_bottom, bottom)
            pend_out = None
            d_fox_b_f[j] = jnp.sum(dbf, axis=0)[:heads]
        else:
            du1, dgate, nsums = _conv_bwd_norm(dy, sv["proj"], sv["u1"], conv_ln_g_f[j:j + 1],
                                               conv_ln_b_f[j:j + 1], f"conv_bwd_norm{i}")
            da, db, ddw, absums = _conv_bwd_taps(du1, sv["proj"], conv_dw_f[j], f"conv_bwd_taps{i}")
            dproj = jnp.concatenate([da, db, dgate], axis=1)
            half = D // 2
            dhn, got_a = carried(dproj, sv["w_in"], "nt", F32, f"d_conv_in{i}",
                                 carry=scatter(pend_in[:, :half] if pend_in is not None else None))
            dw_in, got_b = carried(sv["hn_t"], dproj, "nn", BF16, f"dw_conv_in{i}",
                                   carry=scatter(pend_in[:, half:] if pend_in is not None else None))
            if pend_in is not None:
                summed_in[i + 1] = jnp.concatenate([got_a, got_b], axis=1)
            pend_in, pend_out = dw_in, own_out
            nsum = jnp.sum(nsums, axis=1)
            absum = jnp.sum(absums, axis=1)
            d_conv_small[j] = dict(b_in=jnp.concatenate([absum[0], absum[1], nsum[3]]),
                                   dw=jnp.sum(ddw, axis=1), dw_b=nsum[2], ln_g=nsum[0], ln_b=nsum[1])
        dh, dh16, dh16_t, dg = _rms_bwd(sv["h"], norm_g[i:i + 1], dhn, dh, f"rms_bwd{i}")
        d_norm_g[i] = jnp.sum(dg, axis=0)
    summed_in[0] = _exchange(pend_in, "rs_w_in0", "scatter")
    if early_in is not None:
        summed_in[0] = jnp.concatenate([early_in, summed_in[0]], axis=1)
    if pend_out is not None:
        summed_out[0] = _exchange(pend_out, "rs_w_out0", "scatter")
    grad_x = dh[None]

    p_small = _pack_shards([jnp.stack([s[name] for s in d_conv_small]) for name in ("b_in", "dw", "dw_b", "ln_g", "ln_b")],
                           small_axes)
    p_small = jnp.pad(p_small, ((0, 0), (0, small.size - p_small.shape[1]))).reshape((N_DEV,) + small.shape)
    r_small = _exchange(p_small, "rs_small", "scatter")

    rep_shapes = [norm_g.shape, fox_b_f.shape, final_norm_g.shape, (1,)]
    rep_part = _flat_rows([jnp.stack(d_norm_g), jnp.stack(d_fox_b_f), jnp.sum(dg_final, axis=0),
                           jnp.sum(loss_part[:, 0])[None]])
    r_rep = _exchange(rep_part, "ag_replicated", "gather")

    def update(parts, w, m, v, name):
        two_d = (-1, w.shape[-1])
        stacked = jnp.stack(parts, axis=1).reshape((N_DEV,) + w.reshape(two_d).shape)
        res = _adamw(stacked, w.reshape(two_d), m.reshape(two_d), v.reshape(two_d), name)
        return [r.reshape(w.shape) for r in res]

    fox_layers = [i for i in range(depth) if is_fox(i)]
    conv_layers = [i for i in range(depth) if not is_fox(i)]
    u_fin = update([summed_in[i] for i in fox_layers], fox_w_in, m_fox_w_in, v_fox_w_in, "adamw_fox_w_in")
    u_fout = update([summed_out[i] for i in fox_layers], fox_w_out, m_fox_w_out, v_fox_w_out, "adamw_fox_w_out")
    u_cin = update([summed_in[i] for i in conv_layers], conv_w_in, m_conv_w_in, v_conv_w_in, "adamw_conv_w_in")
    u_cout = update([summed_out[i] for i in conv_layers], conv_w_out, m_conv_w_out, v_conv_w_out,
                    "adamw_conv_w_out")
    u_small = _adamw(r_small, small,
                     _flat_rows([m_conv_b_in, m_conv_dw, m_conv_dw_b, m_conv_ln_g, m_conv_ln_b]),
                     _flat_rows([v_conv_b_in, v_conv_dw, v_conv_dw_b, v_conv_ln_g, v_conv_ln_b]), "adamw_small")
    zero1 = jnp.zeros((1,), F32)
    u_rep = _adamw(r_rep, _flat_rows([norm_g, fox_b_f, final_norm_g, zero1]),
                   _flat_rows([m_norm_g, m_fox_b_f, m_final_norm_g, zero1]),
                   _flat_rows([v_norm_g, v_fox_b_f, v_final_norm_g, zero1]), "adamw_replicated")

    outs = []
    loss = None
    for kind in range(4):
        b_in_k, dw_k, dwb_k, lng_k, lnb_k = _unflat(u_small[kind], small_shapes)
        ng_k, bf_k, fg_k, loss_k = _unflat(u_rep[kind], rep_shapes)
        if kind == 0:
            loss = loss_k[0]
        outs += [ng_k, u_fin[kind], bf_k, u_fout[kind], u_cin[kind], b_in_k, dw_k, dwb_k, lng_k, lnb_k,
                 u_cout[kind], fg_k]
    return (loss, grad_x, *outs)
```

```python
import functools

import numpy as np
import jax
import jax.numpy as jnp
from jax import lax
from jax.experimental import pallas as pl
from jax.experimental.pallas import tpu as pltpu

F32 = jnp.float32
BF16 = jnp.bfloat16
MESH_ID = pl.DeviceIdType.MESH

N_DEV = 8
RMS_EPS = 1e-6
LN_EPS = 1e-5
ADAM_LR = 0.001
ADAM_B1 = 0.9
ADAM_B2 = 0.999
ADAM_EPS = 1e-08
ADAM_WD = 0.01
ADAM_STEP = 10

LANES = 128
SUBLANES = 8
VMEM_LIMIT = 60 * 1024 * 1024
NEG_BIG = -1e30
CONV_HALO = 32
FLASH_FWD_TILE = 1024
FLASH_BWD_TILE = 1024
CONV_ROWS = 128
CONV_CHUNK = 32
FGATE_ROWS = 256
ROW_TILE = 256


def _params(*sem):
    return pltpu.CompilerParams(dimension_semantics=sem if sem else None, vmem_limit_bytes=VMEM_LIMIT)


def _tile(n, pref):
    if n <= pref:
        return n
    t = pref
    while n % t:
        t //= 2
    return t


def _sigmoid(x):
    return 1.0 / (1.0 + jnp.exp(-x))


def _dsilu(x, s):
    return s * (1.0 + x * (1.0 - s))


def _rows(n, chunk, fn):
    def step(i, carry):
        fn(pl.multiple_of(i * chunk, chunk))
        return carry
    lax.fori_loop(0, n // chunk, step, 0)


def _peer(k):
    x, y, c = lax.axis_index("x"), lax.axis_index("y"), lax.axis_index("c")
    px = 1 - x if (k >> 2) & 1 else x
    py = 1 - y if (k >> 1) & 1 else y
    pc = 1 - c if k & 1 else c
    return (px, py, pc), 4 * px + 2 * py + pc


def _exchange_copies(x_ref, o_ref, send_sems, recv_sems, local_sem, kind):
    _, me = _peer(0)

    def window(ref, idx, n):
        assert n % LANES == 0
        return ref.at[:, pl.ds(pl.multiple_of(idx * n, LANES), n)]

    def src(idx):
        if kind == "scatter":
            return x_ref.at[idx]
        return window(x_ref, idx, o_ref.shape[2]) if kind == "scatter_cols" else x_ref

    def dst(idx):
        return window(o_ref, idx, x_ref.shape[1]) if kind == "gather_cols" else o_ref.at[idx]

    local = pltpu.make_async_copy(src(me), dst(me), local_sem)
    sends, arrivals = [], []
    for k in range(1, N_DEV):
        peer, pidx = _peer(k)
        sems = dict(send_sem=send_sems.at[k - 1], recv_sem=recv_sems.at[k - 1], device_id=peer,
                    device_id_type=MESH_ID)
        sends.append(pltpu.make_async_remote_copy(src_ref=src(pidx), dst_ref=dst(me), **sems))
        arrivals.append(pltpu.make_async_remote_copy(src_ref=src(me), dst_ref=dst(pidx), **sems))
    return local, sends, arrivals


def _exchange_start(*refs, kind):
    local, sends, _ = _exchange_copies(*refs, kind)
    local.start()
    for cp in sends:
        cp.start()


def _exchange_wait(*refs, kind):
    local, sends, arrivals = _exchange_copies(*refs, kind)
    for cp in arrivals:
        cp.wait_recv()
    for cp in sends:
        cp.wait_send()
    local.wait()


EXCHANGE_SCRATCH = [pltpu.SemaphoreType.DMA((N_DEV - 1,)), pltpu.SemaphoreType.DMA((N_DEV - 1,)),
                    pltpu.SemaphoreType.DMA]


def _exchange_shape(x, kind):
    shape = {"gather": lambda: (N_DEV,) + x.shape, "scatter": lambda: x.shape,
             "gather_cols": lambda: (x.shape[0], N_DEV * x.shape[1]),
             "scatter_cols": lambda: (N_DEV, x.shape[0], x.shape[1] // N_DEV)}[kind]()
    return jax.ShapeDtypeStruct(shape, x.dtype)


def _exchange(x, name, kind):
    def body(*refs):
        _exchange_start(*refs, kind=kind)
        _exchange_wait(*refs, kind=kind)

    return pl.pallas_call(
        body, name=name,
        out_shape=_exchange_shape(x, kind),
        in_specs=[pl.BlockSpec(memory_space=pl.ANY)],
        out_specs=pl.BlockSpec(memory_space=pl.ANY),
        scratch_shapes=list(EXCHANGE_SCRATCH),
    )(x)


def _matmul(a, b, mode, out_dtype, name, bias=None, add=None, b_col_off=0, n_out=None, carry=None):
    M, K = a.shape
    N = n_out if n_out is not None else (b.shape[0] if mode == "nt" else b.shape[1])
    tm = _tile(M, 512)
    tn = _tile(N, 1024)
    k_cap = 4096 if a.dtype.itemsize == 2 and b.dtype.itemsize == 2 else 2048
    tk = next(K // d for d in range(1, K + 1) if K % d == 0 and K // d <= k_cap and (K // d) % LANES == 0)
    nm, nn, nk = M // tm, N // tn, K // tk
    assert b_col_off % tn == 0
    joff = b_col_off // tn
    dims = {"nn": (((1,), (0,)), ((), ())), "nt": (((1,), (1,)), ((), ()))}[mode]
    n_in = 2 + (bias is not None) + (add is not None)

    def body(*refs):
        a_ref, b_ref = refs[0], refs[1]
        bias_ref = refs[2] if bias is not None else None
        add_ref = refs[n_in - 1] if add is not None else None
        pos = n_in
        x_ref = o_ref = x_out_ref = None
        if carry is not None:
            x_ref, o_ref, x_out_ref = refs[pos], refs[pos + 1], refs[pos + 2]
            pos += 3
        else:
            o_ref = refs[pos]
            pos += 1
        acc_ref = None
        if nk > 1:
            acc_ref = refs[pos]
            pos += 1
        exchange_refs = (x_ref, x_out_ref) + tuple(refs[pos:])
        i, j, kk = pl.program_id(0), pl.program_id(1), pl.program_id(2)

        if carry is not None:
            @pl.when((i == 0) & (j == 0) & (kk == 0))
            def _():
                _exchange_start(*exchange_refs, kind=carry[1])

        part = lax.dot_general(a_ref[...].astype(BF16), b_ref[...].astype(BF16), dims,
                               preferred_element_type=F32)

        def finish(r):
            if bias_ref is not None:
                r = r + bias_ref[...]
            if add_ref is not None:
                r = r + add_ref[...]
            o_ref[...] = r.astype(o_ref.dtype)

        if nk == 1:
            finish(part)
        else:
            @pl.when(kk == 0)
            def _():
                acc_ref[...] = part

            @pl.when(kk > 0)
            def _():
                acc_ref[...] += part

            @pl.when(kk == nk - 1)
            def _():
                finish(acc_ref[...])

        if carry is not None:
            @pl.when((i == nm - 1) & (j == nn - 1) & (kk == nk - 1))
            def _():
                _exchange_wait(*exchange_refs, kind=carry[1])

    a_spec = pl.BlockSpec((tm, tk), lambda i, j, k: (i, k))
    if mode == "nt":
        b_spec = pl.BlockSpec((tn, tk), lambda i, j, k: (j, k))
    else:
        b_spec = pl.BlockSpec((tk, tn), lambda i, j, k: (k, j + joff))
    in_specs = [a_spec, b_spec]
    args = [a, b]
    if bias is not None:
        in_specs.append(pl.BlockSpec((1, tn), lambda i, j, k: (0, j)))
        args.append(bias)
    if add is not None:
        in_specs.append(pl.BlockSpec((tm, tn), lambda i, j, k: (i, j)))
        args.append(add)
    out_specs = [pl.BlockSpec((tm, tn), lambda i, j, k: (i, j))]
    out_shape = [jax.ShapeDtypeStruct((M, N), out_dtype)]
    scratch = [pltpu.VMEM((tm, tn), F32)] if nk > 1 else []
    if carry is not None:
        in_specs.append(pl.BlockSpec(memory_space=pl.ANY))
        args.append(carry[0])
        out_specs.append(pl.BlockSpec(memory_space=pl.ANY))
        out_shape.append(_exchange_shape(*carry))
        scratch += EXCHANGE_SCRATCH
    res = pl.pallas_call(
        body, name=name,
        grid=(nm, nn, nk),
        in_specs=in_specs, out_specs=out_specs, out_shape=out_shape, scratch_shapes=scratch,
        compiler_params=_params(*(("arbitrary",) * 3 if carry is not None else ("parallel", "parallel", "arbitrary"))),
    )(*args)
    return res if carry is not None else res[0]


def _rms_fwd(h, g, name):
    S, D = h.shape
    ts = _tile(S, 512)

    def body(h_ref, g_ref, o_ref, ot_ref):
        x = h_ref[...]
        r = lax.rsqrt(jnp.mean(x * x, axis=-1, keepdims=True) + RMS_EPS)
        y = x * r * g_ref[...]
        o_ref[...] = y.astype(BF16)
        ot_ref[...] = y.T.astype(BF16)

    return pl.pallas_call(
        body, name=name, grid=(S // ts,),
        in_specs=[pl.BlockSpec((ts, D), lambda i: (i, 0)), pl.BlockSpec((1, D), lambda i: (0, 0))],
        out_specs=[pl.BlockSpec((ts, D), lambda i: (i, 0)), pl.BlockSpec((D, ts), lambda i: (0, i))],
        out_shape=[jax.ShapeDtypeStruct((S, D), BF16), jax.ShapeDtypeStruct((D, S), BF16)],
        compiler_params=_params("parallel"),
    )(h, g)


def _rms_bwd_block(x, g, dy):
    r = lax.rsqrt(jnp.mean(x * x, axis=-1, keepdims=True) + RMS_EPS)
    xr = x * r
    t = dy * g
    dx = r * (t - xr * jnp.mean(t * xr, axis=-1, keepdims=True))
    return dx, dy * xr


def _rms_bwd(h, g, dhn, dh, name):
    S, D = h.shape
    ts = _tile(S, ROW_TILE)

    def body(h_ref, g_ref, dhn_ref, dh_ref, o_ref, o16_ref, o16t_ref, dg_ref):
        dx, dgt = _rms_bwd_block(h_ref[...], g_ref[...], dhn_ref[...])
        out = dh_ref[...] + dx
        o_ref[...] = out
        o16_ref[...] = out.astype(BF16)
        o16t_ref[...] = out.T.astype(BF16)
        part = jnp.sum(dgt.reshape(ts // SUBLANES, SUBLANES, D), axis=0)

        @pl.when(pl.program_id(0) == 0)
        def _():
            dg_ref[...] = part

        @pl.when(pl.program_id(0) > 0)
        def _():
            dg_ref[...] += part

    row = pl.BlockSpec((ts, D), lambda i: (i, 0))
    return pl.pallas_call(
        body, name=name, grid=(S // ts,),
        in_specs=[row, pl.BlockSpec((1, D), lambda i: (0, 0)), row, row],
        out_specs=[row, row, pl.BlockSpec((D, ts), lambda i: (0, i)),
                   pl.BlockSpec((SUBLANES, D), lambda i: (0, 0))],
        out_shape=[jax.ShapeDtypeStruct((S, D), F32), jax.ShapeDtypeStruct((S, D), BF16),
                   jax.ShapeDtypeStruct((D, S), BF16), jax.ShapeDtypeStruct((SUBLANES, D), F32)],
        compiler_params=_params("arbitrary"),
    )(h, g, dhn, dh)


def _loss_head(h, g, target, name):
    S, D = h.shape
    ts = _tile(S, ROW_TILE)

    def body(h_ref, g_ref, t_ref, o_ref, o16_ref, o16t_ref, loss_ref, dg_ref):
        x = h_ref[...]
        gg = g_ref[...]
        r = lax.rsqrt(jnp.mean(x * x, axis=-1, keepdims=True) + RMS_EPS)
        err = x * r * gg - t_ref[...]
        row_loss = 0.5 * jnp.mean(err * err, axis=-1, keepdims=True)
        dx, dgt = _rms_bwd_block(x, gg, err * (1.0 / D))
        o_ref[...] = dx
        o16_ref[...] = dx.astype(BF16)
        o16t_ref[...] = dx.T.astype(BF16)
        part = jnp.sum(dgt.reshape(ts // SUBLANES, SUBLANES, D), axis=0)
        lpart = jnp.sum(jnp.broadcast_to(row_loss, (ts, LANES)).reshape(ts // SUBLANES, SUBLANES, LANES), axis=0)

        @pl.when(pl.program_id(0) == 0)
        def _():
            dg_ref[...] = part
            loss_ref[...] = lpart

        @pl.when(pl.program_id(0) > 0)
        def _():
            dg_ref[...] += part
            loss_ref[...] += lpart

    row = pl.BlockSpec((ts, D), lambda i: (i, 0))
    return pl.pallas_call(
        body, name=name, grid=(S // ts,),
        in_specs=[row, pl.BlockSpec((1, D), lambda i: (0, 0)), row],
        out_specs=[row, row, pl.BlockSpec((D, ts), lambda i: (0, i)),
                   pl.BlockSpec((SUBLANES, LANES), lambda i: (0, 0)), pl.BlockSpec((SUBLANES, D), lambda i: (0, 0))],
        out_shape=[jax.ShapeDtypeStruct((S, D), F32), jax.ShapeDtypeStruct((S, D), BF16),
                   jax.ShapeDtypeStruct((D, S), BF16), jax.ShapeDtypeStruct((SUBLANES, LANES), F32),
                   jax.ShapeDtypeStruct((SUBLANES, D), F32)],
        compiler_params=_params("arbitrary"),
    )(h, g, target)


def _split3(x):
    hi = x.astype(BF16)
    r1 = x - hi.astype(F32)
    mid = r1.astype(BF16)
    lo = (r1 - mid.astype(F32)).astype(BF16)
    return hi, mid, lo


def _tri_sum(tri, x):
    hi, mid, lo = _split3(x)
    dot = functools.partial(jnp.dot, preferred_element_type=F32)
    return dot(tri, hi) + dot(tri, mid) + dot(tri, lo)


def _fgate_fwd(f, b_f, name):
    S = f.shape[0]
    tb = _tile(S, FGATE_ROWS)

    def body(f_ref, b_ref, c_ref, carry_ref):
        @pl.when(pl.program_id(0) == 0)
        def _():
            carry_ref[...] = jnp.zeros_like(carry_ref)

        x = f_ref[...] + b_ref[...]
        lf = jnp.minimum(x, 0.0) - jnp.log1p(jnp.exp(-jnp.abs(x)))
        r = lax.broadcasted_iota(jnp.int32, (tb, tb), 0)
        c = lax.broadcasted_iota(jnp.int32, (tb, tb), 1)
        tri = (c <= r).astype(BF16)
        c_ref[...] = _tri_sum(tri, lf) + carry_ref[0:1, :]
        carry_ref[...] += _tri_sum(jnp.ones((SUBLANES, tb), BF16), lf)

    return pl.pallas_call(
        body, name=name, grid=(S // tb,),
        in_specs=[pl.BlockSpec((tb, LANES), lambda i: (i, 0)), pl.BlockSpec((1, LANES), lambda i: (0, 0))],
        out_specs=pl.BlockSpec((tb, LANES), lambda i: (i, 0)),
        out_shape=jax.ShapeDtypeStruct((S, LANES), F32),
        scratch_shapes=[pltpu.VMEM((SUBLANES, LANES), F32)],
        compiler_params=_params("arbitrary"),
    )(f, b_f)


def _fgate_bwd(dc, f, b_f, name):
    S = f.shape[0]
    tb = _tile(S, FGATE_ROWS)
    nb = S // tb

    def body(dc_ref, f_ref, b_ref, df_ref, db_ref, carry_ref):
        @pl.when(pl.program_id(0) == 0)
        def _():
            carry_ref[...] = jnp.zeros_like(carry_ref)
            db_ref[...] = jnp.zeros_like(db_ref)

        d = dc_ref[...]
        r = lax.broadcasted_iota(jnp.int32, (tb, tb), 0)
        c = lax.broadcasted_iota(jnp.int32, (tb, tb), 1)
        tri = (c >= r).astype(BF16)
        dlf = _tri_sum(tri, d) + carry_ref[0:1, :]
        carry_ref[...] += _tri_sum(jnp.ones((SUBLANES, tb), BF16), d)
        df = dlf * _sigmoid(-(f_ref[...] + b_ref[...]))
        df_ref[...] = df
        db_ref[...] += jnp.sum(df.reshape(tb // SUBLANES, SUBLANES, LANES), axis=0)

    rev = pl.BlockSpec((tb, LANES), lambda i: (nb - 1 - i, 0))
    return pl.pallas_call(
        body, name=name, grid=(nb,),
        in_specs=[rev, rev, pl.BlockSpec((1, LANES), lambda i: (0, 0))],
        out_specs=[rev, pl.BlockSpec((SUBLANES, LANES), lambda i: (0, 0))],
        out_shape=[jax.ShapeDtypeStruct((S, LANES), F32), jax.ShapeDtypeStruct((SUBLANES, LANES), F32)],
        scratch_shapes=[pltpu.VMEM((SUBLANES, LANES), F32)],
        compiler_params=_params("arbitrary"),
    )(dc, f, b_f)


LOG2E = 1.4426950408889634
NT_DIMS = (((1,), (1,)), ((), ()))
TN_DIMS = (((0,), (0,)), ((), ()))


def _causal_mask(t):
    r = lax.broadcasted_iota(jnp.int32, (t, t), 0)
    c = lax.broadcasted_iota(jnp.int32, (t, t), 1)
    return r >= c


def _key_tiles(qi, tile, group):
    def several(i, carry):
        for u in range(group):
            tile(group * i + u, False)
        return carry
    lax.fori_loop(0, qi // group, several, 0)

    def single(j, carry):
        tile(j, False)
        return carry
    lax.fori_loop((qi // group) * group, qi, single, 0)
    tile(qi, True)


def _flash_fwd(qkv, gate, cneg, heads, name):
    S, W3 = qkv.shape
    W = W3 // 3
    dh = W // heads
    assert dh == LANES
    tq = _tile(S, FLASH_FWD_TILE)
    nq = S // tq
    c1 = dh ** -0.5 * LOG2E

    def body(q_ref, k_ref, v_ref, b_ref, g_ref, y_ref, o_ref, lse_ref, m_scr, l_scr, acc_scr):
        qi = pl.program_id(1)
        m_scr[...] = jnp.full_like(m_scr, NEG_BIG)
        l_scr[...] = jnp.zeros_like(l_scr)
        acc_scr[...] = jnp.zeros_like(acc_scr)
        q = q_ref[...]

        def tile(j, diagonal):
            rows = pl.ds(pl.multiple_of(j * tq, tq), tq)
            t = lax.dot_general(q, k_ref[rows, :], NT_DIMS, preferred_element_type=F32) * c1 + b_ref[0, j] * LOG2E
            if diagonal:
                t = jnp.where(_causal_mask(tq), t, NEG_BIG)
            m_prev = m_scr[...]
            m_next = jnp.maximum(m_prev, jnp.max(t, axis=1, keepdims=True))
            alpha = jnp.exp2(m_prev - m_next)
            p = jnp.exp2(t - m_next[:, :1])
            l_scr[...] = alpha * l_scr[...] + jnp.sum(p, axis=1, keepdims=True)
            acc_scr[...] = alpha * acc_scr[...] + jnp.dot(p.astype(BF16), v_ref[rows, :],
                                                          preferred_element_type=F32)
            m_scr[...] = m_next

        _key_tiles(qi, tile, 4)
        l = l_scr[...]
        o = acc_scr[...] / l
        g = g_ref[...]
        o_ref[...] = o
        lse_ref[...] = m_scr[...] + jnp.log2(l)
        y_ref[...] = (o * (g * _sigmoid(g))).astype(BF16)

    qblk = pl.BlockSpec((tq, dh), lambda h, i: (i, h))
    return pl.pallas_call(
        body, name=name, grid=(heads, nq),
        in_specs=[qblk,
                  pl.BlockSpec((S, dh), lambda h, i: (0, heads + h)),
                  pl.BlockSpec((S, dh), lambda h, i: (0, 2 * heads + h)),
                  pl.BlockSpec((1, nq, 1, tq), lambda h, i: (h, 0, 0, 0)),
                  qblk],
        out_specs=[qblk, qblk, qblk],
        out_shape=[jax.ShapeDtypeStruct((S, W), BF16), jax.ShapeDtypeStruct((S, W), F32),
                   jax.ShapeDtypeStruct((S, W), F32)],
        scratch_shapes=[pltpu.VMEM((tq, LANES), F32), pltpu.VMEM((tq, LANES), F32), pltpu.VMEM((tq, dh), F32)],
        compiler_params=_params("parallel", "arbitrary"),
    )(qkv, qkv, qkv, cneg, gate)


def _flash_bwd(qkv, do, delta, lse, cneg, heads, name):
    S, W3 = qkv.shape
    W = W3 // 3
    dh = W // heads
    tq = _tile(S, FLASH_BWD_TILE)
    nq = S // tq
    scale = dh ** -0.5
    c1 = scale * LOG2E

    def body(q_ref, k_ref, v_ref, b_ref, do_ref, delta_ref, lse_ref,
             dq_ref, dk_ref, dv_ref, db_ref, rs_ref, dq_scr, dk_scr, dv_scr, rs_scr):
        h, qi = pl.program_id(0), pl.program_id(1)

        @pl.when((h == 0) & (qi == 0))
        def _():
            rs_ref[...] = jnp.zeros_like(rs_ref)

        @pl.when(qi == 0)
        def _():
            dk_scr[...] = jnp.zeros_like(dk_scr)
            dv_scr[...] = jnp.zeros_like(dv_scr)
            db_ref[...] = jnp.zeros_like(db_ref)

        dq_scr[...] = jnp.zeros_like(dq_scr)
        rs_scr[...] = jnp.zeros_like(rs_scr)
        q, d_o = q_ref[...], do_ref[...]
        lse = lse_ref[...][:, :1]
        delta = delta_ref[...][:, :1]

        def tile(j, diagonal):
            rows = pl.ds(pl.multiple_of(j * tq, tq), tq)
            k, v = k_ref[rows, :], v_ref[rows, :]
            t = lax.dot_general(q, k, NT_DIMS, preferred_element_type=F32) * c1 + b_ref[0, j] * LOG2E
            p = jnp.exp2(t - lse)
            if diagonal:
                p = jnp.where(_causal_mask(tq), p, 0.0)
            dp = lax.dot_general(d_o, v, NT_DIMS, preferred_element_type=F32)
            ds = p * (dp - delta)
            dv_scr[rows, :] += lax.dot_general(p.astype(BF16), d_o, TN_DIMS, preferred_element_type=F32)
            db_ref[0, j] += jnp.sum(ds.reshape(tq // SUBLANES, SUBLANES, tq), axis=0)
            dsb = (ds * scale).astype(BF16)
            dk_scr[rows, :] += lax.dot_general(dsb, q, TN_DIMS, preferred_element_type=F32)
            dq_scr[...] += jnp.dot(dsb, k, preferred_element_type=F32)
            rs_scr[...] += jnp.sum(ds, axis=1, keepdims=True)

        _key_tiles(qi, tile, 4)
        dq_ref[...] = dq_scr[...].astype(BF16)
        lane = lax.broadcasted_iota(jnp.int32, (tq, LANES), 1)
        rs_ref[pl.ds(pl.multiple_of(qi * tq, tq), tq), :] += jnp.where(lane == h, rs_scr[...], 0.0)

        @pl.when(qi == nq - 1)
        def _():
            dk_ref[...] = dk_scr[...].astype(BF16)
            dv_ref[...] = dv_scr[...].astype(BF16)

    qblk = pl.BlockSpec((tq, dh), lambda h, i: (i, h))
    once = dict(pipeline_mode=pl.Buffered(1))
    head = pl.BlockSpec((S, dh), lambda h, i: (0, h), **once)
    return pl.pallas_call(
        body, name=name, grid=(heads, nq),
        in_specs=[qblk,
                  pl.BlockSpec((S, dh), lambda h, i: (0, heads + h), **once),
                  pl.BlockSpec((S, dh), lambda h, i: (0, 2 * heads + h), **once),
                  pl.BlockSpec((1, nq, 1, tq), lambda h, i: (h, 0, 0, 0)),
                  qblk, qblk, qblk],
        out_specs=[qblk, head, head,
                   pl.BlockSpec((1, nq, SUBLANES, tq), lambda h, i: (h, 0, 0, 0)),
                   pl.BlockSpec((S, LANES), lambda h, i: (0, 0), **once)],
        out_shape=[jax.ShapeDtypeStruct((S, W), BF16), jax.ShapeDtypeStruct((S, W), BF16),
                   jax.ShapeDtypeStruct((S, W), BF16), jax.ShapeDtypeStruct((heads, nq, SUBLANES, tq), F32),
                   jax.ShapeDtypeStruct((S, LANES), F32)],
        scratch_shapes=[pltpu.VMEM((tq, dh), F32), pltpu.VMEM((S, dh), F32), pltpu.VMEM((S, dh), F32),
                        pltpu.VMEM((tq, LANES), F32)],
        compiler_params=_params("arbitrary", "arbitrary"),
    )(qkv, qkv, qkv, cneg, do, delta, lse)


def _fox_gate_bwd(dy, o, gate, name):
    S, W = dy.shape
    ts = _tile(S, ROW_TILE)

    def body(dy_ref, o_ref, g_ref, do_ref, dg_ref, delta_ref):
        d, g, o_val = dy_ref[...], g_ref[...], o_ref[...]
        sg = _sigmoid(g)
        d_o = (d * (g * sg)).astype(BF16)
        do_ref[...] = d_o
        dg_ref[...] = (d * o_val * _dsilu(g, sg)).astype(BF16)
        prod = d_o.astype(F32) * o_val
        for h in range(W // LANES):
            cols = slice(h * LANES, (h + 1) * LANES)
            delta_ref[:, cols] = jnp.broadcast_to(jnp.sum(prod[:, cols], axis=1, keepdims=True), (ts, LANES))

    row = pl.BlockSpec((ts, W), lambda i: (i, 0))
    return pl.pallas_call(
        body, name=name, grid=(S // ts,),
        in_specs=[row, row, row], out_specs=[row, row, row],
        out_shape=[jax.ShapeDtypeStruct((S, W), BF16), jax.ShapeDtypeStruct((S, W), BF16),
                   jax.ShapeDtypeStruct((S, W), F32)],
        compiler_params=_params("parallel"),
    )(dy, o, gate)


def _shifted_copies(buf_ref, sh_ref, rows):
    for j in range(1, SUBLANES):
        sh_ref[j, 0:rows, :] = buf_ref[j:j + rows, :]


def _tap(buf_ref, sh_ref, r0, off, cols):
    j, base = off % SUBLANES, off - off % SUBLANES
    if j == 0:
        return buf_ref[pl.ds(r0 + base, SUBLANES), cols]
    return sh_ref[j, pl.ds(r0 + base, SUBLANES), cols]


def _tap_weights(dw_ref, cols):
    return [jnp.broadcast_to(dw_ref[k:k + 1, cols], (SUBLANES, LANES)) for k in range(dw_ref.shape[0])]


def _conv_fwd(proj, dw, dw_b, ln_g, ln_b, name):
    S, C3 = proj.shape
    C = C3 // 3
    K = dw.shape[0]
    assert K - 1 <= CONV_HALO - 2
    ts = _tile(S, CONV_ROWS)
    hb = ts // CONV_HALO
    nrows = ts + CONV_HALO
    lead = CONV_HALO - (K - 1)

    def body(a_ref, b_ref, ah_ref, bh_ref, g_ref, dw_ref, dwb_ref, lg_ref, lb_ref, y_ref, u1_ref, buf, sh):
        first = pl.program_id(0) == 0
        buf[0:CONV_HALO, :] = jnp.where(first, 0.0, ah_ref[...] * _sigmoid(bh_ref[...]))

        def glu(r0):
            rows = pl.ds(r0, CONV_CHUNK)
            buf[pl.ds(r0 + CONV_HALO, CONV_CHUNK), :] = a_ref[rows, :] * _sigmoid(b_ref[rows, :])
        _rows(ts, CONV_CHUNK, glu)
        _shifted_copies(buf, sh, nrows - SUBLANES)

        for s in range(C // LANES):
            cols = slice(s * LANES, (s + 1) * LANES)
            w = _tap_weights(dw_ref, cols)
            bias = jnp.broadcast_to(dwb_ref[:, cols], (SUBLANES, LANES))

            def taps(r0, w=w, bias=bias, cols=cols):
                for u in range(CONV_CHUNK // SUBLANES):
                    r = r0 + u * SUBLANES
                    acc = bias
                    for k in range(K):
                        acc = acc + w[k] * _tap(buf, sh, r, lead + k, cols)
                    u1_ref[pl.ds(r, SUBLANES), cols] = acc
            _rows(ts, CONV_CHUNK, taps)

        def norm(r0):
            rows = pl.ds(r0, CONV_CHUNK)
            u1 = u1_ref[rows, :]
            mu = jnp.mean(u1, axis=-1, keepdims=True)
            xc = u1 - mu
            rstd = lax.rsqrt(jnp.mean(xc * xc, axis=-1, keepdims=True) + LN_EPS)
            z = xc * rstd * lg_ref[...] + lb_ref[...]
            g = g_ref[rows, :]
            y_ref[rows, :] = ((z * _sigmoid(z)) * (g * _sigmoid(g))).astype(BF16)
        _rows(ts, CONV_CHUNK, norm)

    row = lambda col: pl.BlockSpec((ts, C), lambda i: (i, col))
    halo = lambda col: pl.BlockSpec((CONV_HALO, C), lambda i: (jnp.maximum(i * hb - 1, 0), col))
    vec = pl.BlockSpec((1, C), lambda i: (0, 0))
    return pl.pallas_call(
        body, name=name, grid=(S // ts,),
        in_specs=[row(0), row(1), halo(0), halo(1), row(2), pl.BlockSpec((K, C), lambda i: (0, 0)), vec, vec, vec],
        out_specs=[row(0), row(0)],
        out_shape=[jax.ShapeDtypeStruct((S, C), BF16), jax.ShapeDtypeStruct((S, C), F32)],
        scratch_shapes=[pltpu.VMEM((nrows, C), F32), pltpu.VMEM((SUBLANES, nrows, C), F32)],
        compiler_params=_params("parallel"),
    )(proj, proj, proj, proj, proj, dw, dw_b, ln_g, ln_b)


def _conv_bwd_norm(dy, proj, u1, ln_g, ln_b, name):
    S, C = dy.shape
    ts = _tile(S, ROW_TILE)
    groups = CONV_CHUNK // SUBLANES

    def fold(x):
        return jnp.sum(x.reshape(groups, SUBLANES, C), axis=0)

    def body(dy_ref, g_ref, u1_ref, lg_ref, lb_ref, du1_ref, dg_ref, sums_ref):
        @pl.when(pl.program_id(0) == 0)
        def _():
            sums_ref[...] = jnp.zeros_like(sums_ref)

        def chunk(r0):
            rows = pl.ds(r0, CONV_CHUNK)
            d, g, u1 = dy_ref[rows, :], g_ref[rows, :], u1_ref[rows, :]
            mu = jnp.mean(u1, axis=-1, keepdims=True)
            xc = u1 - mu
            rstd = lax.rsqrt(jnp.mean(xc * xc, axis=-1, keepdims=True) + LN_EPS)
            xh = xc * rstd
            z = xh * lg_ref[...] + lb_ref[...]
            sz, sg = _sigmoid(z), _sigmoid(g)
            dgate = d * (z * sz) * _dsilu(g, sg)
            dz = d * (g * sg) * _dsilu(z, sz)
            dxh = dz * lg_ref[...]
            du1 = rstd * (dxh - jnp.mean(dxh, axis=-1, keepdims=True)
                          - xh * jnp.mean(dxh * xh, axis=-1, keepdims=True))
            du1_ref[rows, :] = du1
            dg_ref[rows, :] = dgate.astype(BF16)
            sums_ref[0] += fold(dz * xh)
            sums_ref[1] += fold(dz)
            sums_ref[2] += fold(du1)
            sums_ref[3] += fold(dgate)
        _rows(ts, CONV_CHUNK, chunk)

    row = pl.BlockSpec((ts, C), lambda i: (i, 0))
    vec = pl.BlockSpec((1, C), lambda i: (0, 0))
    return pl.pallas_call(
        body, name=name, grid=(S // ts,),
        in_specs=[row, pl.BlockSpec((ts, C), lambda i: (i, 2)), row, vec, vec],
        out_specs=[row, row, pl.BlockSpec((4, SUBLANES, C), lambda i: (0, 0, 0))],
        out_shape=[jax.ShapeDtypeStruct((S, C), F32), jax.ShapeDtypeStruct((S, C), BF16),
                   jax.ShapeDtypeStruct((4, SUBLANES, C), F32)],
        compiler_params=_params("arbitrary"),
    )(dy, proj, u1, ln_g, ln_b)


def _conv_bwd_taps(du1, proj, dw, name):
    S, C = du1.shape
    K = dw.shape[0]
    ts = _tile(S, CONV_ROWS)
    hb = ts // CONV_HALO
    nblk = S // ts
    last_halo = S // CONV_HALO - 1
    nrows = ts + CONV_HALO
    groups = CONV_CHUNK // SUBLANES

    def body(d_ref, dn_ref, a_ref, b_ref, dw_ref, da_ref, db_ref, ddw_ref, sums_ref, dbuf, dsh):
        i = pl.program_id(0)

        @pl.when(i == 0)
        def _():
            ddw_ref[...] = jnp.zeros_like(ddw_ref)
            sums_ref[...] = jnp.zeros_like(sums_ref)

        dbuf[ts:nrows, :] = jnp.where(i == nblk - 1, 0.0, dn_ref[...])

        def fill(r0):
            rows = pl.ds(r0, CONV_CHUNK)
            dbuf[rows, :] = d_ref[rows, :]
        _rows(ts, CONV_CHUNK, fill)
        _shifted_copies(dbuf, dsh, nrows - SUBLANES)

        zero = jnp.zeros((SUBLANES, LANES), F32)
        fold = lambda x: jnp.sum(x.reshape(groups, SUBLANES, LANES), axis=0)
        for s in range(C // LANES):
            cols = slice(s * LANES, (s + 1) * LANES)
            w = _tap_weights(dw_ref, cols)

            def step(j, carry, w=w, cols=cols):
                r0 = pl.multiple_of(j * CONV_CHUNK, CONV_CHUNK)
                rows = pl.ds(r0, CONV_CHUNK)
                a, sb = a_ref[rows, cols], _sigmoid(b_ref[rows, cols])
                u0 = a * sb
                accs = list(carry[2:])
                parts = []
                for u in range(groups):
                    u0_u = u0[u * SUBLANES:(u + 1) * SUBLANES]
                    acc = zero
                    for k in range(K):
                        x = _tap(dbuf, dsh, r0 + u * SUBLANES, K - 1 - k, cols)
                        acc = acc + w[k] * x
                        accs[k] = accs[k] + u0_u * x
                    parts.append(acc)
                da = jnp.concatenate(parts, axis=0) * sb
                db = da * a * (1.0 - sb)
                da_ref[rows, cols] = da.astype(BF16)
                db_ref[rows, cols] = db.astype(BF16)
                return (carry[0] + fold(da), carry[1] + fold(db), *accs)
            out = lax.fori_loop(0, ts // CONV_CHUNK, step, (zero,) * (K + 2))
            sums_ref[0, :, cols] += out[0]
            sums_ref[1, :, cols] += out[1]
            for k in range(K):
                ddw_ref[k, :, cols] += out[2 + k]

    row = lambda col: pl.BlockSpec((ts, C), lambda i: (i, col))
    nxt = pl.BlockSpec((CONV_HALO, C), lambda i: (jnp.minimum((i + 1) * hb, last_halo), 0))
    return pl.pallas_call(
        body, name=name, grid=(nblk,),
        in_specs=[row(0), nxt, row(0), row(1), pl.BlockSpec((K, C), lambda i: (0, 0))],
        out_specs=[row(0), row(0), pl.BlockSpec((K, SUBLANES, C), lambda i: (0, 0, 0)),
                   pl.BlockSpec((2, SUBLANES, C), lambda i: (0, 0, 0))],
        out_shape=[jax.ShapeDtypeStruct((S, C), BF16), jax.ShapeDtypeStruct((S, C), BF16),
                   jax.ShapeDtypeStruct((K, SUBLANES, C), F32), jax.ShapeDtypeStruct((2, SUBLANES, C), F32)],
        scratch_shapes=[pltpu.VMEM((nrows, C), F32), pltpu.VMEM((SUBLANES, nrows, C), F32)],
        compiler_params=_params("arbitrary"),
    )(du1, du1, proj, proj, dw)


def _adamw(parts, w, m, v, name):
    R, C = w.shape
    tr = _tile(R, 256)
    c1 = 1.0 - ADAM_B1 ** ADAM_STEP
    c2 = 1.0 - ADAM_B2 ** ADAM_STEP

    def body(p_ref, w_ref, m_ref, v_ref, g_ref, d_ref, nm_ref, nv_ref):
        g = p_ref[0].astype(F32)
        for d in range(1, N_DEV):
            g = g + p_ref[d].astype(F32)
        nm = ADAM_B1 * m_ref[...] + (1.0 - ADAM_B1) * g
        nv = ADAM_B2 * v_ref[...] + (1.0 - ADAM_B2) * (g * g)
        g_ref[...] = g
        nm_ref[...] = nm
        nv_ref[...] = nv
        d_ref[...] = -ADAM_LR * ((nm / c1) / (jnp.sqrt(nv / c2) + ADAM_EPS) + ADAM_WD * w_ref[...])

    row = pl.BlockSpec((tr, C), lambda i: (i, 0))
    out = jax.ShapeDtypeStruct((R, C), F32)
    return pl.pallas_call(
        body, name=name, grid=(R // tr,),
        in_specs=[pl.BlockSpec((N_DEV, tr, C), lambda i: (0, i, 0)), row, row, row],
        out_specs=[row, row, row, row], out_shape=[out, out, out, out],
        compiler_params=_params("parallel"),
    )(parts, w, m, v)


def _pad_lanes(a, width=LANES):
    return jnp.pad(a, ((0, 0), (0, width - a.shape[1])))


def _flat_rows(parts, width=LANES):
    flat = jnp.concatenate([p.reshape(-1) for p in parts])
    rows = -(-flat.shape[0] // width)
    rows = -(-rows // SUBLANES) * SUBLANES
    return jnp.pad(flat, (0, rows * width - flat.shape[0])).reshape(rows, width)


def _unflat(rows2d, shapes):
    flat = rows2d.reshape(-1)
    out, pos = [], 0
    for s in shapes:
        n = int(np.prod(s))
        out.append(flat[pos:pos + n].reshape(s))
        pos += n
    return out


def _pack_shards(parts, axes):
    rows = []
    for a, ax in zip(parts, axes):
        sh = a.shape
        a = a.reshape(sh[:ax] + (N_DEV, sh[ax] // N_DEV) + sh[ax + 1:])
        rows.append(jnp.moveaxis(a, ax, 0).reshape(N_DEV, -1))
    return jnp.concatenate(rows, axis=1)


def _unpack_shards(packed, shard_shapes, axes):
    out, pos = [], 0
    for s, ax in zip(shard_shapes, axes):
        n = int(np.prod(s))
        a = jnp.moveaxis(packed[:, pos:pos + n].reshape((N_DEV,) + tuple(s)), 0, ax)
        out.append(a.reshape(tuple(s[:ax]) + (N_DEV * s[ax],) + tuple(s[ax + 1:])))
        pos += n
    return out


def kernel(x, norm_g, fox_w_in, fox_b_f, fox_w_out, conv_w_in, conv_b_in, conv_dw, conv_dw_b, conv_ln_g, conv_ln_b, conv_w_out, final_norm_g, loss_target, m_norm_g, m_fox_w_in, m_fox_b_f, m_fox_w_out, m_conv_w_in, m_conv_b_in, m_conv_dw, m_conv_dw_b, m_conv_ln_g, m_conv_ln_b, m_conv_w_out, m_final_norm_g, v_norm_g, v_fox_w_in, v_fox_b_f, v_fox_w_out, v_conv_w_in, v_conv_b_in, v_conv_dw, v_conv_dw_b, v_conv_ln_g, v_conv_ln_b, v_conv_w_out, v_final_norm_g):
    h0 = x[0]
    target = loss_target[0]
    S, D = h0.shape
    depth = norm_g.shape[0]
    n_fox, _, fin_shard = fox_w_in.shape
    n_conv, _, cin_shard = conv_w_in.shape
    heads = fox_b_f.shape[1]
    W = fox_w_out.shape[1] * N_DEV
    C = conv_w_out.shape[1] * N_DEV
    assert fin_shard * N_DEV == 4 * W + heads and cin_shard * N_DEV == 3 * C and heads <= LANES
    is_fox = lambda i: i % 2 == 0

    shards = {}
    for i in range(depth):
        j = i // 2
        w_in, w_out = (fox_w_in, fox_w_out) if is_fox(i) else (conv_w_in, conv_w_out)
        shards[i] = (w_in[j].astype(BF16), w_out[j].astype(BF16))
    split = (5 * D) // 8

    def carried(*args, carry=None, **kw):
        r = _matmul(*args, carry=carry, **kw)
        return r if carry is not None else (r, None)

    small_shapes = [conv_b_in.shape, conv_dw.shape, conv_dw_b.shape, conv_ln_g.shape, conv_ln_b.shape]
    small = _flat_rows([conv_b_in, conv_dw, conv_dw_b, conv_ln_g, conv_ln_b])
    small_all = _exchange(small, "ag_small", "gather")
    small_axes = [1, 2, 1, 1, 1]
    conv_b_in_f, conv_dw_f, conv_dw_b_f, conv_ln_g_f, conv_ln_b_f = _unpack_shards(
        small_all.reshape(N_DEV, -1), small_shapes, small_axes)

    gathered_in = {0: _exchange(shards[0][0], "ag_w_in0", "gather")}
    gathered_out = {}

    h = h0
    saved = []
    for i in range(depth):
        j = i // 2
        has_next = i + 1 < depth
        gather = lambda x, kind="gather": (x, kind) if has_next else None
        hn, hn_t = _rms_fwd(h, norm_g[i:i + 1], f"rms_fwd{i}")
        w = gathered_in.pop(i)
        if is_fox(i):
            w = jnp.transpose(w, (1, 0, 2)).reshape(D, -1)
            w_qkvg, w_f = w[:, :4 * W], _pad_lanes(w[:, 4 * W:])
            qkv, got_in = carried(hn, w_qkvg, "nn", BF16, f"fox_qkv{i}", n_out=3 * W,
                                  carry=gather(shards[i + 1][0] if has_next else None, "gather_cols"))
            gate, got = carried(hn, w_qkvg, "nn", F32, f"fox_gate{i}", b_col_off=3 * W, n_out=W,
                                carry=(shards[i][1], "gather"))
            w_out = got.reshape(W, D)
            f = _matmul(hn, w_f, "nn", F32, f"fox_f{i}")
            b_f = _pad_lanes(fox_b_f[j:j + 1])
            c = _fgate_fwd(f, b_f, f"fgate_fwd{i}")
            key_bias = lambda t: (-c[:, :heads]).T.reshape(heads, S // t, 1, t)
            y, o, lse = _flash_fwd(qkv, gate, key_bias(_tile(S, FLASH_FWD_TILE)), heads, f"flash_fwd{i}")
            cneg = key_bias(_tile(S, FLASH_BWD_TILE))
            h_next, got_out = carried(y, w_out, "nn", F32, f"out_proj{i}", add=h,
                                      carry=gather(shards[i + 1][1] if has_next else None))
            if has_next:
                gathered_in[i + 1], gathered_out[i + 1] = got_in, got_out
            saved.append(dict(h=h, hn_t=hn_t, qkv=qkv, gate=gate, f=f, b_f=b_f, cneg=cneg, y=y, o=o, lse=lse,
                              w_qkvg=w_qkvg, w_f=w_f, w_out=w_out))
        else:
            w_out = gathered_out.pop(i).reshape(C, D)
            nxt = shards[i + 1][0] if has_next else None
            proj, got_a = carried(hn, w, "nn", F32, f"conv_in{i}", bias=conv_b_in_f[j:j + 1],
                                  carry=gather(nxt[:split] if has_next else None))
            y, u1 = _conv_fwd(proj, conv_dw_f[j], conv_dw_b_f[j:j + 1], conv_ln_g_f[j:j + 1],
                              conv_ln_b_f[j:j + 1], f"conv_fwd{i}")
            h_next, got_b = carried(y, w_out, "nn", F32, f"out_proj{i}", add=h,
                                    carry=gather(nxt[split:] if has_next else None))
            if has_next:
                gathered_in[i + 1] = jnp.concatenate([got_a, got_b], axis=1)
            saved.append(dict(h=h, hn_t=hn_t, proj=proj, y=y, u1=u1, w_in=w, w_out=w_out))
        h = h_next

    dh, dh16, dh16_t, loss_part, dg_final = _loss_head(h, final_norm_g[None, :], target, "loss_head")

    def shard_cols(g, shard):
        return jnp.transpose(g.reshape(g.shape[0], N_DEV, shard), (1, 0, 2))

    d_norm_g = [None] * depth
    d_fox_b_f = [None] * n_fox
    d_conv_small = [None] * n_conv
    summed_in = [None] * depth
    summed_out = [None] * depth
    pend_in = pend_out = None
    early_in = None
    for i in reversed(range(depth)):
        j = i // 2
        sv = saved[i]
        scatter = lambda x, kind="scatter": (x, kind) if x is not None else None
        dy = _matmul(dh16, sv["w_out"], "nt", F32, f"d_out_proj{i}")
        dw_out_t, got = carried(dh16_t, sv["y"], "nn", F32, f"dw_out{i}", carry=scatter(pend_out))
        if pend_out is not None:
            summed_out[i + 1] = got
        own_out = dw_out_t.T.reshape(N_DEV, -1, D).astype(BF16)
        if is_fox(i):
            do, dgate, delta = _fox_gate_bwd(dy, sv["o"], sv["gate"], f"fox_gate_bwd{i}")
            dq, dk, dv, colsum, rowsum = _flash_bwd(sv["qkv"], do, delta, sv["lse"], sv["cneg"], heads,
                                                    f"flash_bwd{i}")
            dc = _pad_lanes(rowsum[:, :heads] - jnp.sum(colsum, axis=2).reshape(heads, S).T)
            df, dbf = _fgate_bwd(dc, sv["f"], sv["b_f"], f"fgate_bwd{i}")
            dproj = jnp.concatenate([dq, dk, dv, dgate], axis=1)
            dhn = _matmul(df, sv["w_f"], "nt", F32, f"d_fox_f{i}")
            dhn, got = carried(dproj, sv["w_qkvg"], "nt", F32, f"d_fox_in{i}", add=dhn,
                               carry=scatter(pend_in, "scatter_cols"))
            if pend_in is not None:
                summed_in[i + 1] = got
            dw_f = _matmul(sv["hn_t"], df, "nn", F32, f"dw_fox_f{i}")[:, :heads]
            parts_of = lambda dw, rows: shard_cols(jnp.concatenate([dw, dw_f[rows]], axis=1), fin_shard).astype(BF16)
            if i > 0:
                dw_qkvg, summed_out[i] = carried(sv["hn_t"], dproj, "nn", F32, f"dw_fox_in{i}",
                                                 carry=(own_out, "scatter"))
                pend_in = parts_of(dw_qkvg, slice(None))
            else:
                top, bottom = slice(0, D // 2), slice(D // 2, D)
                dw_top, summed_out[i] = carried(sv["hn_t"][top], dproj, "nn", F32, f"dw_fox_in{i}",
                                                carry=(own_out, "scatter"))
                dw_bottom, early_in = carried(sv["hn_t"][bottom], dproj, "nn", F32, f"dw_fox_in{i}_rest",
                                              carry=(parts_of(dw_top, top), "scatter"))
                pend_in = parts_of(dw_bottom, bottom)
            pend_out = None
            d_fox_b_f[j] = jnp.sum(dbf, axis=0)[:heads]
        else:
            du1, dgate, nsums = _conv_bwd_norm(dy, sv["proj"], sv["u1"], conv_ln_g_f[j:j + 1],
                                               conv_ln_b_f[j:j + 1], f"conv_bwd_norm{i}")
            da, db, ddw, absums = _conv_bwd_taps(du1, sv["proj"], conv_dw_f[j], f"conv_bwd_taps{i}")
            dproj = jnp.concatenate([da, db, dgate], axis=1)
            half = D // 2
            dhn, got_a = carried(dproj, sv["w_in"], "nt", F32, f"d_conv_in{i}",
                                 carry=scatter(pend_in[:, :half] if pend_in is not None else None))
            dw_in, got_b = carried(sv["hn_t"], dproj, "nn", BF16, f"dw_conv_in{i}",
                                   carry=scatter(pend_in[:, half:] if pend_in is not None else None))
            if pend_in is not None:
                summed_in[i + 1] = jnp.concatenate([got_a, got_b], axis=1)
            pend_in, pend_out = dw_in, own_out
            nsum = jnp.sum(nsums, axis=1)
            absum = jnp.sum(absums, axis=1)
            d_conv_small[j] = dict(b_in=jnp.concatenate([absum[0], absum[1], nsum[3]]),
                                   dw=jnp.sum(ddw, axis=1), dw_b=nsum[2], ln_g=nsum[0], ln_b=nsum[1])
        dh, dh16, dh16_t, dg = _rms_bwd(sv["h"], norm_g[i:i + 1], dhn, dh, f"rms_bwd{i}")
        d_norm_g[i] = jnp.sum(dg, axis=0)
    summed_in[0] = _exchange(pend_in, "rs_w_in0", "scatter")
    if early_in is not None:
        summed_in[0] = jnp.concatenate([early_in, summed_in[0]], axis=1)
    if pend_out is not None:
        summed_out[0] = _exchange(pend_out, "rs_w_out0", "scatter")
    grad_x = dh[None]

    p_small = _pack_shards([jnp.stack([s[name] for s in d_conv_small]) for name in ("b_in", "dw", "dw_b", "ln_g", "ln_b")],
                           small_axes)
    p_small = jnp.pad(p_small, ((0, 0), (0, small.size - p_small.shape[1]))).reshape((N_DEV,) + small.shape)
    r_small = _exchange(p_small, "rs_small", "scatter")

    rep_shapes = [norm_g.shape, fox_b_f.shape, final_norm_g.shape, (1,)]
    rep_part = _flat_rows([jnp.stack(d_norm_g), jnp.stack(d_fox_b_f), jnp.sum(dg_final, axis=0),
                           jnp.sum(loss_part[:, 0])[None]])
    r_rep = _exchange(rep_part, "ag_replicated", "gather")

    def update(parts, w, m, v, name):
        two_d = (-1, w.shape[-1])
        stacked = jnp.stack(parts, axis=1).reshape((N_DEV,) + w.reshape(two_d).shape)
        res = _adamw(stacked, w.reshape(two_d), m.reshape(two_d), v.reshape(two_d), name)
        return [r.reshape(w.shape) for r in res]

    fox_layers = [i for i in range(depth) if is_fox(i)]
    conv_layers = [i for i in range(depth) if not is_fox(i)]
    u_fin = update([summed_in[i] for i in fox_layers], fox_w_in, m_fox_w_in, v_fox_w_in, "adamw_fox_w_in")
    u_fout = update([summed_out[i] for i in fox_layers], fox_w_out, m_fox_w_out, v_fox_w_out, "adamw_fox_w_out")
    u_cin = update([summed_in[i] for i in conv_layers], conv_w_in, m_conv_w_in, v_conv_w_in, "adamw_conv_w_in")
    u_cout = update([summed_out[i] for i in conv_layers], conv_w_out, m_conv_w_out, v_conv_w_out,
                    "adamw_conv_w_out")
    u_small = _adamw(r_small, small,
                     _flat_rows([m_conv_b_in, m_conv_dw, m_conv_dw_b, m_conv_ln_g, m_conv_ln_b]),
                     _flat_rows([v_conv_b_in, v_conv_dw, v_conv_dw_b, v_conv_ln_g, v_conv_ln_b]), "adamw_small")
    zero1 = jnp.zeros((1,), F32)
    u_rep = _adamw(r_rep, _flat_rows([norm_g, fox_b_f, final_norm_g, zero1]),
                   _flat_rows([m_norm_g, m_fox_b_f, m_final_norm_g, zero1]),
                   _flat_rows([v_norm_g, v_fox_b_f, v_final_norm_g, zero1]), "adamw_replicated")

    outs = []
    loss = None
    for kind in range(4):
        b_in_k, dw_k, dwb_k, lng_k, lnb_k = _unflat(u_small[kind], small_shapes)
        ng_k, bf_k, fg_k, loss_k = _unflat(u_rep[kind], rep_shapes)
        if kind == 0:
            loss = loss_k[0]
        outs += [ng_k, u_fin[kind], bf_k, u_fout[kind], u_cin[kind], b_in_k, dw_k, dwb_k, lng_k, lnb_k,
                 u_cout[kind], fg_k]
    return (loss, grad_x, *outs)
```

```python
import functools

import numpy as np
import jax
import jax.numpy as jnp
from jax import lax
from jax.experimental import pallas as pl
from jax.experimental.pallas import tpu as pltpu

F32 = jnp.float32
BF16 = jnp.bfloat16
MESH_ID = pl.DeviceIdType.MESH

N_DEV = 8
RMS_EPS = 1e-6
LN_EPS = 1e-5
ADAM_LR = 0.001
ADAM_B1 = 0.9
ADAM_B2 = 0.999
ADAM_EPS = 1e-08
ADAM_WD = 0.01
ADAM_STEP = 10

LANES = 128
SUBLANES = 8
VMEM_LIMIT = 60 * 1024 * 1024
NEG_BIG = -1e30
CONV_HALO = 32
FLASH_FWD_TILE = 1024
FLASH_BWD_TILE = 1024
CONV_ROWS = 128
CONV_CHUNK = 32
FGATE_ROWS = 256
ROW_TILE = 256


def _params(*sem):
    return pltpu.CompilerParams(dimension_semantics=sem if sem else None, vmem_limit_bytes=VMEM_LIMIT)


def _tile(n, pref):
    if n <= pref:
        return n
    t = pref
    while n % t:
        t //= 2
    return t


def _sigmoid(x):
    return 1.0 / (1.0 + jnp.exp(-x))


def _dsilu(x, s):
    return s * (1.0 + x * (1.0 - s))


def _rows(n, chunk, fn):
    def step(i, carry):
        fn(pl.multiple_of(i * chunk, chunk))
        return carry
    lax.fori_loop(0, n // chunk, step, 0)


def _peer(k):
    x, y, c = lax.axis_index("x"), lax.axis_index("y"), lax.axis_index("c")
    px = 1 - x if (k >> 2) & 1 else x
    py = 1 - y if (k >> 1) & 1 else y
    pc = 1 - c if k & 1 else c
    return (px, py, pc), 4 * px + 2 * py + pc


def _exchange_copies(x_ref, o_ref, send_sems, recv_sems, local_sem, kind):
    _, me = _peer(0)

    def window(ref, idx, n):
        assert n % LANES == 0
        return ref.at[:, pl.ds(pl.multiple_of(idx * n, LANES), n)]

    def src(idx):
        if kind == "scatter":
            return x_ref.at[idx]
        return window(x_ref, idx, o_ref.shape[2]) if kind == "scatter_cols" else x_ref

    def dst(idx):
        return window(o_ref, idx, x_ref.shape[1]) if kind == "gather_cols" else o_ref.at[idx]

    local = pltpu.make_async_copy(src(me), dst(me), local_sem)
    sends, arrivals = [], []
    for k in range(1, N_DEV):
        peer, pidx = _peer(k)
        sems = dict(send_sem=send_sems.at[k - 1], recv_sem=recv_sems.at[k - 1], device_id=peer,
                    device_id_type=MESH_ID)
        sends.append(pltpu.make_async_remote_copy(src_ref=src(pidx), dst_ref=dst(me), **sems))
        arrivals.append(pltpu.make_async_remote_copy(src_ref=src(me), dst_ref=dst(pidx), **sems))
    return local, sends, arrivals


def _exchange_start(*refs, kind):
    local, sends, _ = _exchange_copies(*refs, kind)
    local.start()
    for cp in sends:
        cp.start()


def _exchange_wait(*refs, kind):
    local, sends, arrivals = _exchange_copies(*refs, kind)
    for cp in arrivals:
        cp.wait_recv()
    for cp in sends:
        cp.wait_send()
    local.wait()


EXCHANGE_SCRATCH = [pltpu.SemaphoreType.DMA((N_DEV - 1,)), pltpu.SemaphoreType.DMA((N_DEV - 1,)),
                    pltpu.SemaphoreType.DMA]


def _exchange_shape(x, kind):
    shape = {"gather": lambda: (N_DEV,) + x.shape, "scatter": lambda: x.shape,
             "gather_cols": lambda: (x.shape[0], N_DEV * x.shape[1]),
             "scatter_cols": lambda: (N_DEV, x.shape[0], x.shape[1] // N_DEV)}[kind]()
    return jax.ShapeDtypeStruct(shape, x.dtype)


def _exchange(x, name, kind):
    def body(*refs):
        _exchange_start(*refs, kind=kind)
        _exchange_wait(*refs, kind=kind)

    return pl.pallas_call(
        body, name=name,
        out_shape=_exchange_shape(x, kind),
        in_specs=[pl.BlockSpec(memory_space=pl.ANY)],
        out_specs=pl.BlockSpec(memory_space=pl.ANY),
        scratch_shapes=list(EXCHANGE_SCRATCH),
    )(x)


def _matmul(a, b, mode, out_dtype, name, bias=None, add=None, b_col_off=0, n_out=None, carry=None):
    M, K = a.shape
    N = n_out if n_out is not None else (b.shape[0] if mode == "nt" else b.shape[1])
    tm = _tile(M, 512)
    tn = _tile(N, 1024)
    k_cap = 4096 if a.dtype.itemsize == 2 and b.dtype.itemsize == 2 else 2048
    tk = next(K // d for d in range(1, K + 1) if K % d == 0 and K // d <= k_cap and (K // d) % LANES == 0)
    nm, nn, nk = M // tm, N // tn, K // tk
    assert b_col_off % tn == 0
    joff = b_col_off // tn
    dims = {"nn": (((1,), (0,)), ((), ())), "nt": (((1,), (1,)), ((), ()))}[mode]
    n_in = 2 + (bias is not None) + (add is not None)

    def body(*refs):
        a_ref, b_ref = refs[0], refs[1]
        bias_ref = refs[2] if bias is not None else None
        add_ref = refs[n_in - 1] if add is not None else None
        pos = n_in
        x_ref = o_ref = x_out_ref = None
        if carry is not None:
            x_ref, o_ref, x_out_ref = refs[pos], refs[pos + 1], refs[pos + 2]
            pos += 3
        else:
            o_ref = refs[pos]
            pos += 1
        acc_ref = None
        if nk > 1:
            acc_ref = refs[pos]
            pos += 1
        exchange_refs = (x_ref, x_out_ref) + tuple(refs[pos:])
        i, j, kk = pl.program_id(0), pl.program_id(1), pl.program_id(2)

        if carry is not None:
            @pl.when((i == 0) & (j == 0) & (kk == 0))
            def _():
                _exchange_start(*exchange_refs, kind=carry[1])

        part = lax.dot_general(a_ref[...].astype(BF16), b_ref[...].astype(BF16), dims,
                               preferred_element_type=F32)

        def finish(r):
            if bias_ref is not None:
                r = r + bias_ref[...]
            if add_ref is not None:
                r = r + add_ref[...]
            o_ref[...] = r.astype(o_ref.dtype)

        if nk == 1:
            finish(part)
        else:
            @pl.when(kk == 0)
            def _():
                acc_ref[...] = part

            @pl.when(kk > 0)
            def _():
                acc_ref[...] += part

            @pl.when(kk == nk - 1)
            def _():
                finish(acc_ref[...])

        if carry is not None:
            @pl.when((i == nm - 1) & (j == nn - 1) & (kk == nk - 1))
            def _():
                _exchange_wait(*exchange_refs, kind=carry[1])

    a_spec = pl.BlockSpec((tm, tk), lambda i, j, k: (i, k))
    if mode == "nt":
        b_spec = pl.BlockSpec((tn, tk), lambda i, j, k: (j, k))
    else:
        b_spec = pl.BlockSpec((tk, tn), lambda i, j, k: (k, j + joff))
    in_specs = [a_spec, b_spec]
    args = [a, b]
    if bias is not None:
        in_specs.append(pl.BlockSpec((1, tn), lambda i, j, k: (0, j)))
        args.append(bias)
    if add is not None:
        in_specs.append(pl.BlockSpec((tm, tn), lambda i, j, k: (i, j)))
        args.append(add)
    out_specs = [pl.BlockSpec((tm, tn), lambda i, j, k: (i, j))]
    out_shape = [jax.ShapeDtypeStruct((M, N), out_dtype)]
    scratch = [pltpu.VMEM((tm, tn), F32)] if nk > 1 else []
    if carry is not None:
        in_specs.append(pl.BlockSpec(memory_space=pl.ANY))
        args.append(carry[0])
        out_specs.append(pl.BlockSpec(memory_space=pl.ANY))
        out_shape.append(_exchange_shape(*carry))
        scratch += EXCHANGE_SCRATCH
    res = pl.pallas_call(
        body, name=name,
        grid=(nm, nn, nk),
        in_specs=in_specs, out_specs=out_specs, out_shape=out_shape, scratch_shapes=scratch,
        compiler_params=_params(*(("arbitrary",) * 3 if carry is not None else ("parallel", "parallel", "arbitrary"))),
    )(*args)
    return res if carry is not None else res[0]


def _rms_fwd(h, g, name):
    S, D = h.shape
    ts = _tile(S, 512)

    def body(h_ref, g_ref, o_ref, ot_ref):
        x = h_ref[...]
        r = lax.rsqrt(jnp.mean(x * x, axis=-1, keepdims=True) + RMS_EPS)
        y = x * r * g_ref[...]
        o_ref[...] = y.astype(BF16)
        ot_ref[...] = y.T.astype(BF16)

    return pl.pallas_call(
        body, name=name, grid=(S // ts,),
        in_specs=[pl.BlockSpec((ts, D), lambda i: (i, 0)), pl.BlockSpec((1, D), lambda i: (0, 0))],
        out_specs=[pl.BlockSpec((ts, D), lambda i: (i, 0)), pl.BlockSpec((D, ts), lambda i: (0, i))],
        out_shape=[jax.ShapeDtypeStruct((S, D), BF16), jax.ShapeDtypeStruct((D, S), BF16)],
        compiler_params=_params("parallel"),
    )(h, g)


def _rms_bwd_block(x, g, dy):
    r = lax.rsqrt(jnp.mean(x * x, axis=-1, keepdims=True) + RMS_EPS)
    xr = x * r
    t = dy * g
    dx = r * (t - xr * jnp.mean(t * xr, axis=-1, keepdims=True))
    return dx, dy * xr


def _rms_bwd(h, g, dhn, dh, name):
    S, D = h.shape
    ts = _tile(S, ROW_TILE)

    def body(h_ref, g_ref, dhn_ref, dh_ref, o_ref, o16_ref, o16t_ref, dg_ref):
        dx, dgt = _rms_bwd_block(h_ref[...], g_ref[...], dhn_ref[...])
        out = dh_ref[...] + dx
        o_ref[...] = out
        o16_ref[...] = out.astype(BF16)
        o16t_ref[...] = out.T.astype(BF16)
        part = jnp.sum(dgt.reshape(ts // SUBLANES, SUBLANES, D), axis=0)

        @pl.when(pl.program_id(0) == 0)
        def _():
            dg_ref[...] = part

        @pl.when(pl.program_id(0) > 0)
        def _():
            dg_ref[...] += part

    row = pl.BlockSpec((ts, D), lambda i: (i, 0))
    return pl.pallas_call(
        body, name=name, grid=(S // ts,),
        in_specs=[row, pl.BlockSpec((1, D), lambda i: (0, 0)), row, row],
        out_specs=[row, row, pl.BlockSpec((D, ts), lambda i: (0, i)),
                   pl.BlockSpec((SUBLANES, D), lambda i: (0, 0))],
        out_shape=[jax.ShapeDtypeStruct((S, D), F32), jax.ShapeDtypeStruct((S, D), BF16),
                   jax.ShapeDtypeStruct((D, S), BF16), jax.ShapeDtypeStruct((SUBLANES, D), F32)],
        compiler_params=_params("arbitrary"),
    )(h, g, dhn, dh)


def _loss_head(h, g, target, name):
    S, D = h.shape
    ts = _tile(S, ROW_TILE)

    def body(h_ref, g_ref, t_ref, o_ref, o16_ref, o16t_ref, loss_ref, dg_ref):
        x = h_ref[...]
        gg = g_ref[...]
        r = lax.rsqrt(jnp.mean(x * x, axis=-1, keepdims=True) + RMS_EPS)
        err = x * r * gg - t_ref[...]
        row_loss = 0.5 * jnp.mean(err * err, axis=-1, keepdims=True)
        dx, dgt = _rms_bwd_block(x, gg, err * (1.0 / D))
        o_ref[...] = dx
        o16_ref[...] = dx.astype(BF16)
        o16t_ref[...] = dx.T.astype(BF16)
        part = jnp.sum(dgt.reshape(ts // SUBLANES, SUBLANES, D), axis=0)
        lpart = jnp.sum(jnp.broadcast_to(row_loss, (ts, LANES)).reshape(ts // SUBLANES, SUBLANES, LANES), axis=0)

        @pl.when(pl.program_id(0) == 0)
        def _():
            dg_ref[...] = part
            loss_ref[...] = lpart

        @pl.when(pl.program_id(0) > 0)
        def _():
            dg_ref[...] += part
            loss_ref[...] += lpart

    row = pl.BlockSpec((ts, D), lambda i: (i, 0))
    return pl.pallas_call(
        body, name=name, grid=(S // ts,),
        in_specs=[row, pl.BlockSpec((1, D), lambda i: (0, 0)), row],
        out_specs=[row, row, pl.BlockSpec((D, ts), lambda i: (0, i)),
                   pl.BlockSpec((SUBLANES, LANES), lambda i: (0, 0)), pl.BlockSpec((SUBLANES, D), lambda i: (0, 0))],
        out_shape=[jax.ShapeDtypeStruct((S, D), F32), jax.ShapeDtypeStruct((S, D), BF16),
                   jax.ShapeDtypeStruct((D, S), BF16), jax.ShapeDtypeStruct((SUBLANES, LANES), F32),
                   jax.ShapeDtypeStruct((SUBLANES, D), F32)],
        compiler_params=_params("arbitrary"),
    )(h, g, target)


def _split3(x):
    hi = x.astype(BF16)
    r1 = x - hi.astype(F32)
    mid = r1.astype(BF16)
    lo = (r1 - mid.astype(F32)).astype(BF16)
    return hi, mid, lo


def _tri_sum(tri, x):
    hi, mid, lo = _split3(x)
    dot = functools.partial(jnp.dot, preferred_element_type=F32)
    return dot(tri, hi) + dot(tri, mid) + dot(tri, lo)


def _fgate_fwd(f, b_f, name):
    S = f.shape[0]
    tb = _tile(S, FGATE_ROWS)

    def body(f_ref, b_ref, c_ref, carry_ref):
        @pl.when(pl.program_id(0) == 0)
        def _():
            carry_ref[...] = jnp.zeros_like(carry_ref)

        x = f_ref[...] + b_ref[...]
        lf = jnp.minimum(x, 0.0) - jnp.log1p(jnp.exp(-jnp.abs(x)))
        r = lax.broadcasted_iota(jnp.int32, (tb, tb), 0)
        c = lax.broadcasted_iota(jnp.int32, (tb, tb), 1)
        tri = (c <= r).astype(BF16)
        c_ref[...] = _tri_sum(tri, lf) + carry_ref[0:1, :]
        carry_ref[...] += _tri_sum(jnp.ones((SUBLANES, tb), BF16), lf)

    return pl.pallas_call(
        body, name=name, grid=(S // tb,),
        in_specs=[pl.BlockSpec((tb, LANES), lambda i: (i, 0)), pl.BlockSpec((1, LANES), lambda i: (0, 0))],
        out_specs=pl.BlockSpec((tb, LANES), lambda i: (i, 0)),
        out_shape=jax.ShapeDtypeStruct((S, LANES), F32),
        scratch_shapes=[pltpu.VMEM((SUBLANES, LANES), F32)],
        compiler_params=_params("arbitrary"),
    )(f, b_f)


def _fgate_bwd(dc, f, b_f, name):
    S = f.shape[0]
    tb = _tile(S, FGATE_ROWS)
    nb = S // tb

    def body(dc_ref, f_ref, b_ref, df_ref, db_ref, carry_ref):
        @pl.when(pl.program_id(0) == 0)
        def _():
            carry_ref[...] = jnp.zeros_like(carry_ref)
            db_ref[...] = jnp.zeros_like(db_ref)

        d = dc_ref[...]
        r = lax.broadcasted_iota(jnp.int32, (tb, tb), 0)
        c = lax.broadcasted_iota(jnp.int32, (tb, tb), 1)
        tri = (c >= r).astype(BF16)
        dlf = _tri_sum(tri, d) + carry_ref[0:1, :]
        carry_ref[...] += _tri_sum(jnp.ones((SUBLANES, tb), BF16), d)
        df = dlf * _sigmoid(-(f_ref[...] + b_ref[...]))
        df_ref[...] = df
        db_ref[...] += jnp.sum(df.reshape(tb // SUBLANES, SUBLANES, LANES), axis=0)

    rev = pl.BlockSpec((tb, LANES), lambda i: (nb - 1 - i, 0))
    return pl.pallas_call(
        body, name=name, grid=(nb,),
        in_specs=[rev, rev, pl.BlockSpec((1, LANES), lambda i: (0, 0))],
        out_specs=[rev, pl.BlockSpec((SUBLANES, LANES), lambda i: (0, 0))],
        out_shape=[jax.ShapeDtypeStruct((S, LANES), F32), jax.ShapeDtypeStruct((SUBLANES, LANES), F32)],
        scratch_shapes=[pltpu.VMEM((SUBLANES, LANES), F32)],
        compiler_params=_params("arbitrary"),
    )(dc, f, b_f)


LOG2E = 1.4426950408889634
NT_DIMS = (((1,), (1,)), ((), ()))
TN_DIMS = (((0,), (0,)), ((), ()))


def _causal_mask(t):
    r = lax.broadcasted_iota(jnp.int32, (t, t), 0)
    c = lax.broadcasted_iota(jnp.int32, (t, t), 1)
    return r >= c


def _key_tiles(qi, tile):
    group = 4

    def several(i, carry):
        for u in range(group):
            tile(group * i + u, False)
        return carry
    lax.fori_loop(0, qi // group, several, 0)

    def single(j, carry):
        tile(j, False)
        return carry
    lax.fori_loop((qi // group) * group, qi, single, 0)
    tile(qi, True)


def _flash_fwd(qkv, gate, cneg, heads, name):
    S, W3 = qkv.shape
    W = W3 // 3
    dh = W // heads
    assert dh == LANES
    tq = _tile(S, FLASH_FWD_TILE)
    nq = S // tq
    c1 = dh ** -0.5 * LOG2E

    def body(q_ref, k_ref, v_ref, b_ref, g_ref, y_ref, o_ref, lse_ref, m_scr, l_scr, acc_scr):
        qi = pl.program_id(1)
        m_scr[...] = jnp.full_like(m_scr, NEG_BIG)
        l_scr[...] = jnp.zeros_like(l_scr)
        acc_scr[...] = jnp.zeros_like(acc_scr)
        q = q_ref[...]

        def tile(j, diagonal):
            rows = pl.ds(pl.multiple_of(j * tq, tq), tq)
            t = lax.dot_general(q, k_ref[rows, :], NT_DIMS, preferred_element_type=F32) * c1 + b_ref[0, j] * LOG2E
            if diagonal:
                t = jnp.where(_causal_mask(tq), t, NEG_BIG)
            m_prev = m_scr[...]
            m_next = jnp.maximum(m_prev, jnp.max(t, axis=1, keepdims=True))
            alpha = jnp.exp2(m_prev - m_next)
            p = jnp.exp2(t - m_next[:, :1])
            l_scr[...] = alpha * l_scr[...] + jnp.sum(p, axis=1, keepdims=True)
            acc_scr[...] = alpha * acc_scr[...] + jnp.dot(p.astype(BF16), v_ref[rows, :],
                                                          preferred_element_type=F32)
            m_scr[...] = m_next

        _key_tiles(qi, tile)
        l = l_scr[...]
        o = acc_scr[...] / l
        g = g_ref[...]
        o_ref[...] = o
        lse_ref[...] = m_scr[...] + jnp.log2(l)
        y_ref[...] = (o * (g * _sigmoid(g))).astype(BF16)

    qblk = pl.BlockSpec((tq, dh), lambda h, i: (i, h))
    return pl.pallas_call(
        body, name=name, grid=(heads, nq),
        in_specs=[qblk,
                  pl.BlockSpec((S, dh), lambda h, i: (0, heads + h)),
                  pl.BlockSpec((S, dh), lambda h, i: (0, 2 * heads + h)),
                  pl.BlockSpec((1, nq, 1, tq), lambda h, i: (h, 0, 0, 0)),
                  qblk],
        out_specs=[qblk, qblk, qblk],
        out_shape=[jax.ShapeDtypeStruct((S, W), BF16), jax.ShapeDtypeStruct((S, W), F32),
                   jax.ShapeDtypeStruct((S, W), F32)],
        scratch_shapes=[pltpu.VMEM((tq, LANES), F32), pltpu.VMEM((tq, LANES), F32), pltpu.VMEM((tq, dh), F32)],
        compiler_params=_params("parallel", "arbitrary"),
    )(qkv, qkv, qkv, cneg, gate)


def _flash_bwd(qkv, do, delta, lse, cneg, heads, name):
    S, W3 = qkv.shape
    W = W3 // 3
    dh = W // heads
    tq = _tile(S, FLASH_BWD_TILE)
    nq = S // tq
    scale = dh ** -0.5
    c1 = scale * LOG2E

    def body(q_ref, k_ref, v_ref, b_ref, do_ref, delta_ref, lse_ref,
             dq_ref, dk_ref, dv_ref, db_ref, rs_ref, dq_scr, dk_scr, dv_scr, rs_scr):
        h, qi = pl.program_id(0), pl.program_id(1)

        @pl.when((h == 0) & (qi == 0))
        def _():
            rs_ref[...] = jnp.zeros_like(rs_ref)

        @pl.when(qi == 0)
        def _():
            dk_scr[...] = jnp.zeros_like(dk_scr)
            dv_scr[...] = jnp.zeros_like(dv_scr)
            db_ref[...] = jnp.zeros_like(db_ref)

        dq_scr[...] = jnp.zeros_like(dq_scr)
        rs_scr[...] = jnp.zeros_like(rs_scr)
        q, d_o = q_ref[...], do_ref[...]
        lse = lse_ref[...][:, :1]
        delta = delta_ref[...][:, :1]

        def tile(j, diagonal):
            rows = pl.ds(pl.multiple_of(j * tq, tq), tq)
            k, v = k_ref[rows, :], v_ref[rows, :]
            t = lax.dot_general(q, k, NT_DIMS, preferred_element_type=F32) * c1 + b_ref[0, j] * LOG2E
            p = jnp.exp2(t - lse)
            if diagonal:
                p = jnp.where(_causal_mask(tq), p, 0.0)
            dp = lax.dot_general(d_o, v, NT_DIMS, preferred_element_type=F32)
            ds = p * (dp - delta)
            dv_scr[rows, :] += lax.dot_general(p.astype(BF16), d_o, TN_DIMS, preferred_element_type=F32)
            db_ref[0, j] += jnp.sum(ds.reshape(tq // SUBLANES, SUBLANES, tq), axis=0)
            dsb = (ds * scale).astype(BF16)
            dk_scr[rows, :] += lax.dot_general(dsb, q, TN_DIMS, preferred_element_type=F32)
            dq_scr[...] += jnp.dot(dsb, k, preferred_element_type=F32)
            rs_scr[...] += jnp.sum(ds, axis=1, keepdims=True)

        _key_tiles(qi, tile)
        dq_ref[...] = dq_scr[...].astype(BF16)
        lane = lax.broadcasted_iota(jnp.int32, (tq, LANES), 1)
        rs_ref[pl.ds(pl.multiple_of(qi * tq, tq), tq), :] += jnp.where(lane == h, rs_scr[...], 0.0)

        @pl.when(qi == nq - 1)
        def _():
            dk_ref[...] = dk_scr[...].astype(BF16)
            dv_ref[...] = dv_scr[...].astype(BF16)

    qblk = pl.BlockSpec((tq, dh), lambda h, i: (i, h))
    once = dict(pipeline_mode=pl.Buffered(1))
    head = pl.BlockSpec((S, dh), lambda h, i: (0, h), **once)
    return pl.pallas_call(
        body, name=name, grid=(heads, nq),
        in_specs=[qblk,
                  pl.BlockSpec((S, dh), lambda h, i: (0, heads + h), **once),
                  pl.BlockSpec((S, dh), lambda h, i: (0, 2 * heads + h), **once),
                  pl.BlockSpec((1, nq, 1, tq), lambda h, i: (h, 0, 0, 0)),
                  qblk, qblk, qblk],
        out_specs=[qblk, head, head,
                   pl.BlockSpec((1, nq, SUBLANES, tq), lambda h, i: (h, 0, 0, 0)),
                   pl.BlockSpec((S, LANES), lambda h, i: (0, 0), **once)],
        out_shape=[jax.ShapeDtypeStruct((S, W), BF16), jax.ShapeDtypeStruct((S, W), BF16),
                   jax.ShapeDtypeStruct((S, W), BF16), jax.ShapeDtypeStruct((heads, nq, SUBLANES, tq), F32),
                   jax.ShapeDtypeStruct((S, LANES), F32)],
        scratch_shapes=[pltpu.VMEM((tq, dh), F32), pltpu.VMEM((S, dh), F32), pltpu.VMEM((S, dh), F32),
                        pltpu.VMEM((tq, LANES), F32)],
        compiler_params=_params("arbitrary", "arbitrary"),
    )(qkv, qkv, qkv, cneg, do, delta, lse)


def _fox_gate_bwd(dy, o, gate, name):
    S, W = dy.shape
    ts = _tile(S, ROW_TILE)

    def body(dy_ref, o_ref, g_ref, do_ref, dg_ref, delta_ref):
        d, g, o_val = dy_ref[...], g_ref[...], o_ref[...]
        sg = _sigmoid(g)
        d_o = (d * (g * sg)).astype(BF16)
        do_ref[...] = d_o
        dg_ref[...] = (d * o_val * _dsilu(g, sg)).astype(BF16)
        prod = d_o.astype(F32) * o_val
        for h in range(W // LANES):
            cols = slice(h * LANES, (h + 1) * LANES)
            delta_ref[:, cols] = jnp.broadcast_to(jnp.sum(prod[:, cols], axis=1, keepdims=True), (ts, LANES))

    row = pl.BlockSpec((ts, W), lambda i: (i, 0))
    return pl.pallas_call(
        body, name=name, grid=(S // ts,),
        in_specs=[row, row, row], out_specs=[row, row, row],
        out_shape=[jax.ShapeDtypeStruct((S, W), BF16), jax.ShapeDtypeStruct((S, W), BF16),
                   jax.ShapeDtypeStruct((S, W), F32)],
        compiler_params=_params("parallel"),
    )(dy, o, gate)


def _shifted_copies(buf_ref, sh_ref, rows):
    for j in range(1, SUBLANES):
        sh_ref[j, 0:rows, :] = buf_ref[j:j + rows, :]


def _tap(buf_ref, sh_ref, r0, off, cols):
    j, base = off % SUBLANES, off - off % SUBLANES
    if j == 0:
        return buf_ref[pl.ds(r0 + base, SUBLANES), cols]
    return sh_ref[j, pl.ds(r0 + base, SUBLANES), cols]


def _tap_weights(dw_ref, cols):
    return [jnp.broadcast_to(dw_ref[k:k + 1, cols], (SUBLANES, LANES)) for k in range(dw_ref.shape[0])]


def _conv_fwd(proj, dw, dw_b, ln_g, ln_b, name):
    S, C3 = proj.shape
    C = C3 // 3
    K = dw.shape[0]
    assert K - 1 <= CONV_HALO - 2
    ts = _tile(S, CONV_ROWS)
    hb = ts // CONV_HALO
    nrows = ts + CONV_HALO
    lead = CONV_HALO - (K - 1)

    def body(a_ref, b_ref, ah_ref, bh_ref, g_ref, dw_ref, dwb_ref, lg_ref, lb_ref, y_ref, u1_ref, buf, sh):
        first = pl.program_id(0) == 0
        buf[0:CONV_HALO, :] = jnp.where(first, 0.0, ah_ref[...] * _sigmoid(bh_ref[...]))

        def glu(r0):
            rows = pl.ds(r0, CONV_CHUNK)
            buf[pl.ds(r0 + CONV_HALO, CONV_CHUNK), :] = a_ref[rows, :] * _sigmoid(b_ref[rows, :])
        _rows(ts, CONV_CHUNK, glu)
        _shifted_copies(buf, sh, nrows - SUBLANES)

        for s in range(C // LANES):
            cols = slice(s * LANES, (s + 1) * LANES)
            w = _tap_weights(dw_ref, cols)
            bias = jnp.broadcast_to(dwb_ref[:, cols], (SUBLANES, LANES))

            def taps(r0, w=w, bias=bias, cols=cols):
                for u in range(CONV_CHUNK // SUBLANES):
                    r = r0 + u * SUBLANES
                    acc = bias
                    for k in range(K):
                        acc = acc + w[k] * _tap(buf, sh, r, lead + k, cols)
                    u1_ref[pl.ds(r, SUBLANES), cols] = acc
            _rows(ts, CONV_CHUNK, taps)

        def norm(r0):
            rows = pl.ds(r0, CONV_CHUNK)
            u1 = u1_ref[rows, :]
            mu = jnp.mean(u1, axis=-1, keepdims=True)
            xc = u1 - mu
            rstd = lax.rsqrt(jnp.mean(xc * xc, axis=-1, keepdims=True) + LN_EPS)
            z = xc * rstd * lg_ref[...] + lb_ref[...]
            g = g_ref[rows, :]
            y_ref[rows, :] = ((z * _sigmoid(z)) * (g * _sigmoid(g))).astype(BF16)
        _rows(ts, CONV_CHUNK, norm)

    row = lambda col: pl.BlockSpec((ts, C), lambda i: (i, col))
    halo = lambda col: pl.BlockSpec((CONV_HALO, C), lambda i: (jnp.maximum(i * hb - 1, 0), col))
    vec = pl.BlockSpec((1, C), lambda i: (0, 0))
    return pl.pallas_call(
        body, name=name, grid=(S // ts,),
        in_specs=[row(0), row(1), halo(0), halo(1), row(2), pl.BlockSpec((K, C), lambda i: (0, 0)), vec, vec, vec],
        out_specs=[row(0), row(0)],
        out_shape=[jax.ShapeDtypeStruct((S, C), BF16), jax.ShapeDtypeStruct((S, C), F32)],
        scratch_shapes=[pltpu.VMEM((nrows, C), F32), pltpu.VMEM((SUBLANES, nrows, C), F32)],
        compiler_params=_params("parallel"),
    )(proj, proj, proj, proj, proj, dw, dw_b, ln_g, ln_b)


def _conv_bwd_norm(dy, proj, u1, ln_g, ln_b, name):
    S, C = dy.shape
    ts = _tile(S, ROW_TILE)
    groups = CONV_CHUNK // SUBLANES

    def fold(x):
        return jnp.sum(x.reshape(groups, SUBLANES, C), axis=0)

    def body(dy_ref, g_ref, u1_ref, lg_ref, lb_ref, du1_ref, dg_ref, sums_ref):
        @pl.when(pl.program_id(0) == 0)
        def _():
            sums_ref[...] = jnp.zeros_like(sums_ref)

        def chunk(r0):
            rows = pl.ds(r0, CONV_CHUNK)
            d, g, u1 = dy_ref[rows, :], g_ref[rows, :], u1_ref[rows, :]
            mu = jnp.mean(u1, axis=-1, keepdims=True)
            xc = u1 - mu
            rstd = lax.rsqrt(jnp.mean(xc * xc, axis=-1, keepdims=True) + LN_EPS)
            xh = xc * rstd
            z = xh * lg_ref[...] + lb_ref[...]
            sz, sg = _sigmoid(z), _sigmoid(g)
            dgate = d * (z * sz) * _dsilu(g, sg)
            dz = d * (g * sg) * _dsilu(z, sz)
            dxh = dz * lg_ref[...]
            du1 = rstd * (dxh - jnp.mean(dxh, axis=-1, keepdims=True)
                          - xh * jnp.mean(dxh * xh, axis=-1, keepdims=True))
            du1_ref[rows, :] = du1
            dg_ref[rows, :] = dgate.astype(BF16)
            sums_ref[0] += fold(dz * xh)
            sums_ref[1] += fold(dz)
            sums_ref[2] += fold(du1)
            sums_ref[3] += fold(dgate)
        _rows(ts, CONV_CHUNK, chunk)

    row = pl.BlockSpec((ts, C), lambda i: (i, 0))
    vec = pl.BlockSpec((1, C), lambda i: (0, 0))
    return pl.pallas_call(
        body, name=name, grid=(S // ts,),
        in_specs=[row, pl.BlockSpec((ts, C), lambda i: (i, 2)), row, vec, vec],
        out_specs=[row, row, pl.BlockSpec((4, SUBLANES, C), lambda i: (0, 0, 0))],
        out_shape=[jax.ShapeDtypeStruct((S, C), F32), jax.ShapeDtypeStruct((S, C), BF16),
                   jax.ShapeDtypeStruct((4, SUBLANES, C), F32)],
        compiler_params=_params("arbitrary"),
    )(dy, proj, u1, ln_g, ln_b)


def _conv_bwd_taps(du1, proj, dw, name):
    S, C = du1.shape
    K = dw.shape[0]
    ts = _tile(S, CONV_ROWS)
    hb = ts // CONV_HALO
    nblk = S // ts
    last_halo = S // CONV_HALO - 1
    nrows = ts + CONV_HALO
    groups = CONV_CHUNK // SUBLANES

    def body(d_ref, dn_ref, a_ref, b_ref, dw_ref, da_ref, db_ref, ddw_ref, sums_ref, dbuf, dsh):
        i = pl.program_id(0)

        @pl.when(i == 0)
        def _():
            ddw_ref[...] = jnp.zeros_like(ddw_ref)
            sums_ref[...] = jnp.zeros_like(sums_ref)

        dbuf[ts:nrows, :] = jnp.where(i == nblk - 1, 0.0, dn_ref[...])

        def fill(r0):
            rows = pl.ds(r0, CONV_CHUNK)
            dbuf[rows, :] = d_ref[rows, :]
        _rows(ts, CONV_CHUNK, fill)
        _shifted_copies(dbuf, dsh, nrows - SUBLANES)

        zero = jnp.zeros((SUBLANES, LANES), F32)
        fold = lambda x: jnp.sum(x.reshape(groups, SUBLANES, LANES), axis=0)
        for s in range(C // LANES):
            cols = slice(s * LANES, (s + 1) * LANES)
            w = _tap_weights(dw_ref, cols)

            def step(j, carry, w=w, cols=cols):
                r0 = pl.multiple_of(j * CONV_CHUNK, CONV_CHUNK)
                rows = pl.ds(r0, CONV_CHUNK)
                a, sb = a_ref[rows, cols], _sigmoid(b_ref[rows, cols])
                u0 = a * sb
                accs = list(carry[2:])
                parts = []
                for u in range(groups):
                    u0_u = u0[u * SUBLANES:(u + 1) * SUBLANES]
                    acc = zero
                    for k in range(K):
                        x = _tap(dbuf, dsh, r0 + u * SUBLANES, K - 1 - k, cols)
                        acc = acc + w[k] * x
                        accs[k] = accs[k] + u0_u * x
                    parts.append(acc)
                da = jnp.concatenate(parts, axis=0) * sb
                db = da * a * (1.0 - sb)
                da_ref[rows, cols] = da.astype(BF16)
                db_ref[rows, cols] = db.astype(BF16)
                return (carry[0] + fold(da), carry[1] + fold(db), *accs)
            out = lax.fori_loop(0, ts // CONV_CHUNK, step, (zero,) * (K + 2))
            sums_ref[0, :, cols] += out[0]
            sums_ref[1, :, cols] += out[1]
            for k in range(K):
                ddw_ref[k, :, cols] += out[2 + k]

    row = lambda col: pl.BlockSpec((ts, C), lambda i: (i, col))
    nxt = pl.BlockSpec((CONV_HALO, C), lambda i: (jnp.minimum((i + 1) * hb, last_halo), 0))
    return pl.pallas_call(
        body, name=name, grid=(nblk,),
        in_specs=[row(0), nxt, row(0), row(1), pl.BlockSpec((K, C), lambda i: (0, 0))],
        out_specs=[row(0), row(0), pl.BlockSpec((K, SUBLANES, C), lambda i: (0, 0, 0)),
                   pl.BlockSpec((2, SUBLANES, C), lambda i: (0, 0, 0))],
        out_shape=[jax.ShapeDtypeStruct((S, C), BF16), jax.ShapeDtypeStruct((S, C), BF16),
                   jax.ShapeDtypeStruct((K, SUBLANES, C), F32), jax.ShapeDtypeStruct((2, SUBLANES, C), F32)],
        scratch_shapes=[pltpu.VMEM((nrows, C), F32), pltpu.VMEM((SUBLANES, nrows, C), F32)],
        compiler_params=_params("arbitrary"),
    )(du1, du1, proj, proj, dw)


def _adamw(parts, w, m, v, name):
    R, C = w.shape
    tr = _tile(R, 256)
    c1 = 1.0 - ADAM_B1 ** ADAM_STEP
    c2 = 1.0 - ADAM_B2 ** ADAM_STEP

    def body(p_ref, w_ref, m_ref, v_ref, g_ref, d_ref, nm_ref, nv_ref):
        g = p_ref[0].astype(F32)
        for d in range(1, N_DEV):
            g = g + p_ref[d].astype(F32)
        nm = ADAM_B1 * m_ref[...] + (1.0 - ADAM_B1) * g
        nv = ADAM_B2 * v_ref[...] + (1.0 - ADAM_B2) * (g * g)
        g_ref[...] = g
        nm_ref[...] = nm
        nv_ref[...] = nv
        d_ref[...] = -ADAM_LR * ((nm / c1) / (jnp.sqrt(nv / c2) + ADAM_EPS) + ADAM_WD * w_ref[...])

    row = pl.BlockSpec((tr, C), lambda i: (i, 0))
    out = jax.ShapeDtypeStruct((R, C), F32)
    return pl.pallas_call(
        body, name=name, grid=(R // tr,),
        in_specs=[pl.BlockSpec((N_DEV, tr, C), lambda i: (0, i, 0)), row, row, row],
        out_specs=[row, row, row, row], out_shape=[out, out, out, out],
        compiler_params=_params("parallel"),
    )(parts, w, m, v)


def _pad_lanes(a, width=LANES):
    return jnp.pad(a, ((0, 0), (0, width - a.shape[1])))


def _flat_rows(parts, width=LANES):
    flat = jnp.concatenate([p.reshape(-1) for p in parts])
    rows = -(-flat.shape[0] // width)
    rows = -(-rows // SUBLANES) * SUBLANES
    return jnp.pad(flat, (0, rows * width - flat.shape[0])).reshape(rows, width)


def _unflat(rows2d, shapes):
    flat = rows2d.reshape(-1)
    out, pos = [], 0
    for s in shapes:
        n = int(np.prod(s))
        out.append(flat[pos:pos + n].reshape(s))
        pos += n
    return out


def _pack_shards(parts, axes):
    rows = []
    for a, ax in zip(parts, axes):
        sh = a.shape
        a = a.reshape(sh[:ax] + (N_DEV, sh[ax] // N_DEV) + sh[ax + 1:])
        rows.append(jnp.moveaxis(a, ax, 0).reshape(N_DEV, -1))
    return jnp.concatenate(rows, axis=1)


def _unpack_shards(packed, shard_shapes, axes):
    out, pos = [], 0
    for s, ax in zip(shard_shapes, axes):
        n = int(np.prod(s))
        a = jnp.moveaxis(packed[:, pos:pos + n].reshape((N_DEV,) + tuple(s)), 0, ax)
        out.append(a.reshape(tuple(s[:ax]) + (N_DEV * s[ax],) + tuple(s[ax + 1:])))
        pos += n
    return out


def kernel(x, norm_g, fox_w_in, fox_b_f, fox_w_out, conv_w_in, conv_b_in, conv_dw, conv_dw_b, conv_ln_g, conv_ln_b, conv_w_out, final_norm_g, loss_target, m_norm_g, m_fox_w_in, m_fox_b_f, m_fox_w_out, m_conv_w_in, m_conv_b_in, m_conv_dw, m_conv_dw_b, m_conv_ln_g, m_conv_ln_b, m_conv_w_out, m_final_norm_g, v_norm_g, v_fox_w_in, v_fox_b_f, v_fox_w_out, v_conv_w_in, v_conv_b_in, v_conv_dw, v_conv_dw_b, v_conv_ln_g, v_conv_ln_b, v_conv_w_out, v_final_norm_g):
    h0 = x[0]
    target = loss_target[0]
    S, D = h0.shape
    depth = norm_g.shape[0]
    n_fox, _, fin_shard = fox_w_in.shape
    n_conv, _, cin_shard = conv_w_in.shape
    heads = fox_b_f.shape[1]
    W = fox_w_out.shape[1] * N_DEV
    C = conv_w_out.shape[1] * N_DEV
    assert fin_shard * N_DEV == 4 * W + heads and cin_shard * N_DEV == 3 * C and heads <= LANES
    is_fox = lambda i: i % 2 == 0

    shards = {}
    for i in range(depth):
        j = i // 2
        w_in, w_out = (fox_w_in, fox_w_out) if is_fox(i) else (conv_w_in, conv_w_out)
        shards[i] = (w_in[j].astype(BF16), w_out[j].astype(BF16))
    split = (11 * D) // 16

    def carried(*args, carry=None, **kw):
        r = _matmul(*args, carry=carry, **kw)
        return r if carry is not None else (r, None)

    small_shapes = [conv_b_in.shape, conv_dw.shape, conv_dw_b.shape, conv_ln_g.shape, conv_ln_b.shape]
    small = _flat_rows([conv_b_in, conv_dw, conv_dw_b, conv_ln_g, conv_ln_b])
    small_all = _exchange(small, "ag_small", "gather")
    small_axes = [1, 2, 1, 1, 1]
    conv_b_in_f, conv_dw_f, conv_dw_b_f, conv_ln_g_f, conv_ln_b_f = _unpack_shards(
        small_all.reshape(N_DEV, -1), small_shapes, small_axes)

    gathered_in = {0: _exchange(shards[0][0], "ag_w_in0", "gather")}
    gathered_out = {}

    h = h0
    saved = []
    for i in range(depth):
        j = i // 2
        has_next = i + 1 < depth
        gather = lambda x, kind="gather": (x, kind) if has_next else None
        hn, hn_t = _rms_fwd(h, norm_g[i:i + 1], f"rms_fwd{i}")
        w = gathered_in.pop(i)
        if is_fox(i):
            w = jnp.transpose(w, (1, 0, 2)).reshape(D, -1)
            w_qkvg, w_f = w[:, :4 * W], _pad_lanes(w[:, 4 * W:])
            qkv, got_in = carried(hn, w_qkvg, "nn", BF16, f"fox_qkv{i}", n_out=3 * W,
                                  carry=gather(shards[i + 1][0] if has_next else None, "gather_cols"))
            gate, got = carried(hn, w_qkvg, "nn", F32, f"fox_gate{i}", b_col_off=3 * W, n_out=W,
                                carry=(shards[i][1], "gather"))
            w_out = got.reshape(W, D)
            f = _matmul(hn, w_f, "nn", F32, f"fox_f{i}")
            b_f = _pad_lanes(fox_b_f[j:j + 1])
            c = _fgate_fwd(f, b_f, f"fgate_fwd{i}")
            key_bias = lambda t: (-c[:, :heads]).T.reshape(heads, S // t, 1, t)
            y, o, lse = _flash_fwd(qkv, gate, key_bias(_tile(S, FLASH_FWD_TILE)), heads, f"flash_fwd{i}")
            cneg = key_bias(_tile(S, FLASH_BWD_TILE))
            h_next, got_out = carried(y, w_out, "nn", F32, f"out_proj{i}", add=h,
                                      carry=gather(shards[i + 1][1] if has_next else None))
            if has_next:
                gathered_in[i + 1], gathered_out[i + 1] = got_in, got_out
            saved.append(dict(h=h, hn_t=hn_t, qkv=qkv, gate=gate, f=f, b_f=b_f, cneg=cneg, y=y, o=o, lse=lse,
                              w_qkvg=w_qkvg, w_f=w_f, w_out=w_out))
        else:
            w_out = gathered_out.pop(i).reshape(C, D)
            nxt = shards[i + 1][0] if has_next else None
            proj, got_a = carried(hn, w, "nn", F32, f"conv_in{i}", bias=conv_b_in_f[j:j + 1],
                                  carry=gather(nxt[:split] if has_next else None))
            y, u1 = _conv_fwd(proj, conv_dw_f[j], conv_dw_b_f[j:j + 1], conv_ln_g_f[j:j + 1],
                              conv_ln_b_f[j:j + 1], f"conv_fwd{i}")
            h_next, got_b = carried(y, w_out, "nn", F32, f"out_proj{i}", add=h,
                                    carry=gather(nxt[split:] if has_next else None))
            if has_next:
                gathered_in[i + 1] = jnp.concatenate([got_a, got_b], axis=1)
            saved.append(dict(h=h, hn_t=hn_t, proj=proj, y=y, u1=u1, w_in=w, w_out=w_out))
        h = h_next

    dh, dh16, dh16_t, loss_part, dg_final = _loss_head(h, final_norm_g[None, :], target, "loss_head")

    def shard_cols(g, shard):
        return jnp.transpose(g.reshape(g.shape[0], N_DEV, shard), (1, 0, 2))

    d_norm_g = [None] * depth
    d_fox_b_f = [None] * n_fox
    d_conv_small = [None] * n_conv
    summed_in = [None] * depth
    summed_out = [None] * depth
    pend_in = pend_out = None
    early_in = None
    for i in reversed(range(depth)):
        j = i // 2
        sv = saved[i]
        scatter = lambda x, kind="scatter": (x, kind) if x is not None else None
        dy = _matmul(dh16, sv["w_out"], "nt", F32, f"d_out_proj{i}")
        dw_out_t, got = carried(dh16_t, sv["y"], "nn", F32, f"dw_out{i}", carry=scatter(pend_out))
        if pend_out is not None:
            summed_out[i + 1] = got
        own_out = dw_out_t.T.reshape(N_DEV, -1, D).astype(BF16)
        if is_fox(i):
            do, dgate, delta = _fox_gate_bwd(dy, sv["o"], sv["gate"], f"fox_gate_bwd{i}")
            dq, dk, dv, colsum, rowsum = _flash_bwd(sv["qkv"], do, delta, sv["lse"], sv["cneg"], heads,
                                                    f"flash_bwd{i}")
            dc = _pad_lanes(rowsum[:, :heads] - jnp.sum(colsum, axis=2).reshape(heads, S).T)
            df, dbf = _fgate_bwd(dc, sv["f"], sv["b_f"], f"fgate_bwd{i}")
            dproj = jnp.concatenate([dq, dk, dv, dgate], axis=1)
            dhn = _matmul(df, sv["w_f"], "nt", F32, f"d_fox_f{i}")
            dhn, got = carried(dproj, sv["w_qkvg"], "nt", F32, f"d_fox_in{i}", add=dhn,
                               carry=scatter(pend_in, "scatter_cols"))
            if pend_in is not None:
                summed_in[i + 1] = got
            dw_f = _matmul(sv["hn_t"], df, "nn", F32, f"dw_fox_f{i}")[:, :heads]
            parts_of = lambda dw, rows: shard_cols(jnp.concatenate([dw, dw_f[rows]], axis=1), fin_shard).astype(BF16)
            if i > 0:
                dw_qkvg, summed_out[i] = carried(sv["hn_t"], dproj, "nn", F32, f"dw_fox_in{i}",
                                                 carry=(own_out, "scatter"))
                pend_in = parts_of(dw_qkvg, slice(None))
            else:
                top, bottom = slice(0, D // 2), slice(D // 2, D)
                dw_top, summed_out[i] = carried(sv["hn_t"][top], dproj, "nn", F32, f"dw_fox_in{i}",
                                                carry=(own_out, "scatter"))
                dw_bottom, early_in = carried(sv["hn_t"][bottom], dproj, "nn", F32, f"dw_fox_in{i}_rest",
                                              carry=(parts_of(dw_top, top), "scatter"))
                pend_in = parts_of(dw_bottom, bottom)
            pend_out = None
            d_fox_b_f[j] = jnp.sum(dbf, axis=0)[:heads]
        else:
            du1, dgate, nsums = _conv_bwd_norm(dy, sv["proj"], sv["u1"], conv_ln_g_f[j:j + 1],
                                               conv_ln_b_f[j:j + 1], f"conv_bwd_norm{i}")
            da, db, ddw, absums = _conv_bwd_taps(du1, sv["proj"], conv_dw_f[j], f"conv_bwd_taps{i}")
            dproj = jnp.concatenate([da, db, dgate], axis=1)
            half = D // 2
            dhn, got_a = carried(dproj, sv["w_in"], "nt", F32, f"d_conv_in{i}",
                                 carry=scatter(pend_in[:, :half] if pend_in is not None else None))
            dw_in, got_b = carried(sv["hn_t"], dproj, "nn", BF16, f"dw_conv_in{i}",
                                   carry=scatter(pend_in[:, half:] if pend_in is not None else None))
            if pend_in is not None:
                summed_in[i + 1] = jnp.concatenate([got_a, got_b], axis=1)
            pend_in, pend_out = dw_in, own_out
            nsum = jnp.sum(nsums, axis=1)
            absum = jnp.sum(absums, axis=1)
            d_conv_small[j] = dict(b_in=jnp.concatenate([absum[0], absum[1], nsum[3]]),
                                   dw=jnp.sum(ddw, axis=1), dw_b=nsum[2], ln_g=nsum[0], ln_b=nsum[1])
        dh, dh16, dh16_t, dg = _rms_bwd(sv["h"], norm_g[i:i + 1], dhn, dh, f"rms_bwd{i}")
        d_norm_g[i] = jnp.sum(dg, axis=0)
    summed_in[0] = _exchange(pend_in, "rs_w_in0", "scatter")
    if early_in is not None:
        summed_in[0] = jnp.concatenate([early_in, summed_in[0]], axis=1)
    if pend_out is not None:
        summed_out[0] = _exchange(pend_out, "rs_w_out0", "scatter")
    grad_x = dh[None]

    p_small = _pack_shards([jnp.stack([s[name] for s in d_conv_small]) for name in ("b_in", "dw", "dw_b", "ln_g", "ln_b")],
                           small_axes)
    p_small = jnp.pad(p_small, ((0, 0), (0, small.size - p_small.shape[1]))).reshape((N_DEV,) + small.shape)
    r_small = _exchange(p_small, "rs_small", "scatter")

    rep_shapes = [norm_g.shape, fox_b_f.shape, final_norm_g.shape, (1,)]
    rep_part = _flat_rows([jnp.stack(d_norm_g), jnp.stack(d_fox_b_f), jnp.sum(dg_final, axis=0),
                           jnp.sum(loss_part[:, 0])[None]])
    r_rep = _exchange(rep_part, "ag_replicated", "gather")

    def update(parts, w, m, v, name):
        two_d = (-1, w.shape[-1])
        stacked = jnp.stack(parts, axis=1).reshape((N_DEV,) + w.reshape(two_d).shape)
        res = _adamw(stacked, w.reshape(two_d), m.reshape(two_d), v.reshape(two_d), name)
        return [r.reshape(w.shape) for r in res]

    fox_layers = [i for i in range(depth) if is_fox(i)]
    conv_layers = [i for i in range(depth) if not is_fox(i)]
    u_fin = update([summed_in[i] for i in fox_layers], fox_w_in, m_fox_w_in, v_fox_w_in, "adamw_fox_w_in")
    u_fout = update([summed_out[i] for i in fox_layers], fox_w_out, m_fox_w_out, v_fox_w_out, "adamw_fox_w_out")
    u_cin = update([summed_in[i] for i in conv_layers], conv_w_in, m_conv_w_in, v_conv_w_in, "adamw_conv_w_in")
    u_cout = update([summed_out[i] for i in conv_layers], conv_w_out, m_conv_w_out, v_conv_w_out,
                    "adamw_conv_w_out")
    u_small = _adamw(r_small, small,
                     _flat_rows([m_conv_b_in, m_conv_dw, m_conv_dw_b, m_conv_ln_g, m_conv_ln_b]),
                     _flat_rows([v_conv_b_in, v_conv_dw, v_conv_dw_b, v_conv_ln_g, v_conv_ln_b]), "adamw_small")
    zero1 = jnp.zeros((1,), F32)
    u_rep = _adamw(r_rep, _flat_rows([norm_g, fox_b_f, final_norm_g, zero1]),
                   _flat_rows([m_norm_g, m_fox_b_f, m_final_norm_g, zero1]),
                   _flat_rows([v_norm_g, v_fox_b_f, v_final_norm_g, zero1]), "adamw_replicated")

    outs = []
    loss = None
    for kind in range(4):
        b_in_k, dw_k, dwb_k, lng_k, lnb_k = _unflat(u_small[kind], small_shapes)
        ng_k, bf_k, fg_k, loss_k = _unflat(u_rep[kind], rep_shapes)
        if kind == 0:
            loss = loss_k[0]
        outs += [ng_k, u_fin[kind], bf_k, u_fout[kind], u_cin[kind], b_in_k, dw_k, dwb_k, lng_k, lnb_k,
                 u_cout[kind], fg_k]
    return (loss, grad_x, *outs)
```

```python
import functools

import numpy as np
import jax
import jax.numpy as jnp
from jax import lax
from jax.experimental import pallas as pl
from jax.experimental.pallas import tpu as pltpu

F32 = jnp.float32
BF16 = jnp.bfloat16
MESH_ID = pl.DeviceIdType.MESH

N_DEV = 8
RMS_EPS = 1e-6
LN_EPS = 1e-5
ADAM_LR = 0.001
ADAM_B1 = 0.9
ADAM_B2 = 0.999
ADAM_EPS = 1e-08
ADAM_WD = 0.01
ADAM_STEP = 10

LANES = 128
SUBLANES = 8
VMEM_LIMIT = 60 * 1024 * 1024
NEG_BIG = -1e30
CONV_HALO = 32
FLASH_FWD_TILE = 1024
FLASH_BWD_TILE = 1024
CONV_ROWS = 128
CONV_CHUNK = 32
FGATE_ROWS = 256
ROW_TILE = 256


def _params(*sem):
    return pltpu.CompilerParams(dimension_semantics=sem if sem else None, vmem_limit_bytes=VMEM_LIMIT)


def _tile(n, pref):
    if n <= pref:
        return n
    t = pref
    while n % t:
        t //= 2
    return t


def _sigmoid(x):
    return 1.0 / (1.0 + jnp.exp(-x))


def _dsilu(x, s):
    return s * (1.0 + x * (1.0 - s))


def _rows(n, chunk, fn):
    def step(i, carry):
        fn(pl.multiple_of(i * chunk, chunk))
        return carry
    lax.fori_loop(0, n // chunk, step, 0)


def _peer(k):
    x, y, c = lax.axis_index("x"), lax.axis_index("y"), lax.axis_index("c")
    px = 1 - x if (k >> 2) & 1 else x
    py = 1 - y if (k >> 1) & 1 else y
    pc = 1 - c if k & 1 else c
    return (px, py, pc), 4 * px + 2 * py + pc


def _exchange_copies(x_ref, o_ref, send_sems, recv_sems, local_sem, kind):
    _, me = _peer(0)

    def window(ref, idx, n):
        assert n % LANES == 0
        return ref.at[:, pl.ds(pl.multiple_of(idx * n, LANES), n)]

    def src(idx):
        if kind == "scatter":
            return x_ref.at[idx]
        return window(x_ref, idx, o_ref.shape[2]) if kind == "scatter_cols" else x_ref

    def dst(idx):
        return window(o_ref, idx, x_ref.shape[1]) if kind == "gather_cols" else o_ref.at[idx]

    local = pltpu.make_async_copy(src(me), dst(me), local_sem)
    sends, arrivals = [], []
    for k in range(1, N_DEV):
        peer, pidx = _peer(k)
        sems = dict(send_sem=send_sems.at[k - 1], recv_sem=recv_sems.at[k - 1], device_id=peer,
                    device_id_type=MESH_ID)
        sends.append(pltpu.make_async_remote_copy(src_ref=src(pidx), dst_ref=dst(me), **sems))
        arrivals.append(pltpu.make_async_remote_copy(src_ref=src(me), dst_ref=dst(pidx), **sems))
    return local, sends, arrivals


def _exchange_start(*refs, kind):
    local, sends, _ = _exchange_copies(*refs, kind)
    local.start()
    for cp in sends:
        cp.start()


def _exchange_wait(*refs, kind):
    local, sends, arrivals = _exchange_copies(*refs, kind)
    for cp in arrivals:
        cp.wait_recv()
    for cp in sends:
        cp.wait_send()
    local.wait()


EXCHANGE_SCRATCH = [pltpu.SemaphoreType.DMA((N_DEV - 1,)), pltpu.SemaphoreType.DMA((N_DEV - 1,)),
                    pltpu.SemaphoreType.DMA]


def _exchange_shape(x, kind):
    shape = {"gather": lambda: (N_DEV,) + x.shape, "scatter": lambda: x.shape,
             "gather_cols": lambda: (x.shape[0], N_DEV * x.shape[1]),
             "scatter_cols": lambda: (N_DEV, x.shape[0], x.shape[1] // N_DEV)}[kind]()
    return jax.ShapeDtypeStruct(shape, x.dtype)


def _exchange(x, name, kind):
    def body(*refs):
        _exchange_start(*refs, kind=kind)
        _exchange_wait(*refs, kind=kind)

    return pl.pallas_call(
        body, name=name,
        out_shape=_exchange_shape(x, kind),
        in_specs=[pl.BlockSpec(memory_space=pl.ANY)],
        out_specs=pl.BlockSpec(memory_space=pl.ANY),
        scratch_shapes=list(EXCHANGE_SCRATCH),
    )(x)


def _matmul(a, b, mode, out_dtype, name, bias=None, add=None, b_col_off=0, n_out=None, carry=None):
    M, K = a.shape
    N = n_out if n_out is not None else (b.shape[0] if mode == "nt" else b.shape[1])
    tm = _tile(M, 512)
    tn = _tile(N, 1024)
    k_cap = 4096 if a.dtype.itemsize == 2 and b.dtype.itemsize == 2 else 2048
    tk = next(K // d for d in range(1, K + 1) if K % d == 0 and K // d <= k_cap and (K // d) % LANES == 0)
    nm, nn, nk = M // tm, N // tn, K // tk
    assert b_col_off % tn == 0
    joff = b_col_off // tn
    dims = {"nn": (((1,), (0,)), ((), ())), "nt": (((1,), (1,)), ((), ()))}[mode]
    n_in = 2 + (bias is not None) + (add is not None)

    def body(*refs):
        a_ref, b_ref = refs[0], refs[1]
        bias_ref = refs[2] if bias is not None else None
        add_ref = refs[n_in - 1] if add is not None else None
        pos = n_in
        x_ref = o_ref = x_out_ref = None
        if carry is not None:
            x_ref, o_ref, x_out_ref = refs[pos], refs[pos + 1], refs[pos + 2]
            pos += 3
        else:
            o_ref = refs[pos]
            pos += 1
        acc_ref = None
        if nk > 1:
            acc_ref = refs[pos]
            pos += 1
        exchange_refs = (x_ref, x_out_ref) + tuple(refs[pos:])
        i, j, kk = pl.program_id(0), pl.program_id(1), pl.program_id(2)

        if carry is not None:
            @pl.when((i == 0) & (j == 0) & (kk == 0))
            def _():
                _exchange_start(*exchange_refs, kind=carry[1])

        part = lax.dot_general(a_ref[...].astype(BF16), b_ref[...].astype(BF16), dims,
                               preferred_element_type=F32)

        def finish(r):
            if bias_ref is not None:
                r = r + bias_ref[...]
            if add_ref is not None:
                r = r + add_ref[...]
            o_ref[...] = r.astype(o_ref.dtype)

        if nk == 1:
            finish(part)
        else:
            @pl.when(kk == 0)
            def _():
                acc_ref[...] = part

            @pl.when(kk > 0)
            def _():
                acc_ref[...] += part

            @pl.when(kk == nk - 1)
            def _():
                finish(acc_ref[...])

        if carry is not None:
            @pl.when((i == nm - 1) & (j == nn - 1) & (kk == nk - 1))
            def _():
                _exchange_wait(*exchange_refs, kind=carry[1])

    a_spec = pl.BlockSpec((tm, tk), lambda i, j, k: (i, k))
    if mode == "nt":
        b_spec = pl.BlockSpec((tn, tk), lambda i, j, k: (j, k))
    else:
        b_spec = pl.BlockSpec((tk, tn), lambda i, j, k: (k, j + joff))
    in_specs = [a_spec, b_spec]
    args = [a, b]
    if bias is not None:
        in_specs.append(pl.BlockSpec((1, tn), lambda i, j, k: (0, j)))
        args.append(bias)
    if add is not None:
        in_specs.append(pl.BlockSpec((tm, tn), lambda i, j, k: (i, j)))
        args.append(add)
    out_specs = [pl.BlockSpec((tm, tn), lambda i, j, k: (i, j))]
    out_shape = [jax.ShapeDtypeStruct((M, N), out_dtype)]
    scratch = [pltpu.VMEM((tm, tn), F32)] if nk > 1 else []
    if carry is not None:
        in_specs.append(pl.BlockSpec(memory_space=pl.ANY))
        args.append(carry[0])
        out_specs.append(pl.BlockSpec(memory_space=pl.ANY))
        out_shape.append(_exchange_shape(*carry))
        scratch += EXCHANGE_SCRATCH
    res = pl.pallas_call(
        body, name=name,
        grid=(nm, nn, nk),
        in_specs=in_specs, out_specs=out_specs, out_shape=out_shape, scratch_shapes=scratch,
        compiler_params=_params(*(("arbitrary",) * 3 if carry is not None else ("parallel", "parallel", "arbitrary"))),
    )(*args)
    return res if carry is not None else res[0]


def _rms_fwd(h, g, name):
    S, D = h.shape
    ts = _tile(S, 512)

    def body(h_ref, g_ref, o_ref, ot_ref):
        x = h_ref[...]
        r = lax.rsqrt(jnp.mean(x * x, axis=-1, keepdims=True) + RMS_EPS)
        y = x * r * g_ref[...]
        o_ref[...] = y.astype(BF16)
        ot_ref[...] = y.T.astype(BF16)

    return pl.pallas_call(
        body, name=name, grid=(S // ts,),
        in_specs=[pl.BlockSpec((ts, D), lambda i: (i, 0)), pl.BlockSpec((1, D), lambda i: (0, 0))],
        out_specs=[pl.BlockSpec((ts, D), lambda i: (i, 0)), pl.BlockSpec((D, ts), lambda i: (0, i))],
        out_shape=[jax.ShapeDtypeStruct((S, D), BF16), jax.ShapeDtypeStruct((D, S), BF16)],
        compiler_params=_params("parallel"),
    )(h, g)


def _rms_bwd_block(x, g, dy):
    r = lax.rsqrt(jnp.mean(x * x, axis=-1, keepdims=True) + RMS_EPS)
    xr = x * r
    t = dy * g
    dx = r * (t - xr * jnp.mean(t * xr, axis=-1, keepdims=True))
    return dx, dy * xr


def _rms_bwd(h, g, dhn, dh, name):
    S, D = h.shape
    ts = _tile(S, ROW_TILE)

    def body(h_ref, g_ref, dhn_ref, dh_ref, o_ref, o16_ref, o16t_ref, dg_ref):
        dx, dgt = _rms_bwd_block(h_ref[...], g_ref[...], dhn_ref[...])
        out = dh_ref[...] + dx
        o_ref[...] = out
        o16_ref[...] = out.astype(BF16)
        o16t_ref[...] = out.T.astype(BF16)
        part = jnp.sum(dgt.reshape(ts // SUBLANES, SUBLANES, D), axis=0)

        @pl.when(pl.program_id(0) == 0)
        def _():
            dg_ref[...] = part

        @pl.when(pl.program_id(0) > 0)
        def _():
            dg_ref[...] += part

    row = pl.BlockSpec((ts, D), lambda i: (i, 0))
    return pl.pallas_call(
        body, name=name, grid=(S // ts,),
        in_specs=[row, pl.BlockSpec((1, D), lambda i: (0, 0)), row, row],
        out_specs=[row, row, pl.BlockSpec((D, ts), lambda i: (0, i)),
                   pl.BlockSpec((SUBLANES, D), lambda i: (0, 0))],
        out_shape=[jax.ShapeDtypeStruct((S, D), F32), jax.ShapeDtypeStruct((S, D), BF16),
                   jax.ShapeDtypeStruct((D, S), BF16), jax.ShapeDtypeStruct((SUBLANES, D), F32)],
        compiler_params=_params("arbitrary"),
    )(h, g, dhn, dh)


def _loss_head(h, g, target, name):
    S, D = h.shape
    ts = _tile(S, ROW_TILE)

    def body(h_ref, g_ref, t_ref, o_ref, o16_ref, o16t_ref, loss_ref, dg_ref):
        x = h_ref[...]
        gg = g_ref[...]
        r = lax.rsqrt(jnp.mean(x * x, axis=-1, keepdims=True) + RMS_EPS)
        err = x * r * gg - t_ref[...]
        row_loss = 0.5 * jnp.mean(err * err, axis=-1, keepdims=True)
        dx, dgt = _rms_bwd_block(x, gg, err * (1.0 / D))
        o_ref[...] = dx
        o16_ref[...] = dx.astype(BF16)
        o16t_ref[...] = dx.T.astype(BF16)
        part = jnp.sum(dgt.reshape(ts // SUBLANES, SUBLANES, D), axis=0)
        lpart = jnp.sum(jnp.broadcast_to(row_loss, (ts, LANES)).reshape(ts // SUBLANES, SUBLANES, LANES), axis=0)

        @pl.when(pl.program_id(0) == 0)
        def _():
            dg_ref[...] = part
            loss_ref[...] = lpart

        @pl.when(pl.program_id(0) > 0)
        def _():
            dg_ref[...] += part
            loss_ref[...] += lpart

    row = pl.BlockSpec((ts, D), lambda i: (i, 0))
    return pl.pallas_call(
        body, name=name, grid=(S // ts,),
        in_specs=[row, pl.BlockSpec((1, D), lambda i: (0, 0)), row],
        out_specs=[row, row, pl.BlockSpec((D, ts), lambda i: (0, i)),
                   pl.BlockSpec((SUBLANES, LANES), lambda i: (0, 0)), pl.BlockSpec((SUBLANES, D), lambda i: (0, 0))],
        out_shape=[jax.ShapeDtypeStruct((S, D), F32), jax.ShapeDtypeStruct((S, D), BF16),
                   jax.ShapeDtypeStruct((D, S), BF16), jax.ShapeDtypeStruct((SUBLANES, LANES), F32),
                   jax.ShapeDtypeStruct((SUBLANES, D), F32)],
        compiler_params=_params("arbitrary"),
    )(h, g, target)


def _split3(x):
    hi = x.astype(BF16)
    r1 = x - hi.astype(F32)
    mid = r1.astype(BF16)
    lo = (r1 - mid.astype(F32)).astype(BF16)
    return hi, mid, lo


def _tri_sum(tri, x):
    hi, mid, lo = _split3(x)
    dot = functools.partial(jnp.dot, preferred_element_type=F32)
    return dot(tri, hi) + dot(tri, mid) + dot(tri, lo)


def _fgate_fwd(f, b_f, name):
    S = f.shape[0]
    tb = _tile(S, FGATE_ROWS)

    def body(f_ref, b_ref, c_ref, carry_ref):
        @pl.when(pl.program_id(0) == 0)
        def _():
            carry_ref[...] = jnp.zeros_like(carry_ref)

        x = f_ref[...] + b_ref[...]
        lf = jnp.minimum(x, 0.0) - jnp.log1p(jnp.exp(-jnp.abs(x)))
        r = lax.broadcasted_iota(jnp.int32, (tb, tb), 0)
        c = lax.broadcasted_iota(jnp.int32, (tb, tb), 1)
        tri = (c <= r).astype(BF16)
        c_ref[...] = _tri_sum(tri, lf) + carry_ref[0:1, :]
        carry_ref[...] += _tri_sum(jnp.ones((SUBLANES, tb), BF16), lf)

    return pl.pallas_call(
        body, name=name, grid=(S // tb,),
        in_specs=[pl.BlockSpec((tb, LANES), lambda i: (i, 0)), pl.BlockSpec((1, LANES), lambda i: (0, 0))],
        out_specs=pl.BlockSpec((tb, LANES), lambda i: (i, 0)),
        out_shape=jax.ShapeDtypeStruct((S, LANES), F32),
        scratch_shapes=[pltpu.VMEM((SUBLANES, LANES), F32)],
        compiler_params=_params("arbitrary"),
    )(f, b_f)


def _fgate_bwd(dc, f, b_f, name):
    S = f.shape[0]
    tb = _tile(S, FGATE_ROWS)
    nb = S // tb

    def body(dc_ref, f_ref, b_ref, df_ref, db_ref, carry_ref):
        @pl.when(pl.program_id(0) == 0)
        def _():
            carry_ref[...] = jnp.zeros_like(carry_ref)
            db_ref[...] = jnp.zeros_like(db_ref)

        d = dc_ref[...]
        r = lax.broadcasted_iota(jnp.int32, (tb, tb), 0)
        c = lax.broadcasted_iota(jnp.int32, (tb, tb), 1)
        tri = (c >= r).astype(BF16)
        dlf = _tri_sum(tri, d) + carry_ref[0:1, :]
        carry_ref[...] += _tri_sum(jnp.ones((SUBLANES, tb), BF16), d)
        df = dlf * _sigmoid(-(f_ref[...] + b_ref[...]))
        df_ref[...] = df
        db_ref[...] += jnp.sum(df.reshape(tb // SUBLANES, SUBLANES, LANES), axis=0)

    rev = pl.BlockSpec((tb, LANES), lambda i: (nb - 1 - i, 0))
    return pl.pallas_call(
        body, name=name, grid=(nb,),
        in_specs=[rev, rev, pl.BlockSpec((1, LANES), lambda i: (0, 0))],
        out_specs=[rev, pl.BlockSpec((SUBLANES, LANES), lambda i: (0, 0))],
        out_shape=[jax.ShapeDtypeStruct((S, LANES), F32), jax.ShapeDtypeStruct((SUBLANES, LANES), F32)],
        scratch_shapes=[pltpu.VMEM((SUBLANES, LANES), F32)],
        compiler_params=_params("arbitrary"),
    )(dc, f, b_f)


LOG2E = 1.4426950408889634
NT_DIMS = (((1,), (1,)), ((), ()))
TN_DIMS = (((0,), (0,)), ((), ()))


def _causal_mask(t):
    r = lax.broadcasted_iota(jnp.int32, (t, t), 0)
    c = lax.broadcasted_iota(jnp.int32, (t, t), 1)
    return r >= c


def _key_tiles(qi, tile):
    group = 4

    def several(i, carry):
        for u in range(group):
            tile(group * i + u, False)
        return carry
    lax.fori_loop(0, qi // group, several, 0)

    def single(j, carry):
        tile(j, False)
        return carry
    lax.fori_loop((qi // group) * group, qi, single, 0)
    tile(qi, True)


def _flash_fwd(qkv, gate, cneg, heads, name):
    S, W3 = qkv.shape
    W = W3 // 3
    dh = W // heads
    assert dh == LANES
    tq = _tile(S, FLASH_FWD_TILE)
    nq = S // tq
    c1 = dh ** -0.5 * LOG2E

    def body(q_ref, k_ref, v_ref, b_ref, g_ref, y_ref, o_ref, lse_ref, m_scr, l_scr, acc_scr):
        qi = pl.program_id(1)
        m_scr[...] = jnp.full_like(m_scr, NEG_BIG)
        l_scr[...] = jnp.zeros_like(l_scr)
        acc_scr[...] = jnp.zeros_like(acc_scr)
        q = (q_ref[...].astype(F32) * c1).astype(BF16)

        def tile(j, diagonal):
            rows = pl.ds(pl.multiple_of(j * tq, tq), tq)
            t = lax.dot_general(q, k_ref[rows, :], NT_DIMS, preferred_element_type=F32) + b_ref[0, j] * LOG2E
            if diagonal:
                t = jnp.where(_causal_mask(tq), t, NEG_BIG)
            m_prev = m_scr[...]
            m_next = jnp.maximum(m_prev, jnp.max(t, axis=1, keepdims=True))
            alpha = jnp.exp2(m_prev - m_next)
            p = jnp.exp2(t - m_next[:, :1])
            l_scr[...] = alpha * l_scr[...] + jnp.sum(p, axis=1, keepdims=True)
            acc_scr[...] = alpha * acc_scr[...] + jnp.dot(p.astype(BF16), v_ref[rows, :],
                                                          preferred_element_type=F32)
            m_scr[...] = m_next

        _key_tiles(qi, tile)
        l = l_scr[...]
        o = acc_scr[...] / l
        g = g_ref[...]
        o_ref[...] = o
        lse_ref[...] = m_scr[...] + jnp.log2(l)
        y_ref[...] = (o * (g * _sigmoid(g))).astype(BF16)

    qblk = pl.BlockSpec((tq, dh), lambda h, i: (i, h))
    return pl.pallas_call(
        body, name=name, grid=(heads, nq),
        in_specs=[qblk,
                  pl.BlockSpec((S, dh), lambda h, i: (0, heads + h)),
                  pl.BlockSpec((S, dh), lambda h, i: (0, 2 * heads + h)),
                  pl.BlockSpec((1, nq, 1, tq), lambda h, i: (h, 0, 0, 0)),
                  qblk],
        out_specs=[qblk, qblk, qblk],
        out_shape=[jax.ShapeDtypeStruct((S, W), BF16), jax.ShapeDtypeStruct((S, W), F32),
                   jax.ShapeDtypeStruct((S, W), F32)],
        scratch_shapes=[pltpu.VMEM((tq, LANES), F32), pltpu.VMEM((tq, LANES), F32), pltpu.VMEM((tq, dh), F32)],
        compiler_params=_params("parallel", "arbitrary"),
    )(qkv, qkv, qkv, cneg, gate)


def _flash_bwd(qkv, do, delta, lse, cneg, heads, name):
    S, W3 = qkv.shape
    W = W3 // 3
    dh = W // heads
    tq = _tile(S, FLASH_BWD_TILE)
    nq = S // tq
    scale = dh ** -0.5
    c1 = scale * LOG2E

    def body(q_ref, k_ref, v_ref, b_ref, do_ref, delta_ref, lse_ref,
             dq_ref, dk_ref, dv_ref, db_ref, rs_ref, dq_scr, dk_scr, dv_scr, rs_scr):
        h, qi = pl.program_id(0), pl.program_id(1)

        @pl.when((h == 0) & (qi == 0))
        def _():
            rs_ref[...] = jnp.zeros_like(rs_ref)

        @pl.when(qi == 0)
        def _():
            dk_scr[...] = jnp.zeros_like(dk_scr)
            dv_scr[...] = jnp.zeros_like(dv_scr)
            db_ref[...] = jnp.zeros_like(db_ref)

        dq_scr[...] = jnp.zeros_like(dq_scr)
        rs_scr[...] = jnp.zeros_like(rs_scr)
        q, d_o = q_ref[...], do_ref[...]
        q_scaled = (q.astype(F32) * c1).astype(BF16)
        lse = lse_ref[...][:, :1]
        delta = delta_ref[...][:, :1]

        def tile(j, diagonal):
            rows = pl.ds(pl.multiple_of(j * tq, tq), tq)
            k, v = k_ref[rows, :], v_ref[rows, :]
            t = lax.dot_general(q_scaled, k, NT_DIMS, preferred_element_type=F32) + b_ref[0, j] * LOG2E
            p = jnp.exp2(t - lse)
            if diagonal:
                p = jnp.where(_causal_mask(tq), p, 0.0)
            dp = lax.dot_general(d_o, v, NT_DIMS, preferred_element_type=F32)
            ds = p * (dp - delta)
            dv_scr[rows, :] += lax.dot_general(p.astype(BF16), d_o, TN_DIMS, preferred_element_type=F32)
            db_ref[0, j] += jnp.sum(ds.reshape(tq // SUBLANES, SUBLANES, tq), axis=0)
            dsb = (ds * scale).astype(BF16)
            dk_scr[rows, :] += lax.dot_general(dsb, q, TN_DIMS, preferred_element_type=F32)
            dq_scr[...] += jnp.dot(dsb, k, preferred_element_type=F32)
            rs_scr[...] += jnp.sum(ds, axis=1, keepdims=True)

        _key_tiles(qi, tile)
        dq_ref[...] = dq_scr[...].astype(BF16)
        lane = lax.broadcasted_iota(jnp.int32, (tq, LANES), 1)
        rs_ref[pl.ds(pl.multiple_of(qi * tq, tq), tq), :] += jnp.where(lane == h, rs_scr[...], 0.0)

        @pl.when(qi == nq - 1)
        def _():
            dk_ref[...] = dk_scr[...].astype(BF16)
            dv_ref[...] = dv_scr[...].astype(BF16)

    qblk = pl.BlockSpec((tq, dh), lambda h, i: (i, h))
    once = dict(pipeline_mode=pl.Buffered(1))
    head = pl.BlockSpec((S, dh), lambda h, i: (0, h), **once)
    return pl.pallas_call(
        body, name=name, grid=(heads, nq),
        in_specs=[qblk,
                  pl.BlockSpec((S, dh), lambda h, i: (0, heads + h), **once),
                  pl.BlockSpec((S, dh), lambda h, i: (0, 2 * heads + h), **once),
                  pl.BlockSpec((1, nq, 1, tq), lambda h, i: (h, 0, 0, 0)),
                  qblk, qblk, qblk],
        out_specs=[qblk, head, head,
                   pl.BlockSpec((1, nq, SUBLANES, tq), lambda h, i: (h, 0, 0, 0)),
                   pl.BlockSpec((S, LANES), lambda h, i: (0, 0), **once)],
        out_shape=[jax.ShapeDtypeStruct((S, W), BF16), jax.ShapeDtypeStruct((S, W), BF16),
                   jax.ShapeDtypeStruct((S, W), BF16), jax.ShapeDtypeStruct((heads, nq, SUBLANES, tq), F32),
                   jax.ShapeDtypeStruct((S, LANES), F32)],
        scratch_shapes=[pltpu.VMEM((tq, dh), F32), pltpu.VMEM((S, dh), F32), pltpu.VMEM((S, dh), F32),
                        pltpu.VMEM((tq, LANES), F32)],
        compiler_params=_params("arbitrary", "arbitrary"),
    )(qkv, qkv, qkv, cneg, do, delta, lse)


def _fox_gate_bwd(dy, o, gate, name):
    S, W = dy.shape
    ts = _tile(S, ROW_TILE)

    def body(dy_ref, o_ref, g_ref, do_ref, dg_ref, delta_ref):
        d, g, o_val = dy_ref[...], g_ref[...], o_ref[...]
        sg = _sigmoid(g)
        d_o = (d * (g * sg)).astype(BF16)
        do_ref[...] = d_o
        dg_ref[...] = (d * o_val * _dsilu(g, sg)).astype(BF16)
        prod = d_o.astype(F32) * o_val
        for h in range(W // LANES):
            cols = slice(h * LANES, (h + 1) * LANES)
            delta_ref[:, cols] = jnp.broadcast_to(jnp.sum(prod[:, cols], axis=1, keepdims=True), (ts, LANES))

    row = pl.BlockSpec((ts, W), lambda i: (i, 0))
    return pl.pallas_call(
        body, name=name, grid=(S // ts,),
        in_specs=[row, row, row], out_specs=[row, row, row],
        out_shape=[jax.ShapeDtypeStruct((S, W), BF16), jax.ShapeDtypeStruct((S, W), BF16),
                   jax.ShapeDtypeStruct((S, W), F32)],
        compiler_params=_params("parallel"),
    )(dy, o, gate)


def _shifted_copies(buf_ref, sh_ref, rows):
    for j in range(1, SUBLANES):
        sh_ref[j, 0:rows, :] = buf_ref[j:j + rows, :]


def _tap(buf_ref, sh_ref, r0, off, cols):
    j, base = off % SUBLANES, off - off % SUBLANES
    if j == 0:
        return buf_ref[pl.ds(r0 + base, SUBLANES), cols]
    return sh_ref[j, pl.ds(r0 + base, SUBLANES), cols]


def _tap_weights(dw_ref, cols):
    return [jnp.broadcast_to(dw_ref[k:k + 1, cols], (SUBLANES, LANES)) for k in range(dw_ref.shape[0])]


def _conv_fwd(proj, dw, dw_b, ln_g, ln_b, name):
    S, C3 = proj.shape
    C = C3 // 3
    K = dw.shape[0]
    assert K - 1 <= CONV_HALO - 2
    ts = _tile(S, CONV_ROWS)
    hb = ts // CONV_HALO
    nrows = ts + CONV_HALO
    lead = CONV_HALO - (K - 1)

    def body(a_ref, b_ref, ah_ref, bh_ref, g_ref, dw_ref, dwb_ref, lg_ref, lb_ref, y_ref, u1_ref, buf, sh):
        first = pl.program_id(0) == 0
        buf[0:CONV_HALO, :] = jnp.where(first, 0.0, ah_ref[...] * _sigmoid(bh_ref[...]))

        def glu(r0):
            rows = pl.ds(r0, CONV_CHUNK)
            buf[pl.ds(r0 + CONV_HALO, CONV_CHUNK), :] = a_ref[rows, :] * _sigmoid(b_ref[rows, :])
        _rows(ts, CONV_CHUNK, glu)
        _shifted_copies(buf, sh, nrows - SUBLANES)

        for s in range(C // LANES):
            cols = slice(s * LANES, (s + 1) * LANES)
            w = _tap_weights(dw_ref, cols)
            bias = jnp.broadcast_to(dwb_ref[:, cols], (SUBLANES, LANES))

            def taps(r0, w=w, bias=bias, cols=cols):
                for u in range(CONV_CHUNK // SUBLANES):
                    r = r0 + u * SUBLANES
                    acc = bias
                    for k in range(K):
                        acc = acc + w[k] * _tap(buf, sh, r, lead + k, cols)
                    u1_ref[pl.ds(r, SUBLANES), cols] = acc
            _rows(ts, CONV_CHUNK, taps)

        def norm(r0):
            rows = pl.ds(r0, CONV_CHUNK)
            u1 = u1_ref[rows, :]
            mu = jnp.mean(u1, axis=-1, keepdims=True)
            xc = u1 - mu
            rstd = lax.rsqrt(jnp.mean(xc * xc, axis=-1, keepdims=True) + LN_EPS)
            z = xc * rstd * lg_ref[...] + lb_ref[...]
            g = g_ref[rows, :]
            y_ref[rows, :] = ((z * _sigmoid(z)) * (g * _sigmoid(g))).astype(BF16)
        _rows(ts, CONV_CHUNK, norm)

    row = lambda col: pl.BlockSpec((ts, C), lambda i: (i, col))
    halo = lambda col: pl.BlockSpec((CONV_HALO, C), lambda i: (jnp.maximum(i * hb - 1, 0), col))
    vec = pl.BlockSpec((1, C), lambda i: (0, 0))
    return pl.pallas_call(
        body, name=name, grid=(S // ts,),
        in_specs=[row(0), row(1), halo(0), halo(1), row(2), pl.BlockSpec((K, C), lambda i: (0, 0)), vec, vec, vec],
        out_specs=[row(0), row(0)],
        out_shape=[jax.ShapeDtypeStruct((S, C), BF16), jax.ShapeDtypeStruct((S, C), F32)],
        scratch_shapes=[pltpu.VMEM((nrows, C), F32), pltpu.VMEM((SUBLANES, nrows, C), F32)],
        compiler_params=_params("parallel"),
    )(proj, proj, proj, proj, proj, dw, dw_b, ln_g, ln_b)


def _conv_bwd_norm(dy, proj, u1, ln_g, ln_b, name):
    S, C = dy.shape
    ts = _tile(S, ROW_TILE)
    groups = CONV_CHUNK // SUBLANES

    def fold(x):
        return jnp.sum(x.reshape(groups, SUBLANES, C), axis=0)

    def body(dy_ref, g_ref, u1_ref, lg_ref, lb_ref, du1_ref, dg_ref, sums_ref):
        @pl.when(pl.program_id(0) == 0)
        def _():
            sums_ref[...] = jnp.zeros_like(sums_ref)

        def chunk(r0):
            rows = pl.ds(r0, CONV_CHUNK)
            d, g, u1 = dy_ref[rows, :], g_ref[rows, :], u1_ref[rows, :]
            mu = jnp.mean(u1, axis=-1, keepdims=True)
            xc = u1 - mu
            rstd = lax.rsqrt(jnp.mean(xc * xc, axis=-1, keepdims=True) + LN_EPS)
            xh = xc * rstd
            z = xh * lg_ref[...] + lb_ref[...]
            sz, sg = _sigmoid(z), _sigmoid(g)
            dgate = d * (z * sz) * _dsilu(g, sg)
            dz = d * (g * sg) * _dsilu(z, sz)
            dxh = dz * lg_ref[...]
            du1 = rstd * (dxh - jnp.mean(dxh, axis=-1, keepdims=True)
                          - xh * jnp.mean(dxh * xh, axis=-1, keepdims=True))
            du1_ref[rows, :] = du1
            dg_ref[rows, :] = dgate.astype(BF16)
            sums_ref[0] += fold(dz * xh)
            sums_ref[1] += fold(dz)
            sums_ref[2] += fold(du1)
            sums_ref[3] += fold(dgate)
        _rows(ts, CONV_CHUNK, chunk)

    row = pl.BlockSpec((ts, C), lambda i: (i, 0))
    vec = pl.BlockSpec((1, C), lambda i: (0, 0))
    return pl.pallas_call(
        body, name=name, grid=(S // ts,),
        in_specs=[row, pl.BlockSpec((ts, C), lambda i: (i, 2)), row, vec, vec],
        out_specs=[row, row, pl.BlockSpec((4, SUBLANES, C), lambda i: (0, 0, 0))],
        out_shape=[jax.ShapeDtypeStruct((S, C), F32), jax.ShapeDtypeStruct((S, C), BF16),
                   jax.ShapeDtypeStruct((4, SUBLANES, C), F32)],
        compiler_params=_params("arbitrary"),
    )(dy, proj, u1, ln_g, ln_b)


def _conv_bwd_taps(du1, proj, dw, name):
    S, C = du1.shape
    K = dw.shape[0]
    ts = _tile(S, CONV_ROWS)
    hb = ts // CONV_HALO
    nblk = S // ts
    last_halo = S // CONV_HALO - 1
    nrows = ts + CONV_HALO
    groups = CONV_CHUNK // SUBLANES

    def body(d_ref, dn_ref, a_ref, b_ref, dw_ref, da_ref, db_ref, ddw_ref, sums_ref, dbuf, dsh):
        i = pl.program_id(0)

        @pl.when(i == 0)
        def _():
            ddw_ref[...] = jnp.zeros_like(ddw_ref)
            sums_ref[...] = jnp.zeros_like(sums_ref)

        dbuf[ts:nrows, :] = jnp.where(i == nblk - 1, 0.0, dn_ref[...])

        def fill(r0):
            rows = pl.ds(r0, CONV_CHUNK)
            dbuf[rows, :] = d_ref[rows, :]
        _rows(ts, CONV_CHUNK, fill)
        _shifted_copies(dbuf, dsh, nrows - SUBLANES)

        zero = jnp.zeros((SUBLANES, LANES), F32)
        fold = lambda x: jnp.sum(x.reshape(groups, SUBLANES, LANES), axis=0)
        for s in range(C // LANES):
            cols = slice(s * LANES, (s + 1) * LANES)
            w = _tap_weights(dw_ref, cols)

            def step(j, carry, w=w, cols=cols):
                r0 = pl.multiple_of(j * CONV_CHUNK, CONV_CHUNK)
                rows = pl.ds(r0, CONV_CHUNK)
                a, sb = a_ref[rows, cols], _sigmoid(b_ref[rows, cols])
                u0 = a * sb
                accs = list(carry[2:])
                parts = []
                for u in range(groups):
                    u0_u = u0[u * SUBLANES:(u + 1) * SUBLANES]
                    acc = zero
                    for k in range(K):
                        x = _tap(dbuf, dsh, r0 + u * SUBLANES, K - 1 - k, cols)
                        acc = acc + w[k] * x
                        accs[k] = accs[k] + u0_u * x
                    parts.append(acc)
                da = jnp.concatenate(parts, axis=0) * sb
                db = da * a * (1.0 - sb)
                da_ref[rows, cols] = da.astype(BF16)
                db_ref[rows, cols] = db.astype(BF16)
                return (carry[0] + fold(da), carry[1] + fold(db), *accs)
            out = lax.fori_loop(0, ts // CONV_CHUNK, step, (zero,) * (K + 2))
            sums_ref[0, :, cols] += out[0]
            sums_ref[1, :, cols] += out[1]
            for k in range(K):
                ddw_ref[k, :, cols] += out[2 + k]

    row = lambda col: pl.BlockSpec((ts, C), lambda i: (i, col))
    nxt = pl.BlockSpec((CONV_HALO, C), lambda i: (jnp.minimum((i + 1) * hb, last_halo), 0))
    return pl.pallas_call(
        body, name=name, grid=(nblk,),
        in_specs=[row(0), nxt, row(0), row(1), pl.BlockSpec((K, C), lambda i: (0, 0))],
        out_specs=[row(0), row(0), pl.BlockSpec((K, SUBLANES, C), lambda i: (0, 0, 0)),
                   pl.BlockSpec((2, SUBLANES, C), lambda i: (0, 0, 0))],
        out_shape=[jax.ShapeDtypeStruct((S, C), BF16), jax.ShapeDtypeStruct((S, C), BF16),
                   jax.ShapeDtypeStruct((K, SUBLANES, C), F32), jax.ShapeDtypeStruct((2, SUBLANES, C), F32)],
        scratch_shapes=[pltpu.VMEM((nrows, C), F32), pltpu.VMEM((SUBLANES, nrows, C), F32)],
        compiler_params=_params("arbitrary"),
    )(du1, du1, proj, proj, dw)


def _adamw(parts, w, m, v, name):
    R, C = w.shape
    tr = _tile(R, 256)
    c1 = 1.0 - ADAM_B1 ** ADAM_STEP
    c2 = 1.0 - ADAM_B2 ** ADAM_STEP

    def body(p_ref, w_ref, m_ref, v_ref, g_ref, d_ref, nm_ref, nv_ref):
        g = p_ref[0].astype(F32)
        for d in range(1, N_DEV):
            g = g + p_ref[d].astype(F32)
        nm = ADAM_B1 * m_ref[...] + (1.0 - ADAM_B1) * g
        nv = ADAM_B2 * v_ref[...] + (1.0 - ADAM_B2) * (g * g)
        g_ref[...] = g
        nm_ref[...] = nm
        nv_ref[...] = nv
        d_ref[...] = -ADAM_LR * ((nm / c1) / (jnp.sqrt(nv / c2) + ADAM_EPS) + ADAM_WD * w_ref[...])

    row = pl.BlockSpec((tr, C), lambda i: (i, 0))
    out = jax.ShapeDtypeStruct((R, C), F32)
    return pl.pallas_call(
        body, name=name, grid=(R // tr,),
        in_specs=[pl.BlockSpec((N_DEV, tr, C), lambda i: (0, i, 0)), row, row, row],
        out_specs=[row, row, row, row], out_shape=[out, out, out, out],
        compiler_params=_params("parallel"),
    )(parts, w, m, v)


def _pad_lanes(a, width=LANES):
    return jnp.pad(a, ((0, 0), (0, width - a.shape[1])))


def _flat_rows(parts, width=LANES):
    flat = jnp.concatenate([p.reshape(-1) for p in parts])
    rows = -(-flat.shape[0] // width)
    rows = -(-rows // SUBLANES) * SUBLANES
    return jnp.pad(flat, (0, rows * width - flat.shape[0])).reshape(rows, width)


def _unflat(rows2d, shapes):
    flat = rows2d.reshape(-1)
    out, pos = [], 0
    for s in shapes:
        n = int(np.prod(s))
        out.append(flat[pos:pos + n].reshape(s))
        pos += n
    return out


def _pack_shards(parts, axes):
    rows = []
    for a, ax in zip(parts, axes):
        sh = a.shape
        a = a.reshape(sh[:ax] + (N_DEV, sh[ax] // N_DEV) + sh[ax + 1:])
        rows.append(jnp.moveaxis(a, ax, 0).reshape(N_DEV, -1))
    return jnp.concatenate(rows, axis=1)


def _unpack_shards(packed, shard_shapes, axes):
    out, pos = [], 0
    for s, ax in zip(shard_shapes, axes):
        n = int(np.prod(s))
        a = jnp.moveaxis(packed[:, pos:pos + n].reshape((N_DEV,) + tuple(s)), 0, ax)
        out.append(a.reshape(tuple(s[:ax]) + (N_DEV * s[ax],) + tuple(s[ax + 1:])))
        pos += n
    return out


def kernel(x, norm_g, fox_w_in, fox_b_f, fox_w_out, conv_w_in, conv_b_in, conv_dw, conv_dw_b, conv_ln_g, conv_ln_b, conv_w_out, final_norm_g, loss_target, m_norm_g, m_fox_w_in, m_fox_b_f, m_fox_w_out, m_conv_w_in, m_conv_b_in, m_conv_dw, m_conv_dw_b, m_conv_ln_g, m_conv_ln_b, m_conv_w_out, m_final_norm_g, v_norm_g, v_fox_w_in, v_fox_b_f, v_fox_w_out, v_conv_w_in, v_conv_b_in, v_conv_dw, v_conv_dw_b, v_conv_ln_g, v_conv_ln_b, v_conv_w_out, v_final_norm_g):
    h0 = x[0]
    target = loss_target[0]
    S, D = h0.shape
    depth = norm_g.shape[0]
    n_fox, _, fin_shard = fox_w_in.shape
    n_conv, _, cin_shard = conv_w_in.shape
    heads = fox_b_f.shape[1]
    W = fox_w_out.shape[1] * N_DEV
    C = conv_w_out.shape[1] * N_DEV
    assert fin_shard * N_DEV == 4 * W + heads and cin_shard * N_DEV == 3 * C and heads <= LANES
    is_fox = lambda i: i % 2 == 0

    shards = {}
    for i in range(depth):
        j = i // 2
        w_in, w_out = (fox_w_in, fox_w_out) if is_fox(i) else (conv_w_in, conv_w_out)
        shards[i] = (w_in[j].astype(BF16), w_out[j].astype(BF16))
    split = (11 * D) // 16

    def carried(*args, carry=None, **kw):
        r = _matmul(*args, carry=carry, **kw)
        return r if carry is not None else (r, None)

    small_shapes = [conv_b_in.shape, conv_dw.shape, conv_dw_b.shape, conv_ln_g.shape, conv_ln_b.shape]
    small = _flat_rows([conv_b_in, conv_dw, conv_dw_b, conv_ln_g, conv_ln_b])
    small_all = _exchange(small, "ag_small", "gather")
    small_axes = [1, 2, 1, 1, 1]
    conv_b_in_f, conv_dw_f, conv_dw_b_f, conv_ln_g_f, conv_ln_b_f = _unpack_shards(
        small_all.reshape(N_DEV, -1), small_shapes, small_axes)

    gathered_in = {0: _exchange(shards[0][0], "ag_w_in0", "gather")}
    gathered_out = {}

    h = h0
    saved = []
    for i in range(depth):
        j = i // 2
        has_next = i + 1 < depth
        gather = lambda x, kind="gather": (x, kind) if has_next else None
        hn, hn_t = _rms_fwd(h, norm_g[i:i + 1], f"rms_fwd{i}")
        w = gathered_in.pop(i)
        if is_fox(i):
            w = jnp.transpose(w, (1, 0, 2)).reshape(D, -1)
            w_qkvg, w_f = w[:, :4 * W], _pad_lanes(w[:, 4 * W:])
            qkv, got_in = carried(hn, w_qkvg, "nn", BF16, f"fox_qkv{i}", n_out=3 * W,
                                  carry=gather(shards[i + 1][0] if has_next else None, "gather_cols"))
            gate, got = carried(hn, w_qkvg, "nn", F32, f"fox_gate{i}", b_col_off=3 * W, n_out=W,
                                carry=(shards[i][1], "gather"))
            w_out = got.reshape(W, D)
            f = _matmul(hn, w_f, "nn", F32, f"fox_f{i}")
            b_f = _pad_lanes(fox_b_f[j:j + 1])
            c = _fgate_fwd(f, b_f, f"fgate_fwd{i}")
            key_bias = lambda t: (-c[:, :heads]).T.reshape(heads, S // t, 1, t)
            y, o, lse = _flash_fwd(qkv, gate, key_bias(_tile(S, FLASH_FWD_TILE)), heads, f"flash_fwd{i}")
            cneg = key_bias(_tile(S, FLASH_BWD_TILE))
            h_next, got_out = carried(y, w_out, "nn", F32, f"out_proj{i}", add=h,
                                      carry=gather(shards[i + 1][1] if has_next else None))
            if has_next:
                gathered_in[i + 1], gathered_out[i + 1] = got_in, got_out
            saved.append(dict(h=h, hn_t=hn_t, qkv=qkv, gate=gate, f=f, b_f=b_f, cneg=cneg, y=y, o=o, lse=lse,
                              w_qkvg=w_qkvg, w_f=w_f, w_out=w_out))
        else:
            w_out = gathered_out.pop(i).reshape(C, D)
            nxt = shards[i + 1][0] if has_next else None
            proj, got_a = carried(hn, w, "nn", F32, f"conv_in{i}", bias=conv_b_in_f[j:j + 1],
                                  carry=gather(nxt[:split] if has_next else None))
            y, u1 = _conv_fwd(proj, conv_dw_f[j], conv_dw_b_f[j:j + 1], conv_ln_g_f[j:j + 1],
                              conv_ln_b_f[j:j + 1], f"conv_fwd{i}")
            h_next, got_b = carried(y, w_out, "nn", F32, f"out_proj{i}", add=h,
                                    carry=gather(nxt[split:] if has_next else None))
            if has_next:
                gathered_in[i + 1] = jnp.concatenate([got_a, got_b], axis=1)
            saved.append(dict(h=h, hn_t=hn_t, proj=proj, y=y, u1=u1, w_in=w, w_out=w_out))
        h = h_next

    dh, dh16, dh16_t, loss_part, dg_final = _loss_head(h, final_norm_g[None, :], target, "loss_head")

    def shard_cols(g, shard):
        return jnp.transpose(g.reshape(g.shape[0], N_DEV, shard), (1, 0, 2))

    d_norm_g = [None] * depth
    d_fox_b_f = [None] * n_fox
    d_conv_small = [None] * n_conv
    summed_in = [None] * depth
    summed_out = [None] * depth
    pend_in = pend_out = None
    early_in = None
    for i in reversed(range(depth)):
        j = i // 2
        sv = saved[i]
        scatter = lambda x, kind="scatter": (x, kind) if x is not None else None
        dy = _matmul(dh16, sv["w_out"], "nt", F32, f"d_out_proj{i}")
        dw_out_t, got = carried(dh16_t, sv["y"], "nn", F32, f"dw_out{i}", carry=scatter(pend_out))
        if pend_out is not None:
            summed_out[i + 1] = got
        own_out = dw_out_t.T.reshape(N_DEV, -1, D).astype(BF16)
        if is_fox(i):
            do, dgate, delta = _fox_gate_bwd(dy, sv["o"], sv["gate"], f"fox_gate_bwd{i}")
            dq, dk, dv, colsum, rowsum = _flash_bwd(sv["qkv"], do, delta, sv["lse"], sv["cneg"], heads,
                                                    f"flash_bwd{i}")
            dc = _pad_lanes(rowsum[:, :heads] - jnp.sum(colsum, axis=2).reshape(heads, S).T)
            df, dbf = _fgate_bwd(dc, sv["f"], sv["b_f"], f"fgate_bwd{i}")
            dproj = jnp.concatenate([dq, dk, dv, dgate], axis=1)
            dhn = _matmul(df, sv["w_f"], "nt", F32, f"d_fox_f{i}")
            dhn, got = carried(dproj, sv["w_qkvg"], "nt", F32, f"d_fox_in{i}", add=dhn,
                               carry=scatter(pend_in, "scatter_cols"))
            if pend_in is not None:
                summed_in[i + 1] = got
            dw_f = _matmul(sv["hn_t"], df, "nn", F32, f"dw_fox_f{i}")[:, :heads]
            parts_of = lambda dw, rows: shard_cols(jnp.concatenate([dw, dw_f[rows]], axis=1), fin_shard).astype(BF16)
            if i > 0:
                dw_qkvg, summed_out[i] = carried(sv["hn_t"], dproj, "nn", F32, f"dw_fox_in{i}",
                                                 carry=(own_out, "scatter"))
                pend_in = parts_of(dw_qkvg, slice(None))
            else:
                top, bottom = slice(0, D // 2), slice(D // 2, D)
                dw_top, summed_out[i] = carried(sv["hn_t"][top], dproj, "nn", F32, f"dw_fox_in{i}",
                                                carry=(own_out, "scatter"))
                dw_bottom, early_in = carried(sv["hn_t"][bottom], dproj, "nn", F32, f"dw_fox_in{i}_rest",
                                              carry=(parts_of(dw_top, top), "scatter"))
                pend_in = parts_of(dw_bottom, bottom)
            pend_out = None
            d_fox_b_f[j] = jnp.sum(dbf, axis=0)[:heads]
        else:
            du1, dgate, nsums = _conv_bwd_norm(dy, sv["proj"], sv["u1"], conv_ln_g_f[j:j + 1],
                                               conv_ln_b_f[j:j + 1], f"conv_bwd_norm{i}")
            da, db, ddw, absums = _conv_bwd_taps(du1, sv["proj"], conv_dw_f[j], f"conv_bwd_taps{i}")
            dproj = jnp.concatenate([da, db, dgate], axis=1)
            half = D // 2
            dhn, got_a = carried(dproj, sv["w_in"], "nt", F32, f"d_conv_in{i}",
                                 carry=scatter(pend_in[:, :half] if pend_in is not None else None))
            dw_in, got_b = carried(sv["hn_t"], dproj, "nn", BF16, f"dw_conv_in{i}",
                                   carry=scatter(pend_in[:, half:] if pend_in is not None else None))
            if pend_in is not None:
                summed_in[i + 1] = jnp.concatenate([got_a, got_b], axis=1)
            pend_in, pend_out = dw_in, own_out
            nsum = jnp.sum(nsums, axis=1)
            absum = jnp.sum(absums, axis=1)
            d_conv_small[j] = dict(b_in=jnp.concatenate([absum[0], absum[1], nsum[3]]),
                                   dw=jnp.sum(ddw, axis=1), dw_b=nsum[2], ln_g=nsum[0], ln_b=nsum[1])
        dh, dh16, dh16_t, dg = _rms_bwd(sv["h"], norm_g[i:i + 1], dhn, dh, f"rms_bwd{i}")
        d_norm_g[i] = jnp.sum(dg, axis=0)
    summed_in[0] = _exchange(pend_in, "rs_w_in0", "scatter")
    if early_in is not None:
        summed_in[0] = jnp.concatenate([early_in, summed_in[0]], axis=1)
    if pend_out is not None:
        summed_out[0] = _exchange(pend_out, "rs_w_out0", "scatter")
    grad_x = dh[None]

    p_small = _pack_shards([jnp.stack([s[name] for s in d_conv_small]) for name in ("b_in", "dw", "dw_b", "ln_g", "ln_b")],
                           small_axes)
    p_small = jnp.pad(p_small, ((0, 0), (0, small.size - p_small.shape[1]))).reshape((N_DEV,) + small.shape)
    r_small = _exchange(p_small, "rs_small", "scatter")

    rep_shapes = [norm_g.shape, fox_b_f.shape, final_norm_g.shape, (1,)]
    rep_part = _flat_rows([jnp.stack(d_norm_g), jnp.stack(d_fox_b_f), jnp.sum(dg_final, axis=0),
                           jnp.sum(loss_part[:, 0])[None]])
    r_rep = _exchange(rep_part, "ag_replicated", "gather")

    def update(parts, w, m, v, name):
        two_d = (-1, w.shape[-1])
        stacked = jnp.stack(parts, axis=1).reshape((N_DEV,) + w.reshape(two_d).shape)
        res = _adamw(stacked, w.reshape(two_d), m.reshape(two_d), v.reshape(two_d), name)
        return [r.reshape(w.shape) for r in res]

    fox_layers = [i for i in range(depth) if is_fox(i)]
    conv_layers = [i for i in range(depth) if not is_fox(i)]
    u_fin = update([summed_in[i] for i in fox_layers], fox_w_in, m_fox_w_in, v_fox_w_in, "adamw_fox_w_in")
    u_fout = update([summed_out[i] for i in fox_layers], fox_w_out, m_fox_w_out, v_fox_w_out, "adamw_fox_w_out")
    u_cin = update([summed_in[i] for i in conv_layers], conv_w_in, m_conv_w_in, v_conv_w_in, "adamw_conv_w_in")
    u_cout = update([summed_out[i] for i in conv_layers], conv_w_out, m_conv_w_out, v_conv_w_out,
                    "adamw_conv_w_out")
    u_small = _adamw(r_small, small,
                     _flat_rows([m_conv_b_in, m_conv_dw, m_conv_dw_b, m_conv_ln_g, m_conv_ln_b]),
                     _flat_rows([v_conv_b_in, v_conv_dw, v_conv_dw_b, v_conv_ln_g, v_conv_ln_b]), "adamw_small")
    zero1 = jnp.zeros((1,), F32)
    u_rep = _adamw(r_rep, _flat_rows([norm_g, fox_b_f, final_norm_g, zero1]),
                   _flat_rows([m_norm_g, m_fox_b_f, m_final_norm_g, zero1]),
                   _flat_rows([v_norm_g, v_fox_b_f, v_final_norm_g, zero1]), "adamw_replicated")

    outs = []
    loss = None
    for kind in range(4):
        b_in_k, dw_k, dwb_k, lng_k, lnb_k = _unflat(u_small[kind], small_shapes)
        ng_k, bf_k, fg_k, loss_k = _unflat(u_rep[kind], rep_shapes)
        if kind == 0:
            loss = loss_k[0]
        outs += [ng_k, u_fin[kind], bf_k, u_fout[kind], u_cin[kind], b_in_k, dw_k, dwb_k, lng_k, lnb_k,
                 u_cout[kind], fg_k]
    return (loss, grad_x, *outs)
```
